```python
import math
import jax, jax.numpy as jnp
from jax import lax
import numpy as np

D_MODEL = 2048
BATCH = 8
SEQ = 4096
DEPTH = 2

ROPE_THETA = 10000.0
NORM_EPS = 1e-6
MASK_VALUE = -1e30
FORCE_SCORE = 1e6
BAND_BLOCK = 128
N_BRANCHES = 3

NSA_HEADS = 8
NSA_KV_HEADS = 2
NSA_HEAD_DIM = 128
NSA_N_BRANCH = 3
NSA_CMP_STRIDE = 16
NSA_CMP_LEN = 2 * NSA_CMP_STRIDE
NSA_CMP_HIDDEN = 256
NSA_SLC_BLOCK = 64
NSA_SLC_TOPK = 16
NSA_SLC_Q_BLOCK = 32
NSA_WINDOW = 512

SSM_D_INNER = D_MODEL // 2
SSM_HEAD_DIM = 64
SSM_HEADS = SSM_D_INNER // SSM_HEAD_DIM
SSM_GROUPS = 2
SSM_D_STATE = 128
SSM_CONV = 4
SSM_CHUNK = 128

SWA_HEADS = 16
SWA_KV_HEADS = 2
SWA_HEAD_DIM = 64
SWA_WINDOW = 128

MOE_GROUPS = 4
MOE_EXPERTS_PER_GROUP = 8
MOE_TOPK = 2
MOE_D_FF = 512

NSA_Q_DIM = NSA_HEADS * NSA_HEAD_DIM
NSA_KV_DIM = NSA_N_BRANCH * 2 * NSA_KV_HEADS * NSA_HEAD_DIM
NSA_GATE_DIM = NSA_HEADS * NSA_N_BRANCH
SSM_CONV_DIM = SSM_D_INNER + 2 * SSM_GROUPS * SSM_D_STATE
SWA_Q_DIM = SWA_HEADS * SWA_HEAD_DIM
SWA_KV_DIM = 2 * SWA_KV_HEADS * SWA_HEAD_DIM
IN_SPLITS = (NSA_Q_DIM, NSA_KV_DIM, NSA_GATE_DIM, SSM_D_INNER, SSM_CONV_DIM, SSM_HEADS,
             SWA_Q_DIM, SWA_KV_DIM, N_BRANCHES * D_MODEL)
IN_DIM = sum(IN_SPLITS)

kernel_name = 'hybrid_nsa_ssd_swa_hmoe'


def rmsnorm(x, w):
    xf = x.astype(jnp.float32)
    y = xf * lax.rsqrt(jnp.mean(xf * xf, axis=-1, keepdims=True) + NORM_EPS)
    return (y * w.astype(jnp.float32)).astype(x.dtype)


def rope_tables(seq, dim):
    inv = 1.0 / (ROPE_THETA ** (jnp.arange(0, dim, 2, dtype=jnp.float32) / dim))
    ang = jnp.arange(seq, dtype=jnp.float32)[:, None] * inv[None, :]
    return jnp.cos(ang), jnp.sin(ang)


def apply_rope(x, cos, sin):
    half = x.shape[-1] // 2
    shape = (1, x.shape[1]) + (1,) * (x.ndim - 3) + (half,)
    c, s = cos.reshape(shape), sin.reshape(shape)
    xf = x.astype(jnp.float32)
    x1, x2 = xf[..., :half], xf[..., half:]
    return jnp.concatenate([x1 * c - x2 * s, x2 * c + x1 * s], axis=-1).astype(x.dtype)


def banded_attention(q, k, v, window, sinks=None):
    B, S, G, R, dh = q.shape
    nb = S // BAND_BLOCK
    span = window + BAND_BLOCK
    pad = ((0, 0), (window, 0), (0, 0), (0, 0))
    kp, vp = jnp.pad(k, pad), jnp.pad(v, pad)
    scale = dh ** -0.5
    oq = jnp.arange(BAND_BLOCK)
    ok = jnp.arange(span)
    diff = (oq[:, None] + window) - ok[None, :]

    def block(b):
        start = b * BAND_BLOCK
        qb = lax.dynamic_slice_in_dim(q, start, BAND_BLOCK, axis=1)
        kb = lax.dynamic_slice_in_dim(kp, start, span, axis=1)
        vb = lax.dynamic_slice_in_dim(vp, start, span, axis=1)
        s = jnp.einsum('bqgrd,bkgd->bgrqk', qb, kb).astype(jnp.float32) * scale
        kpos = start - window + ok
        mask = (diff >= 0) & (diff < window) & (kpos[None, :] >= 0)
        s = jnp.where(mask, s, MASK_VALUE)
        if sinks is None:
            p = jax.nn.softmax(s, axis=-1)
        else:
            sk = sinks.astype(jnp.float32)[None, :, :, None, None]
            m = jnp.maximum(jnp.max(s, axis=-1, keepdims=True), sk)
            e = jnp.exp(s - m)
            p = e / (jnp.sum(e, axis=-1, keepdims=True) + jnp.exp(sk - m))
        return jnp.einsum('bgrqk,bkgd->bqgrd', p.astype(vb.dtype), vb)

    out = lax.map(block, jnp.arange(nb))
    return jnp.moveaxis(out, 0, 1).reshape(B, S, G, R, dh)


def nsa_mixer(q_cols, kv_cols, gate_cols, cos, sin, cmp_pos, cmp_w1, cmp_b1, cmp_w2):
    B, S, _ = q_cols.shape
    G = NSA_KV_HEADS
    R = NSA_HEADS // G
    dh = NSA_HEAD_DIM
    scale = dh ** -0.5
    q = q_cols.reshape(B, S, G, R, dh)
    kv = kv_cols.reshape(B, S, NSA_N_BRANCH, 2, G, dh)
    pos = jnp.arange(S)

    n_chunk = S // NSA_CMP_STRIDE
    n_cmp = n_chunk - 1
    kvc = jnp.moveaxis(kv[:, :, 0], 2, 0).reshape(2, B, n_chunk, NSA_CMP_STRIDE, G, dh)
    blocks = jnp.concatenate([kvc[:, :, :-1], kvc[:, :, 1:]], axis=3) + cmp_pos[:, None, None, :, None, :]
    flat = jnp.moveaxis(blocks, 4, 3).reshape(2, B, n_cmp, G, NSA_CMP_LEN * dh)
    hid = jax.nn.silu(jnp.einsum('cbngi,cih->cbngh', flat, cmp_w1) + cmp_b1[:, None, None, None, :])
    comp = jnp.einsum('cbngh,chd->cbngd', hid, cmp_w2)
    k_cmp, v_cmp = comp[0], comp[1]
    cmp_start = jnp.arange(n_cmp) * NSA_CMP_STRIDE
    cmp_end = cmp_start + NSA_CMP_LEN - 1
    vis = cmp_end[None, :] <= pos[:, None]
    s = jnp.einsum('bsgrd,bngd->bgrsn', q, k_cmp).astype(jnp.float32) * scale
    p_cmp = jax.nn.softmax(jnp.where(vis, s, MASK_VALUE), axis=-1) * vis
    o_cmp = jnp.einsum('bgrsn,bngd->bsgrd', p_cmp.astype(v_cmp.dtype), v_cmp)

    n_sel = S // NSA_SLC_BLOCK
    sel_start = jnp.arange(n_sel) * NSA_SLC_BLOCK
    overlap = ((cmp_start[:, None] <= sel_start[None, :] + NSA_SLC_BLOCK - 1)
               & (cmp_end[:, None] >= sel_start[None, :])).astype(jnp.float32)
    imp = jnp.einsum('bgrsn,nj->bgsj', p_cmp, overlap)
    q_blk = pos // NSA_SLC_BLOCK
    jj = jnp.arange(n_sel)
    causal_blk = sel_start[None, :] <= pos[:, None]
    forced = (jj[None, :] == 0) | (jj[None, :] == q_blk[:, None]) | (jj[None, :] == q_blk[:, None] - 1)
    imp = jnp.where(causal_blk, imp, MASK_VALUE)
    imp = jnp.where(forced, FORCE_SCORE, imp)
    topk = min(NSA_SLC_TOPK, n_sel)
    _, idx = lax.top_k(imp, topk)

    q_r = apply_rope(q, cos, sin)
    k_slc = apply_rope(kv[:, :, 1, 0], cos, sin)
    k_win = apply_rope(kv[:, :, 2, 0], cos, sin)
    kb = k_slc.reshape(B, n_sel, NSA_SLC_BLOCK, G, dh).transpose(0, 3, 1, 2, 4)
    vb = kv[:, :, 1, 1].reshape(B, n_sel, NSA_SLC_BLOCK, G, dh).transpose(0, 3, 1, 2, 4)
    gather = jax.vmap(jax.vmap(lambda a, i: a[i]))
    Qc = NSA_SLC_Q_BLOCK
    in_blk = jnp.arange(NSA_SLC_BLOCK)

    def slc_block(c):
        start = c * Qc
        qc = lax.dynamic_slice_in_dim(q_r, start, Qc, axis=1)
        ic = lax.dynamic_slice_in_dim(idx, start, Qc, axis=2)
        kg = gather(kb, ic)
        vg = gather(vb, ic)
        sc = jnp.einsum('bqgrd,bgqkjd->bgrqkj', qc, kg).astype(jnp.float32) * scale
        kpos = ic[..., None] * NSA_SLC_BLOCK + in_blk
        qpos = start + jnp.arange(Qc)
        m = kpos <= qpos[None, None, :, None, None]
        sc = jnp.where(m[:, :, None], sc, MASK_VALUE)
        p = jax.nn.softmax(sc.reshape(sc.shape[:4] + (-1,)), axis=-1).reshape(sc.shape)
        return jnp.einsum('bgrqkj,bgqkjd->bqgrd', p.astype(vg.dtype), vg)

    o_slc = lax.map(slc_block, jnp.arange(S // Qc))
    o_slc = jnp.moveaxis(o_slc, 0, 1).reshape(B, S, G, R, dh)

    o_win = banded_attention(q_r, k_win, kv[:, :, 2, 1], NSA_WINDOW)

    g = jax.nn.sigmoid(gate_cols.astype(jnp.float32)).astype(q.dtype).reshape(B, S, G, R, NSA_N_BRANCH)
    o = g[..., 0:1] * o_cmp + g[..., 1:2] * o_slc + g[..., 2:3] * o_win
    return o.reshape(B, S, NSA_Q_DIM)


def ssd_mixer(z, xbc, dt, conv_w, conv_b, dt_bias, a_log, d_skip, norm_w):
    B, S, _ = z.shape
    C = xbc.shape[-1]
    xbc = lax.conv_general_dilated(xbc, conv_w[:, None, :], window_strides=(1,),
                                   padding=[(SSM_CONV - 1, 0)],
                                   dimension_numbers=('NWC', 'WIO', 'NWC'),
                                   feature_group_count=C) + conv_b
    xbc = jax.nn.silu(xbc)
    G, N, P, L = SSM_GROUPS, SSM_D_STATE, SSM_HEAD_DIM, SSM_CHUNK
    R = SSM_HEADS // G
    nc = S // L
    xs = xbc[..., :SSM_D_INNER].reshape(B, S, G, R, P).astype(jnp.float32)
    Bm = xbc[..., SSM_D_INNER:SSM_D_INNER + G * N].reshape(B, nc, L, G, N).astype(jnp.float32)
    Cm = xbc[..., SSM_D_INNER + G * N:].reshape(B, nc, L, G, N).astype(jnp.float32)
    dt = jax.nn.softplus(dt.astype(jnp.float32) + dt_bias.astype(jnp.float32)).reshape(B, S, G, R)
    A = -jnp.exp(a_log.astype(jnp.float32)).reshape(G, R)
    X = (xs * dt[..., None]).reshape(B, nc, L, G, R, P)
    a_cs = jnp.cumsum((dt * A).reshape(B, nc, L, G, R), axis=2)

    seg = a_cs[:, :, :, None] - a_cs[:, :, None, :]
    causal = jnp.tril(jnp.ones((L, L), dtype=bool))[None, None, :, :, None, None]
    decay = jnp.exp(jnp.where(causal, seg, -jnp.inf))
    cb = jnp.einsum('bclgn,bcsgn->bclsg', Cm, Bm)
    y_diag = jnp.einsum('bclsg,bclsgr,bcsgrp->bclgrp', cb, decay, X)

    decay_to_end = jnp.exp(a_cs[:, :, -1:] - a_cs)
    states = jnp.einsum('bclgn,bclgr,bclgrp->bcgrpn', Bm, decay_to_end, X)
    chunk_decay = jnp.exp(a_cs[:, :, -1])

    def step(h, inp):
        st, dec = inp
        return h * dec[..., None, None] + st, h

    h0 = jnp.zeros((B, G, R, P, N), jnp.float32)
    _, h_in = lax.scan(step, h0, (jnp.moveaxis(states, 1, 0), jnp.moveaxis(chunk_decay, 1, 0)))
    h_in = jnp.moveaxis(h_in, 0, 1)
    y_off = jnp.einsum('bclgn,bcgrpn,bclgr->bclgrp', Cm, h_in, jnp.exp(a_cs))

    y = (y_diag + y_off).reshape(B, S, G, R, P) + xs * d_skip.astype(jnp.float32).reshape(G, R)[..., None]
    y = y.reshape(B, S, SSM_D_INNER) * jax.nn.silu(z.astype(jnp.float32))
    yg = y.reshape(B, S, G, -1)
    yg = yg * lax.rsqrt(jnp.mean(yg * yg, axis=-1, keepdims=True) + NORM_EPS)
    y = yg.reshape(B, S, SSM_D_INNER) * norm_w.astype(jnp.float32)
    return y.astype(z.dtype)


def swa_mixer(q_cols, kv_cols, cos, sin, sinks):
    B, S, _ = q_cols.shape
    G = SWA_KV_HEADS
    R = SWA_HEADS // G
    dh = SWA_HEAD_DIM
    q = apply_rope(q_cols.reshape(B, S, SWA_HEADS, dh), cos, sin).reshape(B, S, G, R, dh)
    kv = kv_cols.reshape(B, S, 2, G, dh)
    k = apply_rope(kv[:, :, 0], cos, sin)
    o = banded_attention(q, k, kv[:, :, 1], SWA_WINDOW, sinks.reshape(G, R))
    return o.reshape(B, S, SWA_Q_DIM)


def mixer_block(h, cos_a, sin_a, cos_c, sin_c, w_in, nsa_cmp_pos, nsa_cmp_w1, nsa_cmp_b1, nsa_cmp_w2,
                ssm_conv_w, ssm_conv_b, ssm_dt_bias, ssm_a_log, ssm_d, ssm_norm, swa_sinks,
                proj_nsa, proj_ssm, proj_swa, w_out):
    B, S, D = h.shape
    u = h @ w_in
    (nsa_q, nsa_kv, nsa_g, ssm_z, ssm_xbc, ssm_dt, swa_q, swa_kv, merge_g) = jnp.split(
        u, np.cumsum(IN_SPLITS)[:-1].tolist(), axis=-1)
    o_a = nsa_mixer(nsa_q, nsa_kv, nsa_g, cos_a, sin_a, nsa_cmp_pos, nsa_cmp_w1, nsa_cmp_b1, nsa_cmp_w2)
    o_b = ssd_mixer(ssm_z, ssm_xbc, ssm_dt, ssm_conv_w, ssm_conv_b, ssm_dt_bias, ssm_a_log, ssm_d, ssm_norm)
    o_c = swa_mixer(swa_q, swa_kv, cos_c, sin_c, swa_sinks)
    gate = jax.nn.sigmoid(merge_g.astype(jnp.float32)).astype(h.dtype).reshape(B, S, N_BRANCHES, D)
    y = gate[:, :, 0] * (o_a @ proj_nsa) + gate[:, :, 1] * (o_b @ proj_ssm) + gate[:, :, 2] * (o_c @ proj_swa)
    return y @ w_out


def hier_moe(h, group_router, group_bias, expert_router, expert_bias, w_gate, w_up, w_down):
    B, S, D = h.shape
    t = h.reshape(-1, D)
    g_prob = jax.nn.softmax((t @ group_router).astype(jnp.float32) + group_bias.astype(jnp.float32), axis=-1)
    g_w, g_idx = lax.top_k(g_prob, 1)
    g_onehot = jax.nn.one_hot(g_idx[:, 0], MOE_GROUPS, dtype=jnp.float32)
    e_logits = ((t @ expert_router).astype(jnp.float32) + expert_bias.astype(jnp.float32)).reshape(
        -1, MOE_GROUPS, MOE_EXPERTS_PER_GROUP)
    e_sel = jnp.sum(e_logits * g_onehot[:, :, None], axis=1)
    e_val, e_idx = lax.top_k(e_sel, MOE_TOPK)
    e_w = jax.nn.softmax(e_val, axis=-1)
    in_group = jnp.sum(e_w[..., None] * jax.nn.one_hot(e_idx, MOE_EXPERTS_PER_GROUP, dtype=jnp.float32), axis=1)
    comb = g_onehot[:, :, None] * (g_w * in_group)[:, None, :]
    out = jnp.zeros(t.shape, jnp.float32)
    for g in range(MOE_GROUPS):
        hg = jax.nn.silu(jnp.einsum('td,edf->tef', t, w_gate[g])) * jnp.einsum('td,edf->tef', t, w_up[g])
        hg = hg * comb[:, g, :, None].astype(hg.dtype)
        out = out + jnp.einsum('tef,efd->td', hg, w_down[g]).astype(jnp.float32)
    return out.astype(h.dtype).reshape(B, S, D)


def setup_inputs(seed: int = 0) -> dict:
    key = jax.random.key(seed)
    k = jax.random.split(key, 27)
    f32 = jnp.float32
    L = DEPTH
    NE = MOE_EXPERTS_PER_GROUP

    def nrm(kk, shape, scale):
        return jax.random.normal(kk, shape, f32) * scale

    dt = jnp.exp(jax.random.uniform(k[10], (L, SSM_HEADS), f32, math.log(1e-3), math.log(1e-1)))
    return {
        'x': nrm(k[0], (BATCH, SEQ, D_MODEL), 1.0),
        'norm_mix': 1.0 + nrm(k[1], (L, D_MODEL), 0.05),
        'norm_ffn': 1.0 + nrm(k[2], (L, D_MODEL), 0.05),
        'w_in': nrm(k[3], (L, D_MODEL, IN_DIM), D_MODEL ** -0.5),
        'nsa_cmp_pos': nrm(k[4], (L, 2, NSA_CMP_LEN, NSA_HEAD_DIM), 0.1),
        'nsa_cmp_w1': nrm(k[5], (L, 2, NSA_CMP_LEN * NSA_HEAD_DIM, NSA_CMP_HIDDEN), (NSA_CMP_LEN * NSA_HEAD_DIM) ** -0.5),
        'nsa_cmp_b1': nrm(k[6], (L, 2, NSA_CMP_HIDDEN), 0.01),
        'nsa_cmp_w2': nrm(k[7], (L, 2, NSA_CMP_HIDDEN, NSA_HEAD_DIM), NSA_CMP_HIDDEN ** -0.5),
        'ssm_conv_w': nrm(k[8], (L, SSM_CONV, SSM_CONV_DIM), SSM_CONV ** -0.5),
        'ssm_conv_b': nrm(k[9], (L, SSM_CONV_DIM), 0.01),
        'ssm_dt_bias': dt + jnp.log(-jnp.expm1(-dt)),
        'ssm_a_log': jnp.log(jax.random.uniform(k[11], (L, SSM_HEADS), f32, 1.0, 16.0)),
        'ssm_d': 1.0 + nrm(k[12], (L, SSM_HEADS), 0.1),
        'ssm_norm': 1.0 + nrm(k[13], (L, SSM_D_INNER), 0.05),
        'swa_sinks': nrm(k[14], (L, SWA_HEADS), 0.5),
        'proj_nsa': nrm(k[15], (L, NSA_Q_DIM, D_MODEL), NSA_Q_DIM ** -0.5),
        'proj_ssm': nrm(k[16], (L, SSM_D_INNER, D_MODEL), SSM_D_INNER ** -0.5),
        'proj_swa': nrm(k[17], (L, SWA_Q_DIM, D_MODEL), SWA_Q_DIM ** -0.5),
        'w_out': nrm(k[18], (L, D_MODEL, D_MODEL), D_MODEL ** -0.5),
        'moe_group_router': nrm(k[19], (L, D_MODEL, MOE_GROUPS), D_MODEL ** -0.5),
        'moe_group_bias': nrm(k[20], (L, MOE_GROUPS), 0.01),
        'moe_expert_router': nrm(k[21], (L, D_MODEL, MOE_GROUPS * NE), D_MODEL ** -0.5),
        'moe_expert_bias': nrm(k[22], (L, MOE_GROUPS * NE), 0.01),
        'moe_w_gate': nrm(k[23], (L, MOE_GROUPS, NE, D_MODEL, MOE_D_FF), D_MODEL ** -0.5),
        'moe_w_up': nrm(k[24], (L, MOE_GROUPS, NE, D_MODEL, MOE_D_FF), D_MODEL ** -0.5),
        'moe_w_down': nrm(k[25], (L, MOE_GROUPS, NE, MOE_D_FF, D_MODEL), MOE_D_FF ** -0.5),
        'final_norm': 1.0 + nrm(k[26], (D_MODEL,), 0.05),
    }


def reference(x, norm_mix, norm_ffn, w_in, nsa_cmp_pos, nsa_cmp_w1, nsa_cmp_b1, nsa_cmp_w2,
              ssm_conv_w, ssm_conv_b, ssm_dt_bias, ssm_a_log, ssm_d, ssm_norm, swa_sinks,
              proj_nsa, proj_ssm, proj_swa, w_out, moe_group_router, moe_group_bias,
              moe_expert_router, moe_expert_bias, moe_w_gate, moe_w_up, moe_w_down, final_norm):
    S = x.shape[1]
    cos_a, sin_a = rope_tables(S, NSA_HEAD_DIM)
    cos_c, sin_c = rope_tables(S, SWA_HEAD_DIM)
    for l in range(DEPTH):
        h = rmsnorm(x, norm_mix[l])
        x = x + mixer_block(h, cos_a, sin_a, cos_c, sin_c, w_in[l], nsa_cmp_pos[l], nsa_cmp_w1[l],
                            nsa_cmp_b1[l], nsa_cmp_w2[l], ssm_conv_w[l], ssm_conv_b[l], ssm_dt_bias[l],
                            ssm_a_log[l], ssm_d[l], ssm_norm[l], swa_sinks[l], proj_nsa[l], proj_ssm[l],
                            proj_swa[l], w_out[l])
        h = rmsnorm(x, norm_ffn[l])
        x = x + hier_moe(h, moe_group_router[l], moe_group_bias[l], moe_expert_router[l],
                         moe_expert_bias[l], moe_w_gate[l], moe_w_up[l], moe_w_down[l])
    return rmsnorm(x, final_norm)
```

```python
import functools
import math

import jax
import jax.numpy as jnp
import numpy as np
from jax import lax
from jax.experimental import pallas as pl
from jax.experimental.pallas import tpu as pltpu

F32 = jnp.float32
BF16 = jnp.bfloat16

D_MODEL = 2048
ROPE_THETA = 10000.0
NORM_EPS = 1e-6
MASK_VALUE = -1e30
FORCE_SCORE = 1e6

NSA_HEADS = 8
NSA_KV_HEADS = 2
NSA_HEAD_DIM = 128
NSA_CMP_STRIDE = 16
NSA_CMP_LEN = 32
NSA_CMP_HIDDEN = 256
NSA_SLC_BLOCK = 64
NSA_SLC_TOPK = 16
NSA_WINDOW = 512
NSA_Q_DIM = NSA_HEADS * NSA_HEAD_DIM

SSM_D_INNER = 1024
SSM_HEAD_DIM = 64
SSM_HEADS = 16
SSM_GROUPS = 2
SSM_D_STATE = 128
SSM_CONV = 4
SSM_CHUNK = 128
SSM_BC_DIM = 2 * SSM_GROUPS * SSM_D_STATE

SWA_HEADS = 16
SWA_KV_HEADS = 2
SWA_HEAD_DIM = 64
SWA_WINDOW = 128
SWA_Q_DIM = SWA_HEADS * SWA_HEAD_DIM

MOE_GROUPS = 4
MOE_EXPERTS_PER_GROUP = 8
MOE_EXPERTS = MOE_GROUPS * MOE_EXPERTS_PER_GROUP
MOE_TOPK = 2
MOE_D_FF = 512

LANES = 128
ATT_Q_TILE = 128
SLC_K_TILE = 256
MOE_ROW_TILE = 256
VMEM_LIMIT = 56 * 1024 * 1024

N_MERGE = 0
N_Z = 6144
N_XS = 7168
N_BC = 8192
N_CMP = 8704
N_SLCV = 9216
N_WINV = 9472
N_SWAV = 9728
N_TOTAL = 9856
SMALL_GATE0 = 16


def _params(sem):
    return pltpu.CompilerParams(dimension_semantics=sem, vmem_limit_bytes=VMEM_LIMIT)


def _dot(a, b):
    return jnp.dot(a, b, preferred_element_type=F32)


def _dot_nt(a, b):
    return lax.dot_general(a, b, (((1,), (1,)), ((), ())), preferred_element_type=F32)


def _split_bf16(v, n):
    parts = []
    for _ in range(n):
        p = v.astype(BF16)
        parts.append(p)
        v = v - p.astype(F32)
    return parts


def _expand(v, e):
    hi, lo = _split_bf16(v, 2)
    return _dot(hi, e) + _dot(lo, e)


def _sigmoid(v):
    return 1.0 / (1.0 + jnp.exp(-v))


def _silu(v):
    return v * _sigmoid(v)


def _inproj_kernel(*refs, rope, has_scale, n_out):
    it = iter(refs)
    x_ref, nw_ref, w_ref = next(it), next(it), next(it)
    cs_ref = next(it) if has_scale else None
    tabs = [next(it) for _ in range({None: 0, 'a': 2, 'c': 3}[rope])]
    outs = [next(it) for _ in range(n_out)]
    hn_scr = next(it)

    @pl.when(pl.program_id(1) == 0)
    def _():
        x = x_ref[...]
        ms = jnp.mean(x * x, axis=-1, keepdims=True)
        hn_scr[...] = (x * lax.rsqrt(ms + NORM_EPS) * nw_ref[...]).astype(BF16)

    acc = _dot(hn_scr[...], w_ref[...])
    if has_scale:
        acc = acc * cs_ref[...]
    if rope is None:
        outs[0][...] = acc.astype(outs[0].dtype)
        return
    if n_out == 2:
        outs[1][...] = acc.astype(outs[1].dtype)
    for c in range(acc.shape[1] // LANES):
        a = acc[:, c * LANES:(c + 1) * LANES]
        if rope == 'a':
            r = a * tabs[0][...] + pltpu.roll(a, 64, 1) * tabs[1][...]
        else:
            r = (a * tabs[0][...] + pltpu.roll(a, 96, 1) * tabs[1][...]
                 + pltpu.roll(a, 32, 1) * tabs[2][...])
        outs[0][:, c * LANES:(c + 1) * LANES] = r.astype(outs[0].dtype)


def _inproj(x2d, nw, w, *, seq, tm, tn, rope=None, scale=None, tabs=(), out_dtypes=(BF16,)):
    T, D = x2d.shape
    N = w.shape[1]
    nrow = seq // tm
    in_specs = [pl.BlockSpec((tm, D), lambda i, j: (i, 0)),
                pl.BlockSpec((1, D), lambda i, j: (0, 0)),
                pl.BlockSpec((D, tn), lambda i, j: (0, j))]
    args = [x2d, nw, w]
    if scale is not None:
        in_specs.append(pl.BlockSpec((1, tn), lambda i, j: (0, j)))
        args.append(scale)
    for t in tabs:
        in_specs.append(pl.BlockSpec((tm, LANES), lambda i, j: (i % nrow, 0)))
        args.append(t)
    out_shape = tuple(jax.ShapeDtypeStruct((T, N), dt) for dt in out_dtypes)
    out_specs = tuple(pl.BlockSpec((tm, tn), lambda i, j: (i, j)) for _ in out_dtypes)
    kern = functools.partial(_inproj_kernel, rope=rope, has_scale=scale is not None,
                             n_out=len(out_dtypes))
    return pl.pallas_call(
        kern, grid=(T // tm, N // tn), in_specs=in_specs, out_specs=out_specs, out_shape=out_shape,
        scratch_shapes=[pltpu.VMEM((tm, D), BF16)],
        compiler_params=_params(("parallel", "arbitrary")), name="inproj_" + str(rope))(*args)


def _cmp_mlp_kernel(kv_ref, pos_ref, w1_ref, b1_ref, w2_ref, out_ref, f32_scr):
    nch = out_ref.shape[2]
    f32_scr[...] = kv_ref[...].astype(F32)
    first = jnp.zeros((nch, NSA_CMP_HIDDEN), F32)
    second = jnp.zeros((nch, NSA_CMP_HIDDEN), F32)
    for t in range(NSA_CMP_STRIDE):
        xt = f32_scr[pl.ds(t, nch, stride=NSA_CMP_STRIDE), :]
        first += _dot((xt + pos_ref[0, t:t + 1, :]).astype(BF16), w1_ref[0, t])
        t2 = NSA_CMP_STRIDE + t
        second += _dot((xt + pos_ref[0, t2:t2 + 1, :]).astype(BF16), w1_ref[0, t2])
    hid = _silu(first + pltpu.roll(second, nch - 1, 0) + b1_ref[0])
    out_ref[0, 0] = _dot(hid.astype(BF16), w2_ref[0]).astype(out_ref.dtype)


def _cmp_mlp(n_out, pos, w1r, b1, w2, *, batch, seq):
    nch = seq // NSA_CMP_STRIDE
    cb0 = N_CMP // LANES
    return pl.pallas_call(
        _cmp_mlp_kernel, grid=(batch, 4),
        in_specs=[pl.BlockSpec((seq, LANES), lambda b, c: (b, cb0 + c)),
                  pl.BlockSpec((1, NSA_CMP_LEN, LANES), lambda b, c: (c // 2, 0, 0)),
                  pl.BlockSpec((1, NSA_CMP_LEN, LANES, NSA_CMP_HIDDEN), lambda b, c: (c // 2, 0, 0, 0)),
                  pl.BlockSpec((1, 1, NSA_CMP_HIDDEN), lambda b, c: (c // 2, 0, 0)),
                  pl.BlockSpec((1, NSA_CMP_HIDDEN, LANES), lambda b, c: (c // 2, 0, 0))],
        out_specs=pl.BlockSpec((1, 1, nch, LANES), lambda b, c: (b, c, 0, 0)),
        out_shape=jax.ShapeDtypeStruct((batch, 4, nch, LANES), BF16),
        scratch_shapes=[pltpu.VMEM((seq, LANES), F32)],
        compiler_params=_params(("parallel", "arbitrary")), name="nsa_cmp_mlp")(n_out, pos, w1r, b1, w2)


def _cmp_attn_kernel(q_ref, comp_ref, ovt_ref, o_ref, sel_ref, *, topk):
    tq = q_ref.shape[0]
    nch = comp_ref.shape[2]
    n_sel = ovt_ref.shape[0]
    start = pl.program_id(1) * tq
    pos_r = start + lax.broadcasted_iota(jnp.int32, (tq, nch), 0)
    cend = lax.broadcasted_iota(jnp.int32, (tq, nch), 1) * NSA_CMP_STRIDE + (NSA_CMP_LEN - 1)
    vis = cend <= pos_r
    jj = lax.broadcasted_iota(jnp.int32, (n_sel, tq), 0)
    pos_t = start + lax.broadcasted_iota(jnp.int32, (n_sel, tq), 1)
    qblk = pos_t // NSA_SLC_BLOCK
    causal = jj * NSA_SLC_BLOCK <= pos_t
    forced = (jj == 0) | (jj == qblk) | (jj == qblk - 1)
    eye = (lax.broadcasted_iota(jnp.int32, (tq, tq), 0)
           == lax.broadcasted_iota(jnp.int32, (tq, tq), 1)).astype(BF16)
    sels = []
    for g in range(NSA_KV_HEADS):
        kc = comp_ref[0, g]
        vc = comp_ref[0, 2 + g]
        psum = jnp.zeros((tq, nch), F32)
        for r in range(NSA_HEADS // NSA_KV_HEADS):
            h = g * (NSA_HEADS // NSA_KV_HEADS) + r
            q = q_ref[:, h * LANES:(h + 1) * LANES]
            s = jnp.where(vis, _dot_nt(q, kc), MASK_VALUE)
            e = jnp.exp(s - jnp.max(s, axis=-1, keepdims=True))
            p = jnp.where(vis, e / jnp.sum(e, axis=-1, keepdims=True), 0.0)
            o_ref[:, h * LANES:(h + 1) * LANES] = _dot(p.astype(BF16), vc).astype(o_ref.dtype)
            psum += p
        imp = lax.dot_general(ovt_ref[...], psum, (((1,), (1,)), ((), ())),
                              preferred_element_type=F32, precision=lax.Precision.HIGHEST)
        imp = jnp.where(causal, imp, MASK_VALUE)
        imp = jnp.where(forced, FORCE_SCORE, imp)
        rank = jnp.zeros((n_sel, tq), F32)
        for jp in range(n_sel):
            row = imp[jp:jp + 1, :]
            rank += jnp.where(jj > jp, jnp.where(row >= imp, 1.0, 0.0), jnp.where(row > imp, 1.0, 0.0))
        sel_t = jnp.where(rank < topk, 1.0, 0.0).astype(BF16)
        sels.append(_dot_nt(eye, sel_t))
    sel_ref[...] = jnp.concatenate(sels, axis=1).astype(sel_ref.dtype)


def _cmp_attn(qp, comp, ovt, *, batch, seq):
    tq = ATT_Q_TILE
    nq = seq // tq
    nch = seq // NSA_CMP_STRIDE
    n_sel = seq // NSA_SLC_BLOCK
    T = batch * seq
    kern = functools.partial(_cmp_attn_kernel, topk=min(NSA_SLC_TOPK, n_sel))
    return pl.pallas_call(
        kern, grid=(batch, nq),
        in_specs=[pl.BlockSpec((tq, NSA_Q_DIM), lambda b, t: (b * nq + t, 0)),
                  pl.BlockSpec((1, 4, nch, LANES), lambda b, t: (b, 0, 0, 0)),
                  pl.BlockSpec((n_sel, nch), lambda b, t: (0, 0))],
        out_specs=(pl.BlockSpec((tq, NSA_Q_DIM), lambda b, t: (b * nq + t, 0)),
                   pl.BlockSpec((tq, 2 * n_sel), lambda b, t: (b * nq + t, 0))),
        out_shape=(jax.ShapeDtypeStruct((T, NSA_Q_DIM), BF16),
                   jax.ShapeDtypeStruct((T, 2 * n_sel), BF16)),
        compiler_params=_params(("parallel", "arbitrary")), name="nsa_cmp_attn")(qp, comp, ovt)


def _flash_kernel(*refs, mode, n_sel):
    if mode == 'slc':
        q_ref, k_ref, v_ref, sel_ref, o_ref, m_scr, l_scr, acc_scr = refs
        tk = SLC_K_TILE
    else:
        q_ref, k_ref, v_ref, o_ref, m_scr, l_scr, acc_scr = refs
        tk = ATT_Q_TILE
    tq = q_ref.shape[0]
    rep = NSA_HEADS // NSA_KV_HEADS
    qt = pl.program_id(1)
    qpos = qt * tq + lax.broadcasted_iota(jnp.int32, (tq, tk), 0)
    kofs = lax.broadcasted_iota(jnp.int32, (tq, tk), 1)
    if mode == 'slc':
        lo = 0
        hi = (qt * tq + tq + tk - 1) // tk
    else:
        lo = jnp.maximum(qt - NSA_WINDOW // tk, 0)
        hi = qt + 1
    for g in range(NSA_KV_HEADS):
        q4 = jnp.concatenate([q_ref[:, (g * rep + r) * LANES:(g * rep + r + 1) * LANES]
                              for r in range(rep)], axis=0)
        m_scr[...] = jnp.full(m_scr.shape, MASK_VALUE, F32)
        l_scr[...] = jnp.zeros(l_scr.shape, F32)
        acc_scr[...] = jnp.zeros(acc_scr.shape, F32)

        def body(kt, carry):
            base = pl.multiple_of(kt * tk, tk)
            k = k_ref[pl.ds(base, tk), g * LANES:(g + 1) * LANES]
            v = v_ref[pl.ds(base, tk), g * LANES:(g + 1) * LANES]
            s = _dot_nt(q4, k)
            kpos = base + kofs
            if mode == 'slc':
                blk = (g * n_sel + kt * (tk // NSA_SLC_BLOCK)
                       + lax.broadcasted_iota(jnp.int32, (2 * n_sel, tk), 1) // NSA_SLC_BLOCK)
                expand = (lax.broadcasted_iota(jnp.int32, (2 * n_sel, tk), 0) == blk).astype(BF16)
                picked = _dot(sel_ref[...], expand)
                bias = jnp.where(picked > 0.5, jnp.where(kpos <= qpos, 0.0, MASK_VALUE), MASK_VALUE)
            else:
                diff = qpos - kpos
                bias = jnp.where(diff >= 0, jnp.where(diff < NSA_WINDOW, 0.0, MASK_VALUE), MASK_VALUE)
            s = s + jnp.concatenate([bias] * rep, axis=0)
            m_prev = m_scr[...]
            m_next = jnp.maximum(m_prev, jnp.max(s, axis=-1, keepdims=True))
            alpha = jnp.exp(m_prev - m_next)
            p = jnp.exp(s - m_next[:, :1])
            l_scr[...] = alpha * l_scr[...] + jnp.sum(p, axis=-1, keepdims=True)
            acc_scr[...] = acc_scr[...] * alpha + _dot(p.astype(BF16), v)
            m_scr[...] = m_next
            return carry

        lax.fori_loop(lo, hi, body, 0)
        o = acc_scr[...] / l_scr[...]
        for r in range(rep):
            h = g * rep + r
            o_ref[:, h * LANES:(h + 1) * LANES] = o[r * tq:(r + 1) * tq].astype(o_ref.dtype)


def _flash(qr, n_out, sel, *, mode, batch, seq):
    tq = ATT_Q_TILE
    nq = seq // tq
    n_sel = seq // NSA_SLC_BLOCK
    T = batch * seq
    kblk = (NSA_Q_DIM // 256) + (0 if mode == 'slc' else 1)
    vblk = (N_SLCV if mode == 'slc' else N_WINV) // 256
    in_specs = [pl.BlockSpec((tq, NSA_Q_DIM), lambda b, t: (b * nq + t, 0)),
                pl.BlockSpec((seq, 256), lambda b, t: (b, kblk)),
                pl.BlockSpec((seq, 256), lambda b, t: (b, vblk))]
    args = [qr, qr, n_out]
    if mode == 'slc':
        in_specs.append(pl.BlockSpec((tq, 2 * n_sel), lambda b, t: (b * nq + t, 0)))
        args.append(sel)
    rows = tq * (NSA_HEADS // NSA_KV_HEADS)
    return pl.pallas_call(
        functools.partial(_flash_kernel, mode=mode, n_sel=n_sel), grid=(batch, nq),
        in_specs=in_specs,
        out_specs=pl.BlockSpec((tq, NSA_Q_DIM), lambda b, t: (b * nq + t, 0)),
        out_shape=jax.ShapeDtypeStruct((T, NSA_Q_DIM), BF16),
        scratch_shapes=[pltpu.VMEM((rows, LANES), F32)] * 3,
        compiler_params=_params(("parallel", "arbitrary")), name="nsa_" + mode)(*args)


def _ssd_kernel(xs_ref, bc_ref, z_ref, small_ref, cw_ref, cb_ref, dtb_ref, alog_ref, dexp_ref,
                nw_ref, eh_ref, o_ref, xs_scr, bc_scr, h_scr):
    L = SSM_CHUNK
    P2 = SSM_D_INNER // SSM_GROUPS
    N = SSM_D_STATE
    c = pl.program_id(1)

    @pl.when(c == 0)
    def _():
        xs_scr[0:8, :] = jnp.zeros((8, SSM_D_INNER), F32)
        bc_scr[0:8, :] = jnp.zeros((8, SSM_BC_DIM), F32)
        h_scr[...] = jnp.zeros(h_scr.shape, F32)

    xs_scr[8:8 + L, :] = xs_ref[...].astype(F32)
    bc_scr[8:8 + L, :] = bc_ref[...].astype(F32)

    def conv(scr, col0, width):
        acc = jnp.zeros((L, width), F32) + cb_ref[:, col0:col0 + width]
        for k in range(SSM_CONV):
            acc += scr[8 - (SSM_CONV - 1) + k:8 - (SSM_CONV - 1) + k + L, :] * cw_ref[k:k + 1, col0:col0 + width]
        return _silu(acc)

    xs = conv(xs_scr, 0, SSM_D_INNER)
    bcm = conv(bc_scr, SSM_D_INNER, SSM_BC_DIM)
    xs_scr[0:8, :] = xs_scr[L:L + 8, :]
    bc_scr[0:8, :] = bc_scr[L:L + 8, :]

    lane = lax.broadcasted_iota(jnp.int32, (L, LANES), 1)
    pre = small_ref[...] + dtb_ref[...]
    dt = jnp.maximum(pre, 0.0) + jnp.log(1.0 + jnp.exp(-jnp.abs(pre)))
    dt = jnp.where(lane < SSM_HEADS, dt, 0.0)
    a = dt * (-jnp.exp(alog_ref[...]))
    tri = (lax.broadcasted_iota(jnp.int32, (L, L), 0)
           >= lax.broadcasted_iota(jnp.int32, (L, L), 1))
    tri_b = tri.astype(BF16)
    a_cs = sum(_dot(tri_b, part) for part in _split_bf16(a, 3))
    a_cs_t = a_cs.T
    a_end = a_cs[L - 1:L, :]
    eh = eh_ref[...]
    dt_x = _expand(dt, eh)
    ea_x = _expand(jnp.exp(a_cs), eh)
    de_x = _expand(jnp.exp(a_end - a_cs), eh)
    cd_x = _expand(jnp.broadcast_to(jnp.exp(a_end), (8, LANES)), eh)[0:1]

    X = xs * dt_x
    Xb = X.astype(BF16)
    Xe = (X * de_x).astype(BF16)
    lane_lo = lax.broadcasted_iota(jnp.int32, (L, LANES), 1) < SSM_HEAD_DIM
    y_parts = []
    for g in range(SSM_GROUPS):
        Bg = bcm[:, g * N:(g + 1) * N]
        Cg = bcm[:, (SSM_GROUPS + g) * N:(SSM_GROUPS + g + 1) * N]
        Cb = Cg.astype(BF16)
        cbm = _dot_nt(Cb, Bg.astype(BF16))
        hT = h_scr[g]
        y_off = _dot(Cb, hT.astype(BF16)) * ea_x[:, g * P2:(g + 1) * P2]
        y_dg = []
        for pp in range(P2 // LANES):
            h0 = g * (SSM_HEADS // SSM_GROUPS) + 2 * pp
            acc = None
            for e in range(2):
                h = h0 + e
                seg = a_cs[:, h:h + 1] - a_cs_t[h:h + 1, :]
                dec = jnp.exp(jnp.where(tri, seg, MASK_VALUE))
                m = (cbm * dec).astype(BF16)
                col = g * P2 + pp * LANES
                xh = jnp.where(lane_lo if e == 0 else jnp.logical_not(lane_lo), Xb[:, col:col + LANES],
                               jnp.zeros((), BF16))
                t = _dot(m, xh)
                acc = t if acc is None else acc + t
            y_dg.append(acc)
        y_parts.append(jnp.concatenate(y_dg, axis=1) + y_off)
        st = _dot(Bg.T.astype(BF16), Xe[:, g * P2:(g + 1) * P2])
        h_scr[g] = hT * cd_x[:, g * P2:(g + 1) * P2] + st
    y = jnp.concatenate(y_parts, axis=1) + xs * dexp_ref[...]
    y = y * _silu(z_ref[...].astype(F32))
    outs = []
    for g in range(SSM_GROUPS):
        yg = y[:, g * P2:(g + 1) * P2]
        outs.append(yg * lax.rsqrt(jnp.mean(yg * yg, axis=-1, keepdims=True) + NORM_EPS))
    o_ref[...] = (jnp.concatenate(outs, axis=1) * nw_ref[...]).astype(o_ref.dtype)


def _ssd(n_out, small, cw, cb, dtb, alog, dexp, nw, eh, *, batch, seq):
    L = SSM_CHUNK
    nc = seq // L
    T = batch * seq
    row = lambda b, c: b * nc + c
    full = lambda shape: pl.BlockSpec(shape, lambda b, c: (0,) * len(shape))
    return pl.pallas_call(
        _ssd_kernel, grid=(batch, nc),
        in_specs=[pl.BlockSpec((L, SSM_D_INNER), lambda b, c: (row(b, c), N_XS // SSM_D_INNER)),
                  pl.BlockSpec((L, SSM_BC_DIM), lambda b, c: (row(b, c), N_BC // SSM_BC_DIM)),
                  pl.BlockSpec((L, SSM_D_INNER), lambda b, c: (row(b, c), N_Z // SSM_D_INNER)),
                  pl.BlockSpec((L, LANES), lambda b, c: (row(b, c), 0)),
                  full(cw.shape), full(cb.shape), full(dtb.shape), full(alog.shape),
                  full(dexp.shape), full(nw.shape), full(eh.shape)],
        out_specs=pl.BlockSpec((L, SSM_D_INNER), lambda b, c: (row(b, c), 0)),
        out_shape=jax.ShapeDtypeStruct((T, SSM_D_INNER), BF16),
        scratch_shapes=[pltpu.VMEM((L + 8, SSM_D_INNER), F32), pltpu.VMEM((L + 8, SSM_BC_DIM), F32),
                        pltpu.VMEM((SSM_GROUPS, SSM_D_STATE, SSM_D_INNER // SSM_GROUPS), F32)],
        compiler_params=_params(("parallel", "arbitrary")), name="ssd")(
            n_out, n_out, n_out, small, cw, cb, dtb, alog, dexp, nw, eh)


def _swa_kernel(sink_ref, q_ref, kp_ref, kc_ref, vp_ref, vc_ref, o_ref):
    tq = q_ref.shape[0]
    qt = pl.program_id(1)
    rep = SWA_HEADS // SWA_KV_HEADS
    r_i = lax.broadcasted_iota(jnp.int32, (tq, 2 * tq), 0)
    c_i = lax.broadcasted_iota(jnp.int32, (tq, 2 * tq), 1)
    diff = tq + r_i - c_i
    first_key = jnp.where(qt > 0, 0, tq)
    ok = (diff >= 0) & (diff < SWA_WINDOW) & (c_i >= first_key)
    lane_lo = lax.broadcasted_iota(jnp.int32, (2 * tq, LANES), 1) < SWA_HEAD_DIM
    kf = jnp.concatenate([kp_ref[...], kc_ref[...]], axis=0).astype(F32)
    vf = jnp.concatenate([vp_ref[...], vc_ref[...]], axis=0).astype(F32)
    ks, vs = pltpu.roll(kf, SWA_HEAD_DIM, 1), pltpu.roll(vf, SWA_HEAD_DIM, 1)

    def halves(own, swapped, g):
        a, b = (own, swapped) if g == 0 else (swapped, own)
        return (jnp.where(lane_lo, a, 0.0).astype(BF16), jnp.where(lane_lo, 0.0, b).astype(BF16))

    for g in range(SWA_KV_HEADS):
        k_lo, k_hi = halves(kf, ks, g)
        v_lo, v_hi = halves(vf, vs, g)
        for pp in range(rep // 2):
            col = (g * rep // 2 + pp) * LANES
            q2 = q_ref[:, col:col + LANES]
            acc = None
            for e, (kk, vv) in enumerate(((k_lo, v_lo), (k_hi, v_hi))):
                sk = sink_ref[g * rep + 2 * pp + e]
                s = jnp.where(ok, _dot_nt(q2, kk), MASK_VALUE)
                m = jnp.maximum(jnp.max(s, axis=-1, keepdims=True), sk)
                ex = jnp.exp(s - m)
                p = ex / (jnp.sum(ex, axis=-1, keepdims=True) + jnp.exp(sk - m))
                t = _dot(p.astype(BF16), vv)
                acc = t if acc is None else acc + t
            o_ref[:, col:col + LANES] = acc.astype(o_ref.dtype)


def _swa(sinks, c_out, n_out, *, batch, seq):
    tq = SWA_WINDOW
    nq = seq // tq
    T = batch * seq
    kcol = SWA_Q_DIM // LANES
    vcol = N_SWAV // LANES
    cur = lambda b, t: b * nq + t
    prev = lambda b, t: b * nq + jnp.maximum(t - 1, 0)
    return pl.pallas_call(
        _swa_kernel, grid=(batch, nq),
        in_specs=[pl.BlockSpec(memory_space=pltpu.SMEM),
                  pl.BlockSpec((tq, SWA_Q_DIM), lambda b, t: (cur(b, t), 0)),
                  pl.BlockSpec((tq, LANES), lambda b, t: (prev(b, t), kcol)),
                  pl.BlockSpec((tq, LANES), lambda b, t: (cur(b, t), kcol)),
                  pl.BlockSpec((tq, LANES), lambda b, t: (prev(b, t), vcol)),
                  pl.BlockSpec((tq, LANES), lambda b, t: (cur(b, t), vcol))],
        out_specs=pl.BlockSpec((tq, SWA_Q_DIM), lambda b, t: (cur(b, t), 0)),
        out_shape=jax.ShapeDtypeStruct((T, SWA_Q_DIM), BF16),
        compiler_params=_params(("parallel", "arbitrary")), name="swa")(
            sinks, c_out, c_out, c_out, n_out, n_out)


def _merge_kernel(ocmp_ref, oslc_ref, owin_ref, small_ref, ob_ref, oc_ref, mg_ref,
                  pa_ref, pb_ref, pc_ref, eg_ref, y_ref):
    gx = _expand(_sigmoid(small_ref[...]), eg_ref[...])
    Q = NSA_Q_DIM
    oa = (gx[:, 0:Q] * ocmp_ref[...].astype(F32) + gx[:, Q:2 * Q] * oslc_ref[...].astype(F32)
          + gx[:, 2 * Q:3 * Q] * owin_ref[...].astype(F32)).astype(BF16)
    D = D_MODEL
    y = _sigmoid(mg_ref[:, 0:D].astype(F32)) * _dot(oa, pa_ref[...])
    y += _sigmoid(mg_ref[:, D:2 * D].astype(F32)) * _dot(ob_ref[...], pb_ref[...])
    y += _sigmoid(mg_ref[:, 2 * D:3 * D].astype(F32)) * _dot(oc_ref[...], pc_ref[...])
    y_ref[...] = y.astype(y_ref.dtype)


def _merge(ocmp, oslc, owin, small, ob, oc, n_out, pa, pb, pc, eg, *, tm):
    T = ocmp.shape[0]
    rowblk = lambda w: pl.BlockSpec((tm, w), lambda i: (i, 0))
    const = lambda a: pl.BlockSpec(a.shape, lambda i: (0,) * a.ndim, pipeline_mode=pl.Buffered(1))
    return pl.pallas_call(
        _merge_kernel, grid=(T // tm,),
        in_specs=[rowblk(NSA_Q_DIM), rowblk(NSA_Q_DIM), rowblk(NSA_Q_DIM), rowblk(LANES),
                  rowblk(SSM_D_INNER), rowblk(SWA_Q_DIM), rowblk(3 * D_MODEL),
                  const(pa), const(pb), const(pc), const(eg)],
        out_specs=rowblk(D_MODEL),
        out_shape=jax.ShapeDtypeStruct((T, D_MODEL), BF16),
        compiler_params=_params(("parallel",)), name="merge")(
            ocmp, oslc, owin, small, ob, oc, n_out, pa, pb, pc, eg)


def _outproj_kernel(x_ref, y_ref, wo_ref, nw_ref, wr_ref, rb_ref, xo_ref, hn_ref, route_ref):
    x = x_ref[...] + _dot(y_ref[...], wo_ref[...])
    xo_ref[...] = x
    ms = jnp.mean(x * x, axis=-1, keepdims=True)
    hn = x * lax.rsqrt(ms + NORM_EPS) * nw_ref[...]
    hn_ref[...] = hn.astype(hn_ref.dtype)
    logit = jnp.dot(hn, wr_ref[...], preferred_element_type=F32,
                    precision=lax.Precision.HIGHEST) + rb_ref[...]
    tm = logit.shape[0]
    lane = lax.broadcasted_iota(jnp.int32, (tm, LANES), 1)
    big = jnp.int32(LANES)
    gl = jnp.where(lane < MOE_GROUPS, logit, -jnp.inf)
    gmax = jnp.max(gl, axis=-1, keepdims=True)
    gidx = jnp.min(jnp.where(gl == gmax, lane, big), axis=-1, keepdims=True)
    gw = 1.0 / jnp.sum(jnp.exp(gl - gmax), axis=-1, keepdims=True)
    lo = MOE_GROUPS + MOE_EXPERTS_PER_GROUP * gidx
    el = jnp.where((lane >= lo) & (lane < lo + MOE_EXPERTS_PER_GROUP), logit, -jnp.inf)
    m1 = jnp.max(el, axis=-1, keepdims=True)
    i1 = jnp.min(jnp.where(el == m1, lane, big), axis=-1, keepdims=True)
    el2 = jnp.where(lane == i1, -jnp.inf, el)
    m2 = jnp.max(el2, axis=-1, keepdims=True)
    i2 = jnp.min(jnp.where(el2 == m2, lane, big), axis=-1, keepdims=True)
    e2 = jnp.exp(m2 - m1)
    w1 = gw / (1.0 + e2)
    w2 = gw * e2 / (1.0 + e2)
    route = jnp.where(lane == 0, (i1 - MOE_GROUPS).astype(F32),
                      jnp.where(lane == 1, (i2 - MOE_GROUPS).astype(F32),
                                jnp.where(lane == 2, w1, jnp.where(lane == 3, w2, 0.0))))
    route_ref[...] = route


def _outproj(x2d, y, wo, nw, wr, rb, *, tm):
    T, D = x2d.shape
    rowblk = lambda w: pl.BlockSpec((tm, w), lambda i: (i, 0))
    const = lambda a: pl.BlockSpec(a.shape, lambda i: (0,) * a.ndim, pipeline_mode=pl.Buffered(1))
    return pl.pallas_call(
        _outproj_kernel, grid=(T // tm,),
        in_specs=[rowblk(D), rowblk(D), const(wo), const(nw), const(wr), const(rb)],
        out_specs=(rowblk(D), rowblk(D), rowblk(LANES)),
        out_shape=(jax.ShapeDtypeStruct((T, D), F32), jax.ShapeDtypeStruct((T, D), BF16),
                   jax.ShapeDtypeStruct((T, LANES), F32)),
        compiler_params=_params(("parallel",)), name="outproj")(x2d, y, wo, nw, wr, rb)


def _expert_kernel(te_ref, x_ref, wgu_ref, wd_ref, y_ref):
    gu = _dot(x_ref[...], wgu_ref[0])
    act = (_silu(gu[:, :MOE_D_FF]) * gu[:, MOE_D_FF:]).astype(BF16)
    y_ref[...] = _dot(act, wd_ref[0]).astype(y_ref.dtype)


def _experts(tile_expert, xs, wgu, wd):
    P, D = xs.shape
    tm = MOE_ROW_TILE
    grid_spec = pltpu.PrefetchScalarGridSpec(
        num_scalar_prefetch=1, grid=(P // tm,),
        in_specs=[pl.BlockSpec((tm, D), lambda i, te: (i, 0)),
                  pl.BlockSpec((1, D, 2 * MOE_D_FF), lambda i, te: (te[i], 0, 0)),
                  pl.BlockSpec((1, MOE_D_FF, D), lambda i, te: (te[i], 0, 0))],
        out_specs=pl.BlockSpec((tm, D), lambda i, te: (i, 0)))
    return pl.pallas_call(
        _expert_kernel, grid_spec=grid_spec,
        out_shape=jax.ShapeDtypeStruct((P, D), BF16),
        compiler_params=_params(("arbitrary",)), name="experts")(tile_expert, xs, wgu, wd)


def _combine_kernel(x_ref, y0_ref, y1_ref, route_ref, nw_ref, o_ref, *, final):
    r = route_ref[...]
    x = x_ref[...] + r[:, 2:3] * y0_ref[...].astype(F32) + r[:, 3:4] * y1_ref[...].astype(F32)
    if final:
        ms = jnp.mean(x * x, axis=-1, keepdims=True)
        x = x * lax.rsqrt(ms + NORM_EPS) * nw_ref[...]
    o_ref[...] = x


def _combine(x2d, y0, y1, route, nw, *, final, tm):
    T, D = x2d.shape
    rowblk = lambda w: pl.BlockSpec((tm, w), lambda i: (i, 0))
    return pl.pallas_call(
        functools.partial(_combine_kernel, final=final), grid=(T // tm,),
        in_specs=[rowblk(D), rowblk(D), rowblk(D), rowblk(LANES),
                  pl.BlockSpec((1, D), lambda i: (0, 0))],
        out_specs=rowblk(D), out_shape=jax.ShapeDtypeStruct((T, D), F32),
        compiler_params=_params(("parallel",)), name="combine")(x2d, y0, y1, route, nw)


def _rope_tables(seq):
    def tab(dim):
        inv = 1.0 / (ROPE_THETA ** (jnp.arange(0, dim, 2, dtype=F32) / dim))
        ang = jnp.arange(seq, dtype=F32)[:, None] * inv[None, :]
        return jnp.cos(ang), jnp.sin(ang)
    ca, sa = tab(NSA_HEAD_DIM)
    cc, sc = tab(SWA_HEAD_DIM)
    z = jnp.zeros_like(sc)
    tabs_a = (jnp.concatenate([ca, ca], 1), jnp.concatenate([-sa, sa], 1))
    tabs_c = (jnp.concatenate([cc] * 4, 1), jnp.concatenate([-sc, z, -sc, z], 1),
              jnp.concatenate([z, sc, z, sc], 1))
    return tabs_a, tabs_c


def _overlap_t(seq):
    nch = seq // NSA_CMP_STRIDE
    n_sel = seq // NSA_SLC_BLOCK
    cs = np.arange(nch) * NSA_CMP_STRIDE
    ce = cs + NSA_CMP_LEN - 1
    ss = np.arange(n_sel) * NSA_SLC_BLOCK
    ov = (cs[None, :] <= ss[:, None] + NSA_SLC_BLOCK - 1) & (ce[None, :] >= ss[:, None])
    ov[:, nch - 1] = False
    return jnp.asarray(ov.astype(np.float32))


def _head_expand():
    e = np.zeros((LANES, SSM_D_INNER), np.float32)
    for h in range(SSM_HEADS):
        e[h, h * SSM_HEAD_DIM:(h + 1) * SSM_HEAD_DIM] = 1.0
    return jnp.asarray(e, BF16)


def _gate_expand():
    e = np.zeros((LANES, 3 * NSA_Q_DIM), np.float32)
    for br in range(3):
        for h in range(NSA_HEADS):
            c0 = br * NSA_Q_DIM + h * NSA_HEAD_DIM
            e[SMALL_GATE0 + 3 * h + br, c0:c0 + NSA_HEAD_DIM] = 1.0
    return jnp.asarray(e, BF16)


def _split_w_in(w_in):
    o = np.cumsum([0, NSA_Q_DIM, 1536, 24, SSM_D_INNER, SSM_D_INNER + SSM_BC_DIM, SSM_HEADS,
                   SWA_Q_DIM, 256, 3 * D_MODEL])
    seg = lambda a, b: w_in[:, a:b]
    nsa_q = seg(o[0], o[1])
    kv = o[1]
    cmp_kv, slc_k, slc_v = seg(kv, kv + 512), seg(kv + 512, kv + 768), seg(kv + 768, kv + 1024)
    win_k, win_v = seg(kv + 1024, kv + 1280), seg(kv + 1280, kv + 1536)
    nsa_g = seg(o[2], o[3])
    ssm_z = seg(o[3], o[4])
    ssm_xs, ssm_bc = seg(o[4], o[4] + SSM_D_INNER), seg(o[4] + SSM_D_INNER, o[5])
    ssm_dt = seg(o[5], o[6])
    swa_q = seg(o[6], o[7])
    swa_k, swa_v = seg(o[7], o[7] + 128), seg(o[7] + 128, o[8])
    merge_g = seg(o[8], o[9])
    w_a = jnp.concatenate([nsa_q, slc_k, win_k], 1).astype(BF16)
    w_c = jnp.concatenate([swa_q, swa_k], 1).astype(BF16)
    w_n = jnp.concatenate([merge_g, ssm_z, ssm_xs, ssm_bc, cmp_kv, slc_v, win_v, swa_v], 1).astype(BF16)
    pad = jnp.zeros((w_in.shape[0], LANES - SSM_HEADS - 24), w_in.dtype)
    w_s = jnp.concatenate([ssm_dt, nsa_g, pad], 1).astype(BF16)
    return w_a, w_c, w_n, w_s


def _pad_lanes(v):
    return jnp.pad(v, (0, LANES - v.shape[0]))[None, :]


def _dispatch(route, n_tok):
    tm = MOE_ROW_TILE
    n_asg = n_tok * MOE_TOPK
    n_rows = n_asg + MOE_EXPERTS * tm
    flat_e = route[:, 0:MOE_TOPK].astype(jnp.int32).reshape(-1)
    order = jnp.argsort(flat_e, stable=True)
    sorted_e = flat_e[order]
    counts = jnp.zeros((MOE_EXPERTS,), jnp.int32).at[flat_e].add(1)
    padded = ((counts + tm - 1) // tm) * tm
    pend = jnp.cumsum(padded)
    pstart = pend - padded
    start = jnp.cumsum(counts) - counts
    dest = pstart[sorted_e] + (jnp.arange(n_asg, dtype=jnp.int32) - start[sorted_e])
    row_token = jnp.zeros((n_rows,), jnp.int32).at[dest].set(order // MOE_TOPK)
    pos = jnp.zeros((n_asg,), jnp.int32).at[order].set(dest).reshape(n_tok, MOE_TOPK)
    tile_start = jnp.arange(n_rows // tm, dtype=jnp.int32) * tm
    tile_expert = jnp.minimum(jnp.searchsorted(pend, tile_start, side='right'),
                              MOE_EXPERTS - 1).astype(jnp.int32)
    return row_token, pos, tile_expert


def kernel(x, norm_mix, norm_ffn, w_in, nsa_cmp_pos, nsa_cmp_w1, nsa_cmp_b1, nsa_cmp_w2, ssm_conv_w,
           ssm_conv_b, ssm_dt_bias, ssm_a_log, ssm_d, ssm_norm, swa_sinks, proj_nsa, proj_ssm, proj_swa,
           w_out, moe_group_router, moe_group_bias, moe_expert_router, moe_expert_bias, moe_w_gate,
           moe_w_up, moe_w_down, final_norm):
    B, S, D = x.shape
    T = B * S
    depth = w_in.shape[0]
    tm = 512
    tabs_a, tabs_c = _rope_tables(S)
    ovt = _overlap_t(S)
    eh = _head_expand()
    eg = _gate_expand()
    scale_a = jnp.concatenate([jnp.full((NSA_Q_DIM,), NSA_HEAD_DIM ** -0.5, F32),
                               jnp.ones((512,), F32)])[None, :]
    scale_c = jnp.concatenate([jnp.full((SWA_Q_DIM,), SWA_HEAD_DIM ** -0.5, F32),
                               jnp.ones((LANES,), F32)])[None, :]
    xc = x.reshape(T, D)
    for l in range(depth):
        w_a, w_c, w_n, w_s = _split_w_in(w_in[l])
        nw = norm_mix[l][None, :]
        qr, qp = _inproj(xc, nw, w_a, seq=S, tm=tm, tn=512, rope='a', scale=scale_a, tabs=tabs_a,
                         out_dtypes=(BF16, BF16))
        (c_out,) = _inproj(xc, nw, w_c, seq=S, tm=tm, tn=w_c.shape[1], rope='c', scale=scale_c,
                           tabs=tabs_c)
        (n_out,) = _inproj(xc, nw, w_n, seq=S, tm=tm, tn=896)
        (small,) = _inproj(xc, nw, w_s, seq=S, tm=tm, tn=LANES, out_dtypes=(F32,))

        w1r = nsa_cmp_w1[l].reshape(2, NSA_CMP_LEN, NSA_HEAD_DIM, NSA_CMP_HIDDEN).astype(BF16)
        comp = _cmp_mlp(n_out, nsa_cmp_pos[l], w1r, nsa_cmp_b1[l][:, None, :],
                        nsa_cmp_w2[l].astype(BF16), batch=B, seq=S)
        o_cmp, sel = _cmp_attn(qp, comp, ovt, batch=B, seq=S)
        o_slc = _flash(qr, n_out, sel, mode='slc', batch=B, seq=S)
        o_win = _flash(qr, n_out, None, mode='win', batch=B, seq=S)

        dexp = jnp.repeat(ssm_d[l], SSM_HEAD_DIM)[None, :]
        o_b = _ssd(n_out, small, ssm_conv_w[l], ssm_conv_b[l][None, :], _pad_lanes(ssm_dt_bias[l]),
                   _pad_lanes(ssm_a_log[l]), dexp, ssm_norm[l][None, :], eh, batch=B, seq=S)
        o_c = _swa(swa_sinks[l], c_out, n_out, batch=B, seq=S)

        y = _merge(o_cmp, o_slc, o_win, small, o_b, o_c, n_out, proj_nsa[l].astype(BF16),
                   proj_ssm[l].astype(BF16), proj_swa[l].astype(BF16), eg, tm=256)
        wr = jnp.pad(jnp.concatenate([moe_group_router[l], moe_expert_router[l]], 1),
                     ((0, 0), (0, LANES - MOE_GROUPS - MOE_EXPERTS)))
        rb = _pad_lanes(jnp.concatenate([moe_group_bias[l], moe_expert_bias[l]]))
        x_mid, hn, route = _outproj(xc, y, w_out[l].astype(BF16), norm_ffn[l][None, :], wr, rb, tm=256)

        row_token, pos, tile_expert = _dispatch(route, T)
        wgu = jnp.concatenate([moe_w_gate[l], moe_w_up[l]], -1).reshape(
            MOE_EXPERTS, D, 2 * MOE_D_FF).astype(BF16)
        wd = moe_w_down[l].reshape(MOE_EXPERTS, MOE_D_FF, D).astype(BF16)
        ys = _experts(tile_expert, hn[row_token], wgu, wd)
        xc = _combine(x_mid, ys[pos[:, 0]], ys[pos[:, 1]], route, final_norm[None, :],
                      final=(l == depth - 1), tm=tm)
    return xc.reshape(B, S, D)
```

```python
import functools

import jax
import jax.numpy as jnp
import numpy as np
from jax import lax
from jax.experimental import pallas as pl
from jax.experimental.pallas import tpu as pltpu

F32 = jnp.float32
BF16 = jnp.bfloat16

D_MODEL = 2048
ROPE_THETA = 10000.0
NORM_EPS = 1e-6
MASK_VALUE = -1e30
FORCE_SCORE = 1e6

NSA_HEADS = 8
NSA_KV_HEADS = 2
NSA_REP = NSA_HEADS // NSA_KV_HEADS
NSA_HEAD_DIM = 128
NSA_CMP_STRIDE = 16
NSA_CMP_LEN = 32
NSA_CMP_HIDDEN = 256
NSA_SLC_BLOCK = 64
NSA_SLC_TOPK = 16
NSA_WINDOW = 512
NSA_Q_DIM = NSA_HEADS * NSA_HEAD_DIM

SSM_D_INNER = 1024
SSM_HEAD_DIM = 64
SSM_HEADS = 16
SSM_GROUPS = 2
SSM_D_STATE = 128
SSM_CONV = 4
SSM_CHUNK = 128
SSM_BC_DIM = 2 * SSM_GROUPS * SSM_D_STATE

SWA_HEADS = 16
SWA_KV_HEADS = 2
SWA_HEAD_DIM = 64
SWA_WINDOW = 128
SWA_Q_DIM = SWA_HEADS * SWA_HEAD_DIM

MOE_GROUPS = 4
MOE_EXPERTS_PER_GROUP = 8
MOE_EXPERTS = MOE_GROUPS * MOE_EXPERTS_PER_GROUP
MOE_TOPK = 2
MOE_D_FF = 512

LANES = 128
ATT_Q_TILE = 128
SLC_K_TILE = 256
MOE_ROW_TILE = 256
VMEM_LIMIT = 56 * 1024 * 1024

N_MERGE = 0
N_Z = 6144
N_XS = 7168
N_BC = 8192
N_CMP = 8704
N_SLCV = 9216
N_WINV = 9472
N_SWAV = 9728
N_TOTAL = 9856
SMALL_GATE0 = 16


def _params(sem):
    return pltpu.CompilerParams(dimension_semantics=sem, vmem_limit_bytes=VMEM_LIMIT)


def _dot(a, b):
    return jnp.dot(a, b, preferred_element_type=F32)


def _dot_nt(a, b):
    return lax.dot_general(a, b, (((1,), (1,)), ((), ())), preferred_element_type=F32)


def _split_bf16(v, n):
    parts = []
    for _ in range(n):
        p = v.astype(BF16)
        parts.append(p)
        v = v - p.astype(F32)
    return parts


def _expand(v, e):
    hi, lo = _split_bf16(v, 2)
    return _dot(hi, e) + _dot(lo, e)


def _sigmoid(v):
    return 1.0 / (1.0 + jnp.exp(-v))


def _silu(v):
    return v * _sigmoid(v)


def _inproj_kernel(*refs, rope, has_scale, n_out):
    it = iter(refs)
    x_ref, nw_ref, w_ref = next(it), next(it), next(it)
    cs_ref = next(it) if has_scale else None
    tabs = [next(it) for _ in range({None: 0, 'a': 2, 'c': 3}[rope])]
    outs = [next(it) for _ in range(n_out)]
    hn_scr = next(it)

    @pl.when(pl.program_id(1) == 0)
    def _():
        x = x_ref[...]
        ms = jnp.mean(x * x, axis=-1, keepdims=True)
        hn_scr[...] = (x * lax.rsqrt(ms + NORM_EPS) * nw_ref[...]).astype(BF16)

    acc = _dot(hn_scr[...], w_ref[...])
    if has_scale:
        acc = acc * cs_ref[...]
    if rope is None:
        outs[0][...] = acc.astype(outs[0].dtype)
        return
    if n_out == 2:
        outs[1][...] = acc.astype(outs[1].dtype)
    for c in range(acc.shape[1] // LANES):
        a = acc[:, c * LANES:(c + 1) * LANES]
        if rope == 'a':
            r = a * tabs[0][...] + pltpu.roll(a, 64, 1) * tabs[1][...]
        else:
            r = (a * tabs[0][...] + pltpu.roll(a, 96, 1) * tabs[1][...]
                 + pltpu.roll(a, 32, 1) * tabs[2][...])
        outs[0][:, c * LANES:(c + 1) * LANES] = r.astype(outs[0].dtype)


def _inproj(x2d, nw, w, *, seq, tm, tn, rope=None, scale=None, tabs=(), out_dtypes=(BF16,)):
    T, D = x2d.shape
    N = w.shape[1]
    nrow = seq // tm
    in_specs = [pl.BlockSpec((tm, D), lambda i, j: (i, 0)),
                pl.BlockSpec((1, D), lambda i, j: (0, 0)),
                pl.BlockSpec((D, tn), lambda i, j: (0, j))]
    args = [x2d, nw, w]
    if scale is not None:
        in_specs.append(pl.BlockSpec((1, tn), lambda i, j: (0, j)))
        args.append(scale)
    for t in tabs:
        in_specs.append(pl.BlockSpec((tm, LANES), lambda i, j: (i % nrow, 0)))
        args.append(t)
    out_shape = tuple(jax.ShapeDtypeStruct((T, N), dt) for dt in out_dtypes)
    out_specs = tuple(pl.BlockSpec((tm, tn), lambda i, j: (i, j)) for _ in out_dtypes)
    kern = functools.partial(_inproj_kernel, rope=rope, has_scale=scale is not None,
                             n_out=len(out_dtypes))
    return pl.pallas_call(
        kern, grid=(T // tm, N // tn), in_specs=in_specs, out_specs=out_specs, out_shape=out_shape,
        scratch_shapes=[pltpu.VMEM((tm, D), BF16)],
        compiler_params=_params(("parallel", "arbitrary")), name="inproj_" + str(rope))(*args)


def _cmp_mlp_kernel(k_ref, v_ref, pos_ref, w1_ref, b1_ref, w2k_ref, w2vt_ref, kc_ref, vct_ref, f32_scr):
    nch = kc_ref.shape[2]
    for c, src in enumerate((k_ref, v_ref)):
        f32_scr[...] = src[...].astype(F32)
        first = jnp.zeros((nch, NSA_CMP_HIDDEN), F32)
        second = jnp.zeros((nch, NSA_CMP_HIDDEN), F32)
        for t in range(NSA_CMP_STRIDE):
            xt = f32_scr[pl.ds(t, nch, stride=NSA_CMP_STRIDE), :]
            first += _dot((xt + pos_ref[c, t:t + 1, :]).astype(BF16), w1_ref[c, t])
            t2 = NSA_CMP_STRIDE + t
            second += _dot((xt + pos_ref[c, t2:t2 + 1, :]).astype(BF16), w1_ref[c, t2])
        hid = _silu(first + pltpu.roll(second, nch - 1, 0) + b1_ref[c]).astype(BF16)
        if c == 0:
            kc_ref[0, 0] = _dot(hid, w2k_ref[...]).astype(kc_ref.dtype)
        else:
            vct_ref[0, 0] = _dot_nt(w2vt_ref[...], hid).astype(vct_ref.dtype)


def _cmp_mlp(n_out, pos, w1r, b1, w2k, w2vt, *, batch, seq):
    nch = seq // NSA_CMP_STRIDE
    cb0 = N_CMP // LANES
    full = lambda a: pl.BlockSpec(a.shape, lambda b, g: (0,) * a.ndim)
    return pl.pallas_call(
        _cmp_mlp_kernel, grid=(batch, NSA_KV_HEADS),
        in_specs=[pl.BlockSpec((seq, LANES), lambda b, g: (b, cb0 + g)),
                  pl.BlockSpec((seq, LANES), lambda b, g: (b, cb0 + NSA_KV_HEADS + g)),
                  full(pos), full(w1r), full(b1), full(w2k), full(w2vt)],
        out_specs=(pl.BlockSpec((1, 1, nch, LANES), lambda b, g: (b, g, 0, 0)),
                   pl.BlockSpec((1, 1, LANES, nch), lambda b, g: (b, g, 0, 0))),
        out_shape=(jax.ShapeDtypeStruct((batch, NSA_KV_HEADS, nch, LANES), BF16),
                   jax.ShapeDtypeStruct((batch, NSA_KV_HEADS, LANES, nch), BF16)),
        scratch_shapes=[pltpu.VMEM((seq, LANES), F32)],
        compiler_params=_params(("parallel", "arbitrary")), name="nsa_cmp_mlp")(
            n_out, n_out, pos, w1r, b1, w2k, w2vt)


def _nsa_attn_kernel(qp_ref, qr_ref, small_ref, kc_ref, vct_ref, ovt_ref, slck_ref, slcvt_ref, wink_ref,
                     winvt_ref, o_ref, q4_scr, sel_scr, m_scr, l_scr, acc_scr, part_scr, *, topk):
    tq = qp_ref.shape[0]
    nch = kc_ref.shape[2]
    n_sel = ovt_ref.shape[0]
    rep = NSA_REP
    tk = SLC_K_TILE
    wlen = NSA_WINDOW + tq
    qt = pl.program_id(1)
    start = qt * tq

    def tile4(v):
        return jnp.concatenate([v] * rep, axis=1)

    def qpos(rows):
        return start + lax.broadcasted_iota(jnp.int32, (rows, tq), 1)

    def sub(rows):
        return lax.broadcasted_iota(jnp.int32, (rows, tq), 0)

    for g in range(NSA_KV_HEADS):
        for r in range(rep):
            h = g * rep + r
            q4_scr[0, g, r * tq:(r + 1) * tq, :] = qp_ref[:, h * LANES:(h + 1) * LANES]
            q4_scr[1, g, r * tq:(r + 1) * tq, :] = qr_ref[:, h * LANES:(h + 1) * LANES]

    gates = _sigmoid(small_ref[...]).T

    def gate4(g, br):
        return jnp.concatenate([gates[SMALL_GATE0 + 3 * (g * rep + r) + br:SMALL_GATE0 + 3 * (g * rep + r) + br + 1, :]
                                for r in range(rep)], axis=1)

    vis = sub(nch) * NSA_CMP_STRIDE + (NSA_CMP_LEN - 1) <= qpos(nch)
    vis_bias = tile4(jnp.where(vis, 0.0, MASK_VALUE))
    vis_one = tile4(jnp.where(vis, 1.0, 0.0))
    jj = sub(n_sel)
    pos_t = qpos(n_sel)
    qblk = pos_t // NSA_SLC_BLOCK
    causal_blk = jj * NSA_SLC_BLOCK <= pos_t
    forced = (jj == 0) | (jj == qblk) | (jj == qblk - 1)
    for g in range(NSA_KV_HEADS):
        s = _dot_nt(kc_ref[0, g], q4_scr[0, g]) + vis_bias
        e = jnp.exp(s - jnp.max(s, axis=0, keepdims=True))
        p = e * (1.0 / jnp.sum(e, axis=0, keepdims=True)) * vis_one
        part_scr[g] = gate4(g, 0) * _dot(vct_ref[0, g], p.astype(BF16))
        psum = p[:, 0:tq]
        for r in range(1, rep):
            psum = psum + p[:, r * tq:(r + 1) * tq]
        imp = sum(_dot(ovt_ref[...], part) for part in _split_bf16(psum, 3))
        imp = jnp.where(causal_blk, imp, MASK_VALUE)
        imp = jnp.where(forced, FORCE_SCORE, imp)
        rank = jnp.zeros((n_sel, tq), F32)
        for jp in range(n_sel):
            row = imp[jp:jp + 1, :]
            rank += jnp.where(jj > jp, jnp.where(row >= imp, 1.0, 0.0), jnp.where(row > imp, 1.0, 0.0))
        sel_scr[g * n_sel:(g + 1) * n_sel, :] = jnp.where(rank < topk, 1.0, 0.0)

    wstart = pl.multiple_of(jnp.maximum(qt - NSA_WINDOW // tq, 0) * tq, tq)
    diff = qpos(wlen) - (wstart + sub(wlen))
    wbias = tile4(jnp.where(diff >= 0, jnp.where(diff < NSA_WINDOW, 0.0, MASK_VALUE), MASK_VALUE))
    for g in range(NSA_KV_HEADS):
        s = _dot_nt(wink_ref[pl.ds(wstart, wlen), g * LANES:(g + 1) * LANES], q4_scr[1, g]) + wbias
        e = jnp.exp(s - jnp.max(s, axis=0, keepdims=True))
        o = _dot(winvt_ref[0, g, :, pl.ds(wstart, wlen)], e.astype(BF16))
        part_scr[g] = part_scr[g] + gate4(g, 2) * (o * (1.0 / jnp.sum(e, axis=0, keepdims=True)))

    m_scr[...] = jnp.full(m_scr.shape, MASK_VALUE, F32)
    l_scr[...] = jnp.zeros(l_scr.shape, F32)
    acc_scr[...] = jnp.zeros(acc_scr.shape, F32)
    blocks_per_tile = tk // NSA_SLC_BLOCK

    def body(kt, carry):
        base = pl.multiple_of(kt * tk, tk)
        causal_bias = jnp.where(base + sub(tk) <= qpos(tk), 0.0, MASK_VALUE)
        for g in range(NSA_KV_HEADS):
            s = _dot_nt(slck_ref[pl.ds(base, tk), g * LANES:(g + 1) * LANES], q4_scr[1, g])
            picked = jnp.concatenate(
                [jnp.broadcast_to(sel_scr[pl.ds(g * n_sel + kt * blocks_per_tile + i, 1), :],
                                  (NSA_SLC_BLOCK, tq)) for i in range(blocks_per_tile)], axis=0)
            s = s + tile4(jnp.where(picked > 0.5, causal_bias, MASK_VALUE))
            m_prev = m_scr[g]
            m_next = jnp.maximum(m_prev, jnp.max(s, axis=0, keepdims=True))
            alpha = jnp.exp(m_prev - m_next)
            p = jnp.exp(s - m_next)
            l_scr[g] = alpha * l_scr[g] + jnp.sum(p, axis=0, keepdims=True)
            acc_scr[g] = acc_scr[g] * alpha + _dot(slcvt_ref[0, g, :, pl.ds(base, tk)], p.astype(BF16))
            m_scr[g] = m_next
        return carry

    lax.fori_loop(0, (start + tq + tk - 1) // tk, body, 0)

    for g in range(NSA_KV_HEADS):
        o = part_scr[g] + gate4(g, 1) * (acc_scr[g] * (1.0 / l_scr[g]))
        for r in range(rep):
            h = g * rep + r
            o_ref[:, h * LANES:(h + 1) * LANES] = o[:, r * tq:(r + 1) * tq].T.astype(o_ref.dtype)


def _nsa_attn(qp, qr, small, kc, vct, ovt, slcvt, winvt, *, batch, seq):
    tq = ATT_Q_TILE
    assert seq >= NSA_WINDOW + tq and seq % SLC_K_TILE == 0
    nq = seq // tq
    nch = seq // NSA_CMP_STRIDE
    n_sel = seq // NSA_SLC_BLOCK
    T = batch * seq
    row = lambda b, t: (b * nq + t, 0)
    per_b4 = lambda b, t: (b, 0, 0, 0)
    kcol = NSA_Q_DIM // 256
    rows = tq * NSA_REP
    kern = functools.partial(_nsa_attn_kernel, topk=min(NSA_SLC_TOPK, n_sel))
    return pl.pallas_call(
        kern, grid=(batch, nq),
        in_specs=[pl.BlockSpec((tq, NSA_Q_DIM), row), pl.BlockSpec((tq, NSA_Q_DIM), row),
                  pl.BlockSpec((tq, LANES), row),
                  pl.BlockSpec((1, NSA_KV_HEADS, nch, LANES), per_b4),
                  pl.BlockSpec((1, NSA_KV_HEADS, LANES, nch), per_b4),
                  pl.BlockSpec((n_sel, nch), lambda b, t: (0, 0)),
                  pl.BlockSpec((seq, 256), lambda b, t: (b, kcol)),
                  pl.BlockSpec((1, NSA_KV_HEADS, LANES, seq), per_b4),
                  pl.BlockSpec((seq, 256), lambda b, t: (b, kcol + 1)),
                  pl.BlockSpec((1, NSA_KV_HEADS, LANES, seq), per_b4)],
        out_specs=pl.BlockSpec((tq, NSA_Q_DIM), row),
        out_shape=jax.ShapeDtypeStruct((T, NSA_Q_DIM), BF16),
        scratch_shapes=[pltpu.VMEM((2, NSA_KV_HEADS, rows, LANES), BF16),
                        pltpu.VMEM((NSA_KV_HEADS * n_sel, tq), F32),
                        pltpu.VMEM((NSA_KV_HEADS, 1, rows), F32), pltpu.VMEM((NSA_KV_HEADS, 1, rows), F32),
                        pltpu.VMEM((NSA_KV_HEADS, LANES, rows), F32),
                        pltpu.VMEM((NSA_KV_HEADS, LANES, rows), F32)],
        compiler_params=_params(("parallel", "arbitrary")), name="nsa_attn")(
            qp, qr, small, kc, vct, ovt, qr, slcvt, qr, winvt)


def _ssd_kernel(xs_ref, bc_ref, z_ref, small_ref, cw_ref, cb_ref, dtb_ref, alog_ref, dexp_ref,
                nw_ref, eh_ref, o_ref, xs_scr, bc_scr, h_scr):
    L = SSM_CHUNK
    P2 = SSM_D_INNER // SSM_GROUPS
    N = SSM_D_STATE
    c = pl.program_id(1)

    @pl.when(c == 0)
    def _():
        xs_scr[0:8, :] = jnp.zeros((8, SSM_D_INNER), F32)
        bc_scr[0:8, :] = jnp.zeros((8, SSM_BC_DIM), F32)
        h_scr[...] = jnp.zeros(h_scr.shape, F32)

    xs_scr[8:8 + L, :] = xs_ref[...].astype(F32)
    bc_scr[8:8 + L, :] = bc_ref[...].astype(F32)

    def conv(scr, col0, width):
        acc = jnp.zeros((L, width), F32) + cb_ref[:, col0:col0 + width]
        for k in range(SSM_CONV):
            acc += scr[8 - (SSM_CONV - 1) + k:8 - (SSM_CONV - 1) + k + L, :] * cw_ref[k:k + 1, col0:col0 + width]
        return _silu(acc)

    xs = conv(xs_scr, 0, SSM_D_INNER)
    bcm = conv(bc_scr, SSM_D_INNER, SSM_BC_DIM)
    xs_scr[0:8, :] = xs_scr[L:L + 8, :]
    bc_scr[0:8, :] = bc_scr[L:L + 8, :]

    lane = lax.broadcasted_iota(jnp.int32, (L, LANES), 1)
    pre = small_ref[...] + dtb_ref[...]
    dt = jnp.maximum(pre, 0.0) + jnp.log(1.0 + jnp.exp(-jnp.abs(pre)))
    dt = jnp.where(lane < SSM_HEADS, dt, 0.0)
    a = dt * (-jnp.exp(alog_ref[...]))
    tri = (lax.broadcasted_iota(jnp.int32, (L, L), 0)
           >= lax.broadcasted_iota(jnp.int32, (L, L), 1))
    tri_b = tri.astype(BF16)
    a_cs = sum(_dot(tri_b, part) for part in _split_bf16(a, 3))
    a_cs_t = a_cs.T
    a_end = a_cs[L - 1:L, :]
    eh = eh_ref[...]
    dt_x = _expand(dt, eh)
    ea_x = _expand(jnp.exp(a_cs), eh)
    de_x = _expand(jnp.exp(a_end - a_cs), eh)
    cd_x = _expand(jnp.broadcast_to(jnp.exp(a_end), (8, LANES)), eh)[0:1]

    X = xs * dt_x
    Xb = X.astype(BF16)
    Xe = (X * de_x).astype(BF16)
    lane_lo = lax.broadcasted_iota(jnp.int32, (L, LANES), 1) < SSM_HEAD_DIM
    y_parts = []
    for g in range(SSM_GROUPS):
        Bg = bcm[:, g * N:(g + 1) * N]
        Cg = bcm[:, (SSM_GROUPS + g) * N:(SSM_GROUPS + g + 1) * N]
        Cb = Cg.astype(BF16)
        cbm = _dot_nt(Cb, Bg.astype(BF16))
        hT = h_scr[g]
        y_off = _dot(Cb, hT.astype(BF16)) * ea_x[:, g * P2:(g + 1) * P2]
        y_dg = []
        for pp in range(P2 // LANES):
            h0 = g * (SSM_HEADS // SSM_GROUPS) + 2 * pp
            acc = None
            for e in range(2):
                h = h0 + e
                seg = a_cs[:, h:h + 1] - a_cs_t[h:h + 1, :]
                dec = jnp.exp(jnp.where(tri, seg, MASK_VALUE))
                m = (cbm * dec).astype(BF16)
                col = g * P2 + pp * LANES
                xh = jnp.where(lane_lo if e == 0 else jnp.logical_not(lane_lo), Xb[:, col:col + LANES],
                               jnp.zeros((), BF16))
                t = _dot(m, xh)
                acc = t if acc is None else acc + t
            y_dg.append(acc)
        y_parts.append(jnp.concatenate(y_dg, axis=1) + y_off)
        st = _dot(Bg.T.astype(BF16), Xe[:, g * P2:(g + 1) * P2])
        h_scr[g] = hT * cd_x[:, g * P2:(g + 1) * P2] + st
    y = jnp.concatenate(y_parts, axis=1) + xs * dexp_ref[...]
    y = y * _silu(z_ref[...].astype(F32))
    outs = []
    for g in range(SSM_GROUPS):
        yg = y[:, g * P2:(g + 1) * P2]
        outs.append(yg * lax.rsqrt(jnp.mean(yg * yg, axis=-1, keepdims=True) + NORM_EPS))
    o_ref[...] = (jnp.concatenate(outs, axis=1) * nw_ref[...]).astype(o_ref.dtype)


def _ssd(n_out, small, cw, cb, dtb, alog, dexp, nw, eh, *, batch, seq):
    L = SSM_CHUNK
    nc = seq // L
    T = batch * seq
    row = lambda b, c: b * nc + c
    full = lambda shape: pl.BlockSpec(shape, lambda b, c: (0,) * len(shape))
    return pl.pallas_call(
        _ssd_kernel, grid=(batch, nc),
        in_specs=[pl.BlockSpec((L, SSM_D_INNER), lambda b, c: (row(b, c), N_XS // SSM_D_INNER)),
                  pl.BlockSpec((L, SSM_BC_DIM), lambda b, c: (row(b, c), N_BC // SSM_BC_DIM)),
                  pl.BlockSpec((L, SSM_D_INNER), lambda b, c: (row(b, c), N_Z // SSM_D_INNER)),
                  pl.BlockSpec((L, LANES), lambda b, c: (row(b, c), 0)),
                  full(cw.shape), full(cb.shape), full(dtb.shape), full(alog.shape),
                  full(dexp.shape), full(nw.shape), full(eh.shape)],
        out_specs=pl.BlockSpec((L, SSM_D_INNER), lambda b, c: (row(b, c), 0)),
        out_shape=jax.ShapeDtypeStruct((T, SSM_D_INNER), BF16),
        scratch_shapes=[pltpu.VMEM((L + 8, SSM_D_INNER), F32), pltpu.VMEM((L + 8, SSM_BC_DIM), F32),
                        pltpu.VMEM((SSM_GROUPS, SSM_D_STATE, SSM_D_INNER // SSM_GROUPS), F32)],
        compiler_params=_params(("parallel", "arbitrary")), name="ssd")(
            n_out, n_out, n_out, small, cw, cb, dtb, alog, dexp, nw, eh)


def _swa_kernel(sink_ref, q_ref, kp_ref, kc_ref, vp_ref, vc_ref, o_ref):
    tq = q_ref.shape[0]
    qt = pl.program_id(1)
    rep = SWA_HEADS // SWA_KV_HEADS
    r_i = lax.broadcasted_iota(jnp.int32, (tq, 2 * tq), 0)
    c_i = lax.broadcasted_iota(jnp.int32, (tq, 2 * tq), 1)
    diff = tq + r_i - c_i
    first_key = jnp.where(qt > 0, 0, tq)
    ok = (diff >= 0) & (diff < SWA_WINDOW) & (c_i >= first_key)
    lane_lo = lax.broadcasted_iota(jnp.int32, (2 * tq, LANES), 1) < SWA_HEAD_DIM
    kf = jnp.concatenate([kp_ref[...], kc_ref[...]], axis=0).astype(F32)
    vf = jnp.concatenate([vp_ref[...], vc_ref[...]], axis=0).astype(F32)
    ks, vs = pltpu.roll(kf, SWA_HEAD_DIM, 1), pltpu.roll(vf, SWA_HEAD_DIM, 1)

    def halves(own, swapped, g):
        a, b = (own, swapped) if g == 0 else (swapped, own)
        return (jnp.where(lane_lo, a, 0.0).astype(BF16), jnp.where(lane_lo, 0.0, b).astype(BF16))

    for g in range(SWA_KV_HEADS):
        k_lo, k_hi = halves(kf, ks, g)
        v_lo, v_hi = halves(vf, vs, g)
        for pp in range(rep // 2):
            col = (g * rep // 2 + pp) * LANES
            q2 = q_ref[:, col:col + LANES]
            acc = None
            for e, (kk, vv) in enumerate(((k_lo, v_lo), (k_hi, v_hi))):
                sk = sink_ref[g * rep + 2 * pp + e]
                s = jnp.where(ok, _dot_nt(q2, kk), MASK_VALUE)
                m = jnp.maximum(jnp.max(s, axis=-1, keepdims=True), sk)
                ex = jnp.exp(s - m)
                p = ex / (jnp.sum(ex, axis=-1, keepdims=True) + jnp.exp(sk - m))
                t = _dot(p.astype(BF16), vv)
                acc = t if acc is None else acc + t
            o_ref[:, col:col + LANES] = acc.astype(o_ref.dtype)


def _swa(sinks, c_out, n_out, *, batch, seq):
    tq = SWA_WINDOW
    nq = seq // tq
    T = batch * seq
    kcol = SWA_Q_DIM // LANES
    vcol = N_SWAV // LANES
    cur = lambda b, t: b * nq + t
    prev = lambda b, t: b * nq + jnp.maximum(t - 1, 0)
    return pl.pallas_call(
        _swa_kernel, grid=(batch, nq),
        in_specs=[pl.BlockSpec(memory_space=pltpu.SMEM),
                  pl.BlockSpec((tq, SWA_Q_DIM), lambda b, t: (cur(b, t), 0)),
                  pl.BlockSpec((tq, LANES), lambda b, t: (prev(b, t), kcol)),
                  pl.BlockSpec((tq, LANES), lambda b, t: (cur(b, t), kcol)),
                  pl.BlockSpec((tq, LANES), lambda b, t: (prev(b, t), vcol)),
                  pl.BlockSpec((tq, LANES), lambda b, t: (cur(b, t), vcol))],
        out_specs=pl.BlockSpec((tq, SWA_Q_DIM), lambda b, t: (cur(b, t), 0)),
        out_shape=jax.ShapeDtypeStruct((T, SWA_Q_DIM), BF16),
        compiler_params=_params(("parallel", "arbitrary")), name="swa")(
            sinks, c_out, c_out, c_out, n_out, n_out)


def _merge_kernel(oa_ref, ob_ref, oc_ref, mg_ref, pa_ref, pb_ref, pc_ref, y_ref):
    D = D_MODEL
    y = _sigmoid(mg_ref[:, 0:D].astype(F32)) * _dot(oa_ref[...], pa_ref[...])
    y += _sigmoid(mg_ref[:, D:2 * D].astype(F32)) * _dot(ob_ref[...], pb_ref[...])
    y += _sigmoid(mg_ref[:, 2 * D:3 * D].astype(F32)) * _dot(oc_ref[...], pc_ref[...])
    y_ref[...] = y.astype(y_ref.dtype)


def _merge(oa, ob, oc, n_out, pa, pb, pc, *, tm):
    T = oa.shape[0]
    rowblk = lambda w: pl.BlockSpec((tm, w), lambda i: (i, 0))
    const = lambda a: pl.BlockSpec(a.shape, lambda i: (0,) * a.ndim, pipeline_mode=pl.Buffered(1))
    return pl.pallas_call(
        _merge_kernel, grid=(T // tm,),
        in_specs=[rowblk(NSA_Q_DIM), rowblk(SSM_D_INNER), rowblk(SWA_Q_DIM), rowblk(3 * D_MODEL),
                  const(pa), const(pb), const(pc)],
        out_specs=rowblk(D_MODEL),
        out_shape=jax.ShapeDtypeStruct((T, D_MODEL), BF16),
        compiler_params=_params(("parallel",)), name="merge")(oa, ob, oc, n_out, pa, pb, pc)


def _outproj_kernel(x_ref, y_ref, wo_ref, nw_ref, wrh_ref, wrl_ref, rb_ref, xo_ref, hn_ref, route_ref):
    x = x_ref[...] + _dot(y_ref[...], wo_ref[...])
    xo_ref[...] = x
    ms = jnp.mean(x * x, axis=-1, keepdims=True)
    hn = x * lax.rsqrt(ms + NORM_EPS) * nw_ref[...]
    hn_hi, hn_lo = _split_bf16(hn, 2)
    hn_ref[...] = hn_hi
    logit = (_dot(hn_hi, wrh_ref[...]) + _dot(hn_lo, wrh_ref[...]) + _dot(hn_hi, wrl_ref[...])
             + rb_ref[...])
    tm = logit.shape[0]
    lane = lax.broadcasted_iota(jnp.int32, (tm, LANES), 1)
    big = jnp.int32(LANES)
    gl = jnp.where(lane < MOE_GROUPS, logit, -jnp.inf)
    gmax = jnp.max(gl, axis=-1, keepdims=True)
    gidx = jnp.min(jnp.where(gl == gmax, lane, big), axis=-1, keepdims=True)
    gw = 1.0 / jnp.sum(jnp.exp(gl - gmax), axis=-1, keepdims=True)
    lo = MOE_GROUPS + MOE_EXPERTS_PER_GROUP * gidx
    el = jnp.where((lane >= lo) & (lane < lo + MOE_EXPERTS_PER_GROUP), logit, -jnp.inf)
    m1 = jnp.max(el, axis=-1, keepdims=True)
    i1 = jnp.min(jnp.where(el == m1, lane, big), axis=-1, keepdims=True)
    el2 = jnp.where(lane == i1, -jnp.inf, el)
    m2 = jnp.max(el2, axis=-1, keepdims=True)
    i2 = jnp.min(jnp.where(el2 == m2, lane, big), axis=-1, keepdims=True)
    e2 = jnp.exp(m2 - m1)
    w1 = gw / (1.0 + e2)
    w2 = gw * e2 / (1.0 + e2)
    route = jnp.where(lane == 0, (i1 - MOE_GROUPS).astype(F32),
                      jnp.where(lane == 1, (i2 - MOE_GROUPS).astype(F32),
                                jnp.where(lane == 2, w1, jnp.where(lane == 3, w2, 0.0))))
    route_ref[...] = route


def _outproj(x2d, y, wo, nw, wrh, wrl, rb, *, tm):
    T, D = x2d.shape
    rowblk = lambda w: pl.BlockSpec((tm, w), lambda i: (i, 0))
    const = lambda a: pl.BlockSpec(a.shape, lambda i: (0,) * a.ndim, pipeline_mode=pl.Buffered(1))
    return pl.pallas_call(
        _outproj_kernel, grid=(T // tm,),
        in_specs=[rowblk(D), rowblk(D), const(wo), const(nw), const(wrh), const(wrl), const(rb)],
        out_specs=(rowblk(D), rowblk(D), rowblk(LANES)),
        out_shape=(jax.ShapeDtypeStruct((T, D), F32), jax.ShapeDtypeStruct((T, D), BF16),
                   jax.ShapeDtypeStruct((T, LANES), F32)),
        compiler_params=_params(("parallel",)), name="outproj")(x2d, y, wo, nw, wrh, wrl, rb)


def _expert_kernel(te_ref, x_ref, wgu_ref, wd_ref, y_ref):
    gu = _dot(x_ref[...], wgu_ref[0])
    act = (_silu(gu[:, :MOE_D_FF]) * gu[:, MOE_D_FF:]).astype(BF16)
    y_ref[...] = _dot(act, wd_ref[0]).astype(y_ref.dtype)


def _experts(tile_expert, xs, wgu, wd):
    P, D = xs.shape
    tm = MOE_ROW_TILE
    grid_spec = pltpu.PrefetchScalarGridSpec(
        num_scalar_prefetch=1, grid=(P // tm,),
        in_specs=[pl.BlockSpec((tm, D), lambda i, te: (i, 0)),
                  pl.BlockSpec((1, D, 2 * MOE_D_FF), lambda i, te: (te[i], 0, 0)),
                  pl.BlockSpec((1, MOE_D_FF, D), lambda i, te: (te[i], 0, 0))],
        out_specs=pl.BlockSpec((tm, D), lambda i, te: (i, 0)))
    return pl.pallas_call(
        _expert_kernel, grid_spec=grid_spec,
        out_shape=jax.ShapeDtypeStruct((P, D), BF16),
        compiler_params=_params(("arbitrary",)), name="experts")(tile_expert, xs, wgu, wd)


def _combine_kernel(x_ref, y0_ref, y1_ref, route_ref, nw_ref, o_ref, *, final):
    r = route_ref[...]
    x = x_ref[...] + r[:, 2:3] * y0_ref[...].astype(F32) + r[:, 3:4] * y1_ref[...].astype(F32)
    if final:
        ms = jnp.mean(x * x, axis=-1, keepdims=True)
        x = x * lax.rsqrt(ms + NORM_EPS) * nw_ref[...]
    o_ref[...] = x


def _combine(x2d, y0, y1, route, nw, *, final, tm):
    T, D = x2d.shape
    rowblk = lambda w: pl.BlockSpec((tm, w), lambda i: (i, 0))
    return pl.pallas_call(
        functools.partial(_combine_kernel, final=final), grid=(T // tm,),
        in_specs=[rowblk(D), rowblk(D), rowblk(D), rowblk(LANES),
                  pl.BlockSpec((1, D), lambda i: (0, 0))],
        out_specs=rowblk(D), out_shape=jax.ShapeDtypeStruct((T, D), F32),
        compiler_params=_params(("parallel",)), name="combine")(x2d, y0, y1, route, nw)


def _rope_tables(seq):
    def tab(dim):
        inv = 1.0 / (ROPE_THETA ** (jnp.arange(0, dim, 2, dtype=F32) / dim))
        ang = jnp.arange(seq, dtype=F32)[:, None] * inv[None, :]
        return jnp.cos(ang), jnp.sin(ang)
    ca, sa = tab(NSA_HEAD_DIM)
    cc, sc = tab(SWA_HEAD_DIM)
    z = jnp.zeros_like(sc)
    tabs_a = (jnp.concatenate([ca, ca], 1), jnp.concatenate([-sa, sa], 1))
    tabs_c = (jnp.concatenate([cc] * 4, 1), jnp.concatenate([-sc, z, -sc, z], 1),
              jnp.concatenate([z, sc, z, sc], 1))
    return tabs_a, tabs_c


def _overlap_t(seq):
    nch = seq // NSA_CMP_STRIDE
    n_sel = seq // NSA_SLC_BLOCK
    cs = np.arange(nch) * NSA_CMP_STRIDE
    ce = cs + NSA_CMP_LEN - 1
    ss = np.arange(n_sel) * NSA_SLC_BLOCK
    ov = (cs[None, :] <= ss[:, None] + NSA_SLC_BLOCK - 1) & (ce[None, :] >= ss[:, None])
    ov[:, nch - 1] = False
    return jnp.asarray(ov.astype(np.float32), BF16)


def _head_expand():
    e = np.zeros((LANES, SSM_D_INNER), np.float32)
    for h in range(SSM_HEADS):
        e[h, h * SSM_HEAD_DIM:(h + 1) * SSM_HEAD_DIM] = 1.0
    return jnp.asarray(e, BF16)


def _split_w_in(w_in):
    o = np.cumsum([0, NSA_Q_DIM, 1536, 24, SSM_D_INNER, SSM_D_INNER + SSM_BC_DIM, SSM_HEADS,
                   SWA_Q_DIM, 256, 3 * D_MODEL])
    seg = lambda a, b: w_in[:, a:b]
    nsa_q = seg(o[0], o[1])
    kv = o[1]
    cmp_kv, slc_k, slc_v = seg(kv, kv + 512), seg(kv + 512, kv + 768), seg(kv + 768, kv + 1024)
    win_k, win_v = seg(kv + 1024, kv + 1280), seg(kv + 1280, kv + 1536)
    nsa_g = seg(o[2], o[3])
    ssm_z = seg(o[3], o[4])
    ssm_xs, ssm_bc = seg(o[4], o[4] + SSM_D_INNER), seg(o[4] + SSM_D_INNER, o[5])
    ssm_dt = seg(o[5], o[6])
    swa_q = seg(o[6], o[7])
    swa_k, swa_v = seg(o[7], o[7] + 128), seg(o[7] + 128, o[8])
    merge_g = seg(o[8], o[9])
    w_a = jnp.concatenate([nsa_q, slc_k, win_k], 1).astype(BF16)
    w_c = jnp.concatenate([swa_q, swa_k], 1).astype(BF16)
    w_n = jnp.concatenate([merge_g, ssm_z, ssm_xs, ssm_bc, cmp_kv, slc_v, win_v, swa_v], 1).astype(BF16)
    pad = jnp.zeros((w_in.shape[0], LANES - SSM_HEADS - 24), w_in.dtype)
    w_s = jnp.concatenate([ssm_dt, nsa_g, pad], 1).astype(BF16)
    return w_a, w_c, w_n, w_s


def _pad_lanes(v):
    return jnp.pad(v, (0, LANES - v.shape[0]))[None, :]


def _kv_transposed(n_out, col0, batch, seq):
    v = n_out[:, col0:col0 + NSA_KV_HEADS * LANES].reshape(batch, seq, NSA_KV_HEADS, LANES)
    return jnp.transpose(v, (0, 2, 3, 1))


def _dispatch(route, n_tok):
    tm = MOE_ROW_TILE
    n_asg = n_tok * MOE_TOPK
    n_rows = n_asg + MOE_EXPERTS * tm
    flat_e = route[:, 0:MOE_TOPK].astype(jnp.int32).reshape(-1)
    order = jnp.argsort(flat_e, stable=True)
    sorted_e = flat_e[order]
    counts = jnp.zeros((MOE_EXPERTS,), jnp.int32).at[flat_e].add(1)
    padded = ((counts + tm - 1) // tm) * tm
    pend = jnp.cumsum(padded)
    pstart = pend - padded
    start = jnp.cumsum(counts) - counts
    dest = pstart[sorted_e] + (jnp.arange(n_asg, dtype=jnp.int32) - start[sorted_e])
    row_token = jnp.zeros((n_rows,), jnp.int32).at[dest].set(order // MOE_TOPK)
    pos = jnp.zeros((n_asg,), jnp.int32).at[order].set(dest).reshape(n_tok, MOE_TOPK)
    tile_start = jnp.arange(n_rows // tm, dtype=jnp.int32) * tm
    tile_expert = jnp.minimum(jnp.searchsorted(pend, tile_start, side='right'),
                              MOE_EXPERTS - 1).astype(jnp.int32)
    return row_token, pos, tile_expert


def kernel(x, norm_mix, norm_ffn, w_in, nsa_cmp_pos, nsa_cmp_w1, nsa_cmp_b1, nsa_cmp_w2, ssm_conv_w,
           ssm_conv_b, ssm_dt_bias, ssm_a_log, ssm_d, ssm_norm, swa_sinks, proj_nsa, proj_ssm, proj_swa,
           w_out, moe_group_router, moe_group_bias, moe_expert_router, moe_expert_bias, moe_w_gate,
           moe_w_up, moe_w_down, final_norm):
    B, S, D = x.shape
    T = B * S
    depth = w_in.shape[0]
    tm = 512
    tabs_a, tabs_c = _rope_tables(S)
    ovt = _overlap_t(S)
    eh = _head_expand()
    scale_a = jnp.concatenate([jnp.full((NSA_Q_DIM,), NSA_HEAD_DIM ** -0.5, F32),
                               jnp.ones((512,), F32)])[None, :]
    scale_c = jnp.concatenate([jnp.full((SWA_Q_DIM,), SWA_HEAD_DIM ** -0.5, F32),
                               jnp.ones((LANES,), F32)])[None, :]
    xc = x.reshape(T, D)
    for l in range(depth):
        w_a, w_c, w_n, w_s = _split_w_in(w_in[l])
        nw = norm_mix[l][None, :]
        qr, qp = _inproj(xc, nw, w_a, seq=S, tm=tm, tn=512, rope='a', scale=scale_a, tabs=tabs_a,
                         out_dtypes=(BF16, BF16))
        (c_out,) = _inproj(xc, nw, w_c, seq=S, tm=tm, tn=w_c.shape[1], rope='c', scale=scale_c,
                           tabs=tabs_c)
        (n_out,) = _inproj(xc, nw, w_n, seq=S, tm=tm, tn=896)
        (small,) = _inproj(xc, nw, w_s, seq=S, tm=tm, tn=LANES, out_dtypes=(F32,))

        w1r = nsa_cmp_w1[l].reshape(2, NSA_CMP_LEN, NSA_HEAD_DIM, NSA_CMP_HIDDEN).astype(BF16)
        kc, vct = _cmp_mlp(n_out, nsa_cmp_pos[l], w1r, nsa_cmp_b1[l][:, None, :],
                           nsa_cmp_w2[l, 0].astype(BF16), nsa_cmp_w2[l, 1].T.astype(BF16), batch=B, seq=S)
        o_a = _nsa_attn(qp, qr, small, kc, vct, ovt, _kv_transposed(n_out, N_SLCV, B, S),
                        _kv_transposed(n_out, N_WINV, B, S), batch=B, seq=S)

        dexp = jnp.repeat(ssm_d[l], SSM_HEAD_DIM)[None, :]
        o_b = _ssd(n_out, small, ssm_conv_w[l], ssm_conv_b[l][None, :], _pad_lanes(ssm_dt_bias[l]),
                   _pad_lanes(ssm_a_log[l]), dexp, ssm_norm[l][None, :], eh, batch=B, seq=S)
        o_c = _swa(swa_sinks[l], c_out, n_out, batch=B, seq=S)

        y = _merge(o_a, o_b, o_c, n_out, proj_nsa[l].astype(BF16), proj_ssm[l].astype(BF16),
                   proj_swa[l].astype(BF16), tm=256)
        wr = jnp.pad(jnp.concatenate([moe_group_router[l], moe_expert_router[l]], 1),
                     ((0, 0), (0, LANES - MOE_GROUPS - MOE_EXPERTS)))
        wrh = wr.astype(BF16)
        wrl = (wr - wrh.astype(F32)).astype(BF16)
        rb = _pad_lanes(jnp.concatenate([moe_group_bias[l], moe_expert_bias[l]]))
        x_mid, hn, route = _outproj(xc, y, w_out[l].astype(BF16), norm_ffn[l][None, :], wrh, wrl, rb,
                                    tm=256)

        row_token, pos, tile_expert = _dispatch(route, T)
        wgu = jnp.concatenate([moe_w_gate[l], moe_w_up[l]], -1).reshape(
            MOE_EXPERTS, D, 2 * MOE_D_FF).astype(BF16)
        wd = moe_w_down[l].reshape(MOE_EXPERTS, MOE_D_FF, D).astype(BF16)
        ys = _experts(tile_expert, hn[row_token], wgu, wd)
        xc = _combine(x_mid, ys[pos[:, 0]], ys[pos[:, 1]], route, final_norm[None, :],
                      final=(l == depth - 1), tm=tm)
    return xc.reshape(B, S, D)
```

```python
import functools

import jax
import jax.numpy as jnp
import numpy as np
from jax import lax
from jax.experimental import pallas as pl
from jax.experimental.pallas import tpu as pltpu

F32 = jnp.float32
BF16 = jnp.bfloat16

D_MODEL = 2048
ROPE_THETA = 10000.0
NORM_EPS = 1e-6
MASK_VALUE = -1e30
FORCE_SCORE = 1e6

NSA_HEADS = 8
NSA_KV_HEADS = 2
NSA_REP = NSA_HEADS // NSA_KV_HEADS
NSA_HEAD_DIM = 128
NSA_CMP_STRIDE = 16
NSA_CMP_LEN = 32
NSA_CMP_HIDDEN = 256
NSA_SLC_BLOCK = 64
NSA_SLC_TOPK = 16
NSA_WINDOW = 512
NSA_Q_DIM = NSA_HEADS * NSA_HEAD_DIM

SSM_D_INNER = 1024
SSM_HEAD_DIM = 64
SSM_HEADS = 16
SSM_GROUPS = 2
SSM_D_STATE = 128
SSM_CONV = 4
SSM_CHUNK = 128
SSM_BC_DIM = 2 * SSM_GROUPS * SSM_D_STATE

SWA_HEADS = 16
SWA_KV_HEADS = 2
SWA_HEAD_DIM = 64
SWA_WINDOW = 128
SWA_Q_DIM = SWA_HEADS * SWA_HEAD_DIM

MOE_GROUPS = 4
MOE_EXPERTS_PER_GROUP = 8
MOE_EXPERTS = MOE_GROUPS * MOE_EXPERTS_PER_GROUP
MOE_TOPK = 2
MOE_D_FF = 512

LANES = 128
ATT_Q_TILE = 128
SLC_K_TILE = 512
MOE_ROW_TILE = 256
VMEM_LIMIT = 56 * 1024 * 1024

N_MERGE = 0
N_Z = 6144
N_XS = 7168
N_BC = 8192
N_CMP = 8704
N_SLCV = 9216
N_WINV = 9472
N_SWAV = 9728
N_TOTAL = 9856
SMALL_GATE0 = 16


def _params(sem):
    return pltpu.CompilerParams(dimension_semantics=sem, vmem_limit_bytes=VMEM_LIMIT)


def _dot(a, b):
    return jnp.dot(a, b, preferred_element_type=F32)


def _dot_nt(a, b):
    return lax.dot_general(a, b, (((1,), (1,)), ((), ())), preferred_element_type=F32)


def _split_bf16(v, n):
    parts = []
    for _ in range(n):
        p = v.astype(BF16)
        parts.append(p)
        v = v - p.astype(F32)
    return parts


def _expand(v, e):
    hi, lo = _split_bf16(v, 2)
    return _dot(hi, e) + _dot(lo, e)


def _sigmoid(v):
    return 1.0 / (1.0 + jnp.exp(-v))


def _silu(v):
    return v * _sigmoid(v)


def _prenorm_kernel(x_ref, nw_ref, hn_ref):
    x = x_ref[...]
    ms = jnp.mean(x * x, axis=-1, keepdims=True)
    hn_ref[...] = (x * lax.rsqrt(ms + NORM_EPS) * nw_ref[...]).astype(hn_ref.dtype)


def _prenorm(x2d, nw, *, tm):
    T, D = x2d.shape
    return pl.pallas_call(
        _prenorm_kernel, grid=(T // tm,),
        in_specs=[pl.BlockSpec((tm, D), lambda i: (i, 0)), pl.BlockSpec((1, D), lambda i: (0, 0))],
        out_specs=pl.BlockSpec((tm, D), lambda i: (i, 0)),
        out_shape=jax.ShapeDtypeStruct((T, D), BF16),
        compiler_params=_params(("parallel",)), name="prenorm")(x2d, nw)


def _inproj_kernel(*refs, rope, has_scale, n_out):
    it = iter(refs)
    hn_ref, w_ref = next(it), next(it)
    cs_ref = next(it) if has_scale else None
    tabs = [next(it) for _ in range({None: 0, 'a': 2, 'c': 3}[rope])]
    outs = [next(it) for _ in range(n_out)]

    acc = _dot(hn_ref[...], w_ref[...])
    if has_scale:
        acc = acc * cs_ref[...]
    if rope is None:
        outs[0][...] = acc.astype(outs[0].dtype)
        return
    if n_out == 2:
        outs[1][...] = acc.astype(outs[1].dtype)
    for c in range(acc.shape[1] // LANES):
        a = acc[:, c * LANES:(c + 1) * LANES]
        if rope == 'a':
            r = a * tabs[0][...] + pltpu.roll(a, 64, 1) * tabs[1][...]
        else:
            r = (a * tabs[0][...] + pltpu.roll(a, 96, 1) * tabs[1][...]
                 + pltpu.roll(a, 32, 1) * tabs[2][...])
        outs[0][:, c * LANES:(c + 1) * LANES] = r.astype(outs[0].dtype)


def _inproj(hn, w, *, seq, tm, tn, rope=None, scale=None, tabs=(), out_dtypes=(BF16,)):
    T, D = hn.shape
    N = w.shape[1]
    nrow = seq // tm
    in_specs = [pl.BlockSpec((tm, D), lambda i, j: (i, 0)),
                pl.BlockSpec((D, tn), lambda i, j: (0, j))]
    args = [hn, w]
    if scale is not None:
        in_specs.append(pl.BlockSpec((1, tn), lambda i, j: (0, j)))
        args.append(scale)
    for t in tabs:
        in_specs.append(pl.BlockSpec((tm, LANES), lambda i, j: (i % nrow, 0)))
        args.append(t)
    out_shape = tuple(jax.ShapeDtypeStruct((T, N), dt) for dt in out_dtypes)
    out_specs = tuple(pl.BlockSpec((tm, tn), lambda i, j: (i, j)) for _ in out_dtypes)
    kern = functools.partial(_inproj_kernel, rope=rope, has_scale=scale is not None,
                             n_out=len(out_dtypes))
    return pl.pallas_call(
        kern, grid=(T // tm, N // tn), in_specs=in_specs, out_specs=out_specs, out_shape=out_shape,
        compiler_params=_params(("parallel", "arbitrary")), name="inproj_" + str(rope))(*args)


def _cmp_mlp_kernel(k_ref, v_ref, pos_ref, w1_ref, b1_ref, w2k_ref, w2vt_ref, kc_ref, vct_ref, f32_scr):
    nch = kc_ref.shape[2]
    for c, src in enumerate((k_ref, v_ref)):
        f32_scr[...] = src[...].astype(F32)
        first = jnp.zeros((nch, NSA_CMP_HIDDEN), F32)
        second = jnp.zeros((nch, NSA_CMP_HIDDEN), F32)
        for t in range(NSA_CMP_STRIDE):
            xt = f32_scr[pl.ds(t, nch, stride=NSA_CMP_STRIDE), :]
            first += _dot((xt + pos_ref[c, t:t + 1, :]).astype(BF16), w1_ref[c, t])
            t2 = NSA_CMP_STRIDE + t
            second += _dot((xt + pos_ref[c, t2:t2 + 1, :]).astype(BF16), w1_ref[c, t2])
        hid = _silu(first + pltpu.roll(second, nch - 1, 0) + b1_ref[c]).astype(BF16)
        if c == 0:
            kc_ref[0, 0] = _dot(hid, w2k_ref[...]).astype(kc_ref.dtype)
        else:
            vct_ref[0, 0] = _dot_nt(w2vt_ref[...], hid).astype(vct_ref.dtype)


def _cmp_mlp(n_out, pos, w1r, b1, w2k, w2vt, *, batch, seq):
    nch = seq // NSA_CMP_STRIDE
    cb0 = N_CMP // LANES
    full = lambda a: pl.BlockSpec(a.shape, lambda b, g: (0,) * a.ndim)
    return pl.pallas_call(
        _cmp_mlp_kernel, grid=(batch, NSA_KV_HEADS),
        in_specs=[pl.BlockSpec((seq, LANES), lambda b, g: (b, cb0 + g)),
                  pl.BlockSpec((seq, LANES), lambda b, g: (b, cb0 + NSA_KV_HEADS + g)),
                  full(pos), full(w1r), full(b1), full(w2k), full(w2vt)],
        out_specs=(pl.BlockSpec((1, 1, nch, LANES), lambda b, g: (b, g, 0, 0)),
                   pl.BlockSpec((1, 1, LANES, nch), lambda b, g: (b, g, 0, 0))),
        out_shape=(jax.ShapeDtypeStruct((batch, NSA_KV_HEADS, nch, LANES), BF16),
                   jax.ShapeDtypeStruct((batch, NSA_KV_HEADS, LANES, nch), BF16)),
        scratch_shapes=[pltpu.VMEM((seq, LANES), F32)],
        compiler_params=_params(("parallel", "arbitrary")), name="nsa_cmp_mlp")(
            n_out, n_out, pos, w1r, b1, w2k, w2vt)


def _nsa_attn_kernel(qp_ref, qr_ref, small_ref, kc_ref, vct_ref, ovt_ref, slck_ref, slcvt_ref, wink_ref,
                     winvt_ref, o_ref, q4_scr, sel_scr, m_scr, l_scr, acc_scr, part_scr, *, topk):
    tq = qp_ref.shape[0]
    nch = kc_ref.shape[2]
    n_sel = ovt_ref.shape[0]
    rep = NSA_REP
    tk = SLC_K_TILE
    wlen = NSA_WINDOW + tq
    qt = pl.program_id(1)
    start = qt * tq

    def tile4(v):
        return jnp.concatenate([v] * rep, axis=1)

    def qpos(rows):
        return start + lax.broadcasted_iota(jnp.int32, (rows, tq), 1)

    def sub(rows):
        return lax.broadcasted_iota(jnp.int32, (rows, tq), 0)

    for g in range(NSA_KV_HEADS):
        for r in range(rep):
            h = g * rep + r
            q4_scr[0, g, r * tq:(r + 1) * tq, :] = qp_ref[:, h * LANES:(h + 1) * LANES]
            q4_scr[1, g, r * tq:(r + 1) * tq, :] = qr_ref[:, h * LANES:(h + 1) * LANES]

    gates = _sigmoid(small_ref[...]).T

    def gate4(g, br):
        return jnp.concatenate([gates[SMALL_GATE0 + 3 * (g * rep + r) + br:SMALL_GATE0 + 3 * (g * rep + r) + br + 1, :]
                                for r in range(rep)], axis=1)

    vis = sub(nch) * NSA_CMP_STRIDE + (NSA_CMP_LEN - 1) <= qpos(nch)
    vis_bias = tile4(jnp.where(vis, 0.0, MASK_VALUE))
    vis_one = tile4(jnp.where(vis, 1.0, 0.0))
    jj = sub(n_sel)
    pos_t = qpos(n_sel)
    qblk = pos_t // NSA_SLC_BLOCK
    causal_blk = jj * NSA_SLC_BLOCK <= pos_t
    forced = (jj == 0) | (jj == qblk) | (jj == qblk - 1)
    for g in range(NSA_KV_HEADS):
        s = _dot_nt(kc_ref[0, g], q4_scr[0, g]) + vis_bias
        e = jnp.exp(s - jnp.max(s, axis=0, keepdims=True))
        p = e * (1.0 / jnp.sum(e, axis=0, keepdims=True)) * vis_one
        part_scr[g] = gate4(g, 0) * _dot(vct_ref[0, g], p.astype(BF16))
        psum = p[:, 0:tq]
        for r in range(1, rep):
            psum = psum + p[:, r * tq:(r + 1) * tq]
        imp = sum(_dot(ovt_ref[...], part) for part in _split_bf16(psum, 3))
        imp = jnp.where(causal_blk, imp, MASK_VALUE)
        imp = jnp.where(forced, FORCE_SCORE, imp)
        rank = jnp.zeros((n_sel, tq), F32)
        for jp in range(n_sel):
            row = imp[jp:jp + 1, :]
            rank += jnp.where(jj > jp, jnp.where(row >= imp, 1.0, 0.0), jnp.where(row > imp, 1.0, 0.0))
        sel_scr[g * n_sel:(g + 1) * n_sel, :] = jnp.where(rank < topk, 1.0, 0.0)

    wstart = pl.multiple_of(jnp.maximum(qt - NSA_WINDOW // tq, 0) * tq, tq)
    diff = qpos(wlen) - (wstart + sub(wlen))
    wbias = tile4(jnp.where(diff >= 0, jnp.where(diff < NSA_WINDOW, 0.0, MASK_VALUE), MASK_VALUE))
    for g in range(NSA_KV_HEADS):
        s = _dot_nt(wink_ref[pl.ds(wstart, wlen), g * LANES:(g + 1) * LANES], q4_scr[1, g]) + wbias
        e = jnp.exp(s - jnp.max(s, axis=0, keepdims=True))
        o = _dot(winvt_ref[0, g, :, pl.ds(wstart, wlen)], e.astype(BF16))
        part_scr[g] = part_scr[g] + gate4(g, 2) * (o * (1.0 / jnp.sum(e, axis=0, keepdims=True)))

    m_scr[...] = jnp.full(m_scr.shape, MASK_VALUE, F32)
    l_scr[...] = jnp.zeros(l_scr.shape, F32)
    acc_scr[...] = jnp.zeros(acc_scr.shape, F32)
    blocks_per_tile = tk // NSA_SLC_BLOCK

    def body(kt, carry):
        base = pl.multiple_of(kt * tk, tk)
        causal_bias = jnp.where(base + sub(tk) <= qpos(tk), 0.0, MASK_VALUE)
        for g in range(NSA_KV_HEADS):
            s = _dot_nt(slck_ref[pl.ds(base, tk), g * LANES:(g + 1) * LANES], q4_scr[1, g])
            picked = jnp.concatenate(
                [jnp.broadcast_to(sel_scr[pl.ds(g * n_sel + kt * blocks_per_tile + i, 1), :],
                                  (NSA_SLC_BLOCK, tq)) for i in range(blocks_per_tile)], axis=0)
            s = s + tile4(jnp.where(picked > 0.5, causal_bias, MASK_VALUE))
            m_prev = m_scr[g]
            m_next = jnp.maximum(m_prev, jnp.max(s, axis=0, keepdims=True))
            alpha = jnp.exp(m_prev - m_next)
            p = jnp.exp(s - m_next)
            l_scr[g] = alpha * l_scr[g] + jnp.sum(p, axis=0, keepdims=True)
            acc_scr[g] = acc_scr[g] * alpha + _dot(slcvt_ref[0, g, :, pl.ds(base, tk)], p.astype(BF16))
            m_scr[g] = m_next
        return carry

    lax.fori_loop(0, (start + tq + tk - 1) // tk, body, 0)

    for g in range(NSA_KV_HEADS):
        o = part_scr[g] + gate4(g, 1) * (acc_scr[g] * (1.0 / l_scr[g]))
        for r in range(rep):
            h = g * rep + r
            o_ref[:, h * LANES:(h + 1) * LANES] = o[:, r * tq:(r + 1) * tq].T.astype(o_ref.dtype)


def _nsa_attn(qp, qr, small, kc, vct, ovt, slcvt, winvt, *, batch, seq):
    tq = ATT_Q_TILE
    assert seq >= NSA_WINDOW + tq and seq % SLC_K_TILE == 0
    nq = seq // tq
    nch = seq // NSA_CMP_STRIDE
    n_sel = seq // NSA_SLC_BLOCK
    T = batch * seq
    row = lambda b, t: (b * nq + t, 0)
    per_b4 = lambda b, t: (b, 0, 0, 0)
    kcol = NSA_Q_DIM // 256
    rows = tq * NSA_REP
    kern = functools.partial(_nsa_attn_kernel, topk=min(NSA_SLC_TOPK, n_sel))
    return pl.pallas_call(
        kern, grid=(batch, nq),
        in_specs=[pl.BlockSpec((tq, NSA_Q_DIM), row), pl.BlockSpec((tq, NSA_Q_DIM), row),
                  pl.BlockSpec((tq, LANES), row),
                  pl.BlockSpec((1, NSA_KV_HEADS, nch, LANES), per_b4),
                  pl.BlockSpec((1, NSA_KV_HEADS, LANES, nch), per_b4),
                  pl.BlockSpec((n_sel, nch), lambda b, t: (0, 0)),
                  pl.BlockSpec((seq, 256), lambda b, t: (b, kcol)),
                  pl.BlockSpec((1, NSA_KV_HEADS, LANES, seq), per_b4),
                  pl.BlockSpec((seq, 256), lambda b, t: (b, kcol + 1)),
                  pl.BlockSpec((1, NSA_KV_HEADS, LANES, seq), per_b4)],
        out_specs=pl.BlockSpec((tq, NSA_Q_DIM), row),
        out_shape=jax.ShapeDtypeStruct((T, NSA_Q_DIM), BF16),
        scratch_shapes=[pltpu.VMEM((2, NSA_KV_HEADS, rows, LANES), BF16),
                        pltpu.VMEM((NSA_KV_HEADS * n_sel, tq), F32),
                        pltpu.VMEM((NSA_KV_HEADS, 1, rows), F32), pltpu.VMEM((NSA_KV_HEADS, 1, rows), F32),
                        pltpu.VMEM((NSA_KV_HEADS, LANES, rows), F32),
                        pltpu.VMEM((NSA_KV_HEADS, LANES, rows), F32)],
        compiler_params=_params(("parallel", "arbitrary")), name="nsa_attn")(
            qp, qr, small, kc, vct, ovt, qr, slcvt, qr, winvt)


def _ssd_kernel(xs_ref, bc_ref, z_ref, small_ref, cw_ref, cb_ref, dtb_ref, alog_ref, dexp_ref,
                nw_ref, eh_ref, o_ref, xs_scr, bc_scr, h_scr):
    L = SSM_CHUNK
    P2 = SSM_D_INNER // SSM_GROUPS
    N = SSM_D_STATE
    c = pl.program_id(1)

    @pl.when(c == 0)
    def _():
        xs_scr[0:8, :] = jnp.zeros((8, SSM_D_INNER), F32)
        bc_scr[0:8, :] = jnp.zeros((8, SSM_BC_DIM), F32)
        h_scr[...] = jnp.zeros(h_scr.shape, F32)

    xs_scr[8:8 + L, :] = xs_ref[...].astype(F32)
    bc_scr[8:8 + L, :] = bc_ref[...].astype(F32)

    def conv(scr, col0, width):
        acc = jnp.zeros((L, width), F32) + cb_ref[:, col0:col0 + width]
        for k in range(SSM_CONV):
            acc += scr[8 - (SSM_CONV - 1) + k:8 - (SSM_CONV - 1) + k + L, :] * cw_ref[k:k + 1, col0:col0 + width]
        return _silu(acc)

    xs = conv(xs_scr, 0, SSM_D_INNER)
    bcm = conv(bc_scr, SSM_D_INNER, SSM_BC_DIM)
    xs_scr[0:8, :] = xs_scr[L:L + 8, :]
    bc_scr[0:8, :] = bc_scr[L:L + 8, :]

    lane = lax.broadcasted_iota(jnp.int32, (L, LANES), 1)
    pre = small_ref[...] + dtb_ref[...]
    dt = jnp.maximum(pre, 0.0) + jnp.log(1.0 + jnp.exp(-jnp.abs(pre)))
    dt = jnp.where(lane < SSM_HEADS, dt, 0.0)
    a = dt * (-jnp.exp(alog_ref[...]))
    tri = (lax.broadcasted_iota(jnp.int32, (L, L), 0)
           >= lax.broadcasted_iota(jnp.int32, (L, L), 1))
    tri_b = tri.astype(BF16)
    a_cs = sum(_dot(tri_b, part) for part in _split_bf16(a, 3))
    a_cs_t = a_cs.T
    a_end = a_cs[L - 1:L, :]
    eh = eh_ref[...]
    dt_x = _expand(dt, eh)
    ea_x = _expand(jnp.exp(a_cs), eh)
    de_x = _expand(jnp.exp(a_end - a_cs), eh)
    cd_x = _expand(jnp.broadcast_to(jnp.exp(a_end), (8, LANES)), eh)[0:1]

    X = xs * dt_x
    Xb = X.astype(BF16)
    Xe = (X * de_x).astype(BF16)
    lane_lo = lax.broadcasted_iota(jnp.int32, (L, LANES), 1) < SSM_HEAD_DIM
    y_parts = []
    for g in range(SSM_GROUPS):
        Bg = bcm[:, g * N:(g + 1) * N]
        Cg = bcm[:, (SSM_GROUPS + g) * N:(SSM_GROUPS + g + 1) * N]
        Cb = Cg.astype(BF16)
        cbm = _dot_nt(Cb, Bg.astype(BF16))
        hT = h_scr[g]
        y_off = _dot(Cb, hT.astype(BF16)) * ea_x[:, g * P2:(g + 1) * P2]
        y_dg = []
        for pp in range(P2 // LANES):
            h0 = g * (SSM_HEADS // SSM_GROUPS) + 2 * pp
            acc = None
            for e in range(2):
                h = h0 + e
                seg = a_cs[:, h:h + 1] - a_cs_t[h:h + 1, :]
                dec = jnp.exp(jnp.where(tri, seg, MASK_VALUE))
                m = (cbm * dec).astype(BF16)
                col = g * P2 + pp * LANES
                xh = jnp.where(lane_lo if e == 0 else jnp.logical_not(lane_lo), Xb[:, col:col + LANES],
                               jnp.zeros((), BF16))
                t = _dot(m, xh)
                acc = t if acc is None else acc + t
            y_dg.append(acc)
        y_parts.append(jnp.concatenate(y_dg, axis=1) + y_off)
        st = _dot(Bg.T.astype(BF16), Xe[:, g * P2:(g + 1) * P2])
        h_scr[g] = hT * cd_x[:, g * P2:(g + 1) * P2] + st
    y = jnp.concatenate(y_parts, axis=1) + xs * dexp_ref[...]
    y = y * _silu(z_ref[...].astype(F32))
    outs = []
    for g in range(SSM_GROUPS):
        yg = y[:, g * P2:(g + 1) * P2]
        outs.append(yg * lax.rsqrt(jnp.mean(yg * yg, axis=-1, keepdims=True) + NORM_EPS))
    o_ref[...] = (jnp.concatenate(outs, axis=1) * nw_ref[...]).astype(o_ref.dtype)


def _ssd(n_out, small, cw, cb, dtb, alog, dexp, nw, eh, *, batch, seq):
    L = SSM_CHUNK
    nc = seq // L
    T = batch * seq
    row = lambda b, c: b * nc + c
    full = lambda shape: pl.BlockSpec(shape, lambda b, c: (0,) * len(shape))
    return pl.pallas_call(
        _ssd_kernel, grid=(batch, nc),
        in_specs=[pl.BlockSpec((L, SSM_D_INNER), lambda b, c: (row(b, c), N_XS // SSM_D_INNER)),
                  pl.BlockSpec((L, SSM_BC_DIM), lambda b, c: (row(b, c), N_BC // SSM_BC_DIM)),
                  pl.BlockSpec((L, SSM_D_INNER), lambda b, c: (row(b, c), N_Z // SSM_D_INNER)),
                  pl.BlockSpec((L, LANES), lambda b, c: (row(b, c), 0)),
                  full(cw.shape), full(cb.shape), full(dtb.shape), full(alog.shape),
                  full(dexp.shape), full(nw.shape), full(eh.shape)],
        out_specs=pl.BlockSpec((L, SSM_D_INNER), lambda b, c: (row(b, c), 0)),
        out_shape=jax.ShapeDtypeStruct((T, SSM_D_INNER), BF16),
        scratch_shapes=[pltpu.VMEM((L + 8, SSM_D_INNER), F32), pltpu.VMEM((L + 8, SSM_BC_DIM), F32),
                        pltpu.VMEM((SSM_GROUPS, SSM_D_STATE, SSM_D_INNER // SSM_GROUPS), F32)],
        compiler_params=_params(("parallel", "arbitrary")), name="ssd")(
            n_out, n_out, n_out, small, cw, cb, dtb, alog, dexp, nw, eh)


def _swa_kernel(sink_ref, q_ref, kp_ref, kc_ref, vtp_ref, vtc_ref, o_ref):
    tq = q_ref.shape[0]
    qt = pl.program_id(1)
    rep = SWA_HEADS // SWA_KV_HEADS
    npair = rep // 2
    nk = 2 * tq
    c_i = lax.broadcasted_iota(jnp.int32, (nk, tq), 0)
    diff = tq + lax.broadcasted_iota(jnp.int32, (nk, tq), 1) - c_i
    first_key = jnp.where(qt > 0, 0, tq)
    bias = jnp.where(diff >= 0, jnp.where(diff < SWA_WINDOW, jnp.where(c_i >= first_key, 0.0, MASK_VALUE),
                                          MASK_VALUE), MASK_VALUE)
    bias = jnp.concatenate([bias] * npair, axis=1)
    kf = jnp.concatenate([kp_ref[...], kc_ref[...]], axis=0).astype(F32)
    ks = pltpu.roll(kf, SWA_HEAD_DIM, 1)
    lane_lo = lax.broadcasted_iota(jnp.int32, (nk, LANES), 1) < SWA_HEAD_DIM
    vt = jnp.concatenate([vtp_ref[0], vtc_ref[0]], axis=1).astype(F32)
    vts = pltpu.roll(vt, SWA_HEAD_DIM, 0)
    row_lo = lax.broadcasted_iota(jnp.int32, (LANES, nk), 0) < SWA_HEAD_DIM
    for g in range(SWA_KV_HEADS):
        k_own, k_swp = (kf, ks) if g == 0 else (ks, kf)
        v_own, v_swp = (vt, vts) if g == 0 else (vts, vt)
        k_e = (jnp.where(lane_lo, k_own, 0.0).astype(BF16), jnp.where(lane_lo, 0.0, k_swp).astype(BF16))
        v_e = (jnp.where(row_lo, v_own, 0.0).astype(BF16), jnp.where(row_lo, 0.0, v_swp).astype(BF16))
        q4 = jnp.concatenate([q_ref[:, (g * npair + pp) * LANES:(g * npair + pp + 1) * LANES]
                              for pp in range(npair)], axis=0)
        o2 = None
        for e in range(2):
            sink = jnp.concatenate([jnp.full((1, tq), sink_ref[g * rep + 2 * pp + e], F32)
                                    for pp in range(npair)], axis=1)
            s = _dot_nt(k_e[e], q4) + bias
            m = jnp.maximum(jnp.max(s, axis=0, keepdims=True), sink)
            ex = jnp.exp(s - m)
            inv = 1.0 / (jnp.sum(ex, axis=0, keepdims=True) + jnp.exp(sink - m))
            t = _dot(v_e[e], ex.astype(BF16)) * inv
            o2 = t if o2 is None else o2 + t
        for pp in range(npair):
            col = (g * npair + pp) * LANES
            o_ref[:, col:col + LANES] = o2[:, pp * tq:(pp + 1) * tq].T.astype(o_ref.dtype)


def _swa(sinks, c_out, vt, *, batch, seq):
    tq = SWA_WINDOW
    nq = seq // tq
    T = batch * seq
    kcol = SWA_Q_DIM // LANES
    cur = lambda b, t: b * nq + t
    prev = lambda b, t: b * nq + jnp.maximum(t - 1, 0)
    return pl.pallas_call(
        _swa_kernel, grid=(batch, nq),
        in_specs=[pl.BlockSpec(memory_space=pltpu.SMEM),
                  pl.BlockSpec((tq, SWA_Q_DIM), lambda b, t: (cur(b, t), 0)),
                  pl.BlockSpec((tq, LANES), lambda b, t: (prev(b, t), kcol)),
                  pl.BlockSpec((tq, LANES), lambda b, t: (cur(b, t), kcol)),
                  pl.BlockSpec((1, LANES, tq), lambda b, t: (b, 0, jnp.maximum(t - 1, 0))),
                  pl.BlockSpec((1, LANES, tq), lambda b, t: (b, 0, t))],
        out_specs=pl.BlockSpec((tq, SWA_Q_DIM), lambda b, t: (cur(b, t), 0)),
        out_shape=jax.ShapeDtypeStruct((T, SWA_Q_DIM), BF16),
        compiler_params=_params(("parallel", "arbitrary")), name="swa")(
            sinks, c_out, c_out, c_out, vt, vt)


def _merge_kernel(oa_ref, ob_ref, oc_ref, mg_ref, pa_ref, pb_ref, pc_ref, y_ref):
    D = D_MODEL
    y = _sigmoid(mg_ref[:, 0:D].astype(F32)) * _dot(oa_ref[...], pa_ref[...])
    y += _sigmoid(mg_ref[:, D:2 * D].astype(F32)) * _dot(ob_ref[...], pb_ref[...])
    y += _sigmoid(mg_ref[:, 2 * D:3 * D].astype(F32)) * _dot(oc_ref[...], pc_ref[...])
    y_ref[...] = y.astype(y_ref.dtype)


def _merge(oa, ob, oc, n_out, pa, pb, pc, *, tm):
    T = oa.shape[0]
    rowblk = lambda w: pl.BlockSpec((tm, w), lambda i: (i, 0))
    const = lambda a: pl.BlockSpec(a.shape, lambda i: (0,) * a.ndim, pipeline_mode=pl.Buffered(1))
    return pl.pallas_call(
        _merge_kernel, grid=(T // tm,),
        in_specs=[rowblk(NSA_Q_DIM), rowblk(SSM_D_INNER), rowblk(SWA_Q_DIM), rowblk(3 * D_MODEL),
                  const(pa), const(pb), const(pc)],
        out_specs=rowblk(D_MODEL),
        out_shape=jax.ShapeDtypeStruct((T, D_MODEL), BF16),
        compiler_params=_params(("parallel",)), name="merge")(oa, ob, oc, n_out, pa, pb, pc)


def _outproj_kernel(x_ref, y_ref, wo_ref, nw_ref, wrh_ref, wrl_ref, rb_ref, xo_ref, hn_ref, route_ref):
    x = x_ref[...] + _dot(y_ref[...], wo_ref[...])
    xo_ref[...] = x
    ms = jnp.mean(x * x, axis=-1, keepdims=True)
    hn = x * lax.rsqrt(ms + NORM_EPS) * nw_ref[...]
    hn_hi, hn_lo = _split_bf16(hn, 2)
    hn_ref[...] = hn_hi
    logit = (_dot(hn_hi, wrh_ref[...]) + _dot(hn_lo, wrh_ref[...]) + _dot(hn_hi, wrl_ref[...])
             + rb_ref[...])
    tm = logit.shape[0]
    lane = lax.broadcasted_iota(jnp.int32, (tm, LANES), 1)
    big = jnp.int32(LANES)
    gl = jnp.where(lane < MOE_GROUPS, logit, -jnp.inf)
    gmax = jnp.max(gl, axis=-1, keepdims=True)
    gidx = jnp.min(jnp.where(gl == gmax, lane, big), axis=-1, keepdims=True)
    gw = 1.0 / jnp.sum(jnp.exp(gl - gmax), axis=-1, keepdims=True)
    lo = MOE_GROUPS + MOE_EXPERTS_PER_GROUP * gidx
    el = jnp.where((lane >= lo) & (lane < lo + MOE_EXPERTS_PER_GROUP), logit, -jnp.inf)
    m1 = jnp.max(el, axis=-1, keepdims=True)
    i1 = jnp.min(jnp.where(el == m1, lane, big), axis=-1, keepdims=True)
    el2 = jnp.where(lane == i1, -jnp.inf, el)
    m2 = jnp.max(el2, axis=-1, keepdims=True)
    i2 = jnp.min(jnp.where(el2 == m2, lane, big), axis=-1, keepdims=True)
    e2 = jnp.exp(m2 - m1)
    w1 = gw / (1.0 + e2)
    w2 = gw * e2 / (1.0 + e2)
    route = jnp.where(lane == 0, (i1 - MOE_GROUPS).astype(F32),
                      jnp.where(lane == 1, (i2 - MOE_GROUPS).astype(F32),
                                jnp.where(lane == 2, w1, jnp.where(lane == 3, w2, 0.0))))
    route_ref[...] = route


def _outproj(x2d, y, wo, nw, wrh, wrl, rb, *, tm):
    T, D = x2d.shape
    rowblk = lambda w: pl.BlockSpec((tm, w), lambda i: (i, 0))
    const = lambda a: pl.BlockSpec(a.shape, lambda i: (0,) * a.ndim, pipeline_mode=pl.Buffered(1))
    return pl.pallas_call(
        _outproj_kernel, grid=(T // tm,),
        in_specs=[rowblk(D), rowblk(D), const(wo), const(nw), const(wrh), const(wrl), const(rb)],
        out_specs=(rowblk(D), rowblk(D), rowblk(LANES)),
        out_shape=(jax.ShapeDtypeStruct((T, D), F32), jax.ShapeDtypeStruct((T, D), BF16),
                   jax.ShapeDtypeStruct((T, LANES), F32)),
        compiler_params=_params(("parallel",)), name="outproj")(x2d, y, wo, nw, wrh, wrl, rb)


def _expert_kernel(te_ref, x_ref, wg_ref, wu_ref, wd_ref, y_ref, wgu_scr, wd_scr):
    i = pl.program_id(0)

    @pl.when((i == 0) | (te_ref[i] != te_ref[jnp.maximum(i - 1, 0)]))
    def _():
        wgu_scr[:, :MOE_D_FF] = wg_ref[0].astype(BF16)
        wgu_scr[:, MOE_D_FF:] = wu_ref[0].astype(BF16)
        wd_scr[...] = wd_ref[0].astype(BF16)

    gu = _dot(x_ref[...], wgu_scr[...])
    act = (_silu(gu[:, :MOE_D_FF]) * gu[:, MOE_D_FF:]).astype(BF16)
    y_ref[...] = _dot(act, wd_scr[...]).astype(y_ref.dtype)


def _experts(tile_expert, xs, wg, wu, wd):
    P, D = xs.shape
    tm = MOE_ROW_TILE
    grid_spec = pltpu.PrefetchScalarGridSpec(
        num_scalar_prefetch=1, grid=(P // tm,),
        in_specs=[pl.BlockSpec((tm, D), lambda i, te: (i, 0)),
                  pl.BlockSpec((1, D, MOE_D_FF), lambda i, te: (te[i], 0, 0)),
                  pl.BlockSpec((1, D, MOE_D_FF), lambda i, te: (te[i], 0, 0)),
                  pl.BlockSpec((1, MOE_D_FF, D), lambda i, te: (te[i], 0, 0))],
        out_specs=pl.BlockSpec((tm, D), lambda i, te: (i, 0)),
        scratch_shapes=[pltpu.VMEM((D, 2 * MOE_D_FF), BF16), pltpu.VMEM((MOE_D_FF, D), BF16)])
    return pl.pallas_call(
        _expert_kernel, grid_spec=grid_spec,
        out_shape=jax.ShapeDtypeStruct((P, D), BF16),
        compiler_params=_params(("arbitrary",)), name="experts")(tile_expert, xs, wg, wu, wd)


def _combine_kernel(x_ref, y0_ref, y1_ref, route_ref, nw_ref, *o_refs, final):
    r = route_ref[...]
    x = x_ref[...] + r[:, 2:3] * y0_ref[...].astype(F32) + r[:, 3:4] * y1_ref[...].astype(F32)
    ms = jnp.mean(x * x, axis=-1, keepdims=True)
    hn = x * lax.rsqrt(ms + NORM_EPS) * nw_ref[...]
    if final:
        o_refs[0][...] = hn
    else:
        o_refs[0][...] = x
        o_refs[1][...] = hn.astype(BF16)


def _combine(x2d, y0, y1, route, nw, *, final, tm):
    T, D = x2d.shape
    rowblk = lambda w: pl.BlockSpec((tm, w), lambda i: (i, 0))
    out_shape = [jax.ShapeDtypeStruct((T, D), F32)] + ([] if final else [jax.ShapeDtypeStruct((T, D), BF16)])
    return pl.pallas_call(
        functools.partial(_combine_kernel, final=final), grid=(T // tm,),
        in_specs=[rowblk(D), rowblk(D), rowblk(D), rowblk(LANES),
                  pl.BlockSpec((1, D), lambda i: (0, 0))],
        out_specs=tuple(rowblk(D) for _ in out_shape), out_shape=tuple(out_shape),
        compiler_params=_params(("parallel",)), name="combine")(x2d, y0, y1, route, nw)


def _rope_tables(seq):
    def tab(dim):
        inv = 1.0 / (ROPE_THETA ** (jnp.arange(0, dim, 2, dtype=F32) / dim))
        ang = jnp.arange(seq, dtype=F32)[:, None] * inv[None, :]
        return jnp.cos(ang), jnp.sin(ang)
    ca, sa = tab(NSA_HEAD_DIM)
    cc, sc = tab(SWA_HEAD_DIM)
    z = jnp.zeros_like(sc)
    tabs_a = (jnp.concatenate([ca, ca], 1), jnp.concatenate([-sa, sa], 1))
    tabs_c = (jnp.concatenate([cc] * 4, 1), jnp.concatenate([-sc, z, -sc, z], 1),
              jnp.concatenate([z, sc, z, sc], 1))
    return tabs_a, tabs_c


def _overlap_t(seq):
    nch = seq // NSA_CMP_STRIDE
    n_sel = seq // NSA_SLC_BLOCK
    cs = np.arange(nch) * NSA_CMP_STRIDE
    ce = cs + NSA_CMP_LEN - 1
    ss = np.arange(n_sel) * NSA_SLC_BLOCK
    ov = (cs[None, :] <= ss[:, None] + NSA_SLC_BLOCK - 1) & (ce[None, :] >= ss[:, None])
    ov[:, nch - 1] = False
    return jnp.asarray(ov.astype(np.float32), BF16)


def _head_expand():
    e = np.zeros((LANES, SSM_D_INNER), np.float32)
    for h in range(SSM_HEADS):
        e[h, h * SSM_HEAD_DIM:(h + 1) * SSM_HEAD_DIM] = 1.0
    return jnp.asarray(e, BF16)


def _split_w_in(w_in):
    o = np.cumsum([0, NSA_Q_DIM, 1536, 24, SSM_D_INNER, SSM_D_INNER + SSM_BC_DIM, SSM_HEADS,
                   SWA_Q_DIM, 256, 3 * D_MODEL])
    seg = lambda a, b: w_in[:, a:b]
    nsa_q = seg(o[0], o[1])
    kv = o[1]
    cmp_kv, slc_k, slc_v = seg(kv, kv + 512), seg(kv + 512, kv + 768), seg(kv + 768, kv + 1024)
    win_k, win_v = seg(kv + 1024, kv + 1280), seg(kv + 1280, kv + 1536)
    nsa_g = seg(o[2], o[3])
    ssm_z = seg(o[3], o[4])
    ssm_xs, ssm_bc = seg(o[4], o[4] + SSM_D_INNER), seg(o[4] + SSM_D_INNER, o[5])
    ssm_dt = seg(o[5], o[6])
    swa_q = seg(o[6], o[7])
    swa_k, swa_v = seg(o[7], o[7] + 128), seg(o[7] + 128, o[8])
    merge_g = seg(o[8], o[9])
    w_a = jnp.concatenate([nsa_q, slc_k, win_k], 1).astype(BF16)
    w_c = jnp.concatenate([swa_q, swa_k], 1).astype(BF16)
    w_n = jnp.concatenate([merge_g, ssm_z, ssm_xs, ssm_bc, cmp_kv, slc_v, win_v, swa_v], 1).astype(BF16)
    pad = jnp.zeros((w_in.shape[0], LANES - SSM_HEADS - 24), w_in.dtype)
    w_s = jnp.concatenate([ssm_dt, nsa_g, pad], 1).astype(BF16)
    return w_a, w_c, w_n, w_s


def _pad_lanes(v):
    return jnp.pad(v, (0, LANES - v.shape[0]))[None, :]


def _kv_transposed(n_out, col0, batch, seq):
    v = n_out[:, col0:col0 + NSA_KV_HEADS * LANES].reshape(batch, seq, NSA_KV_HEADS, LANES)
    return jnp.transpose(v, (0, 2, 3, 1))


def _rank_kernel(route_ref, rank_ref, cnt_ref, base_scr):
    tm = route_ref.shape[0]

    @pl.when(pl.program_id(0) == 0)
    def _():
        base_scr[...] = jnp.zeros(base_scr.shape, F32)

    r = route_ref[...]
    lane = lax.broadcasted_iota(jnp.int32, (tm, LANES), 1)
    lanef = lane.astype(F32)
    oh0 = jnp.where(r[:, 0:1] == lanef, 1.0, 0.0)
    oh1 = jnp.where(r[:, 1:2] == lanef, 1.0, 0.0)
    tri = (lax.broadcasted_iota(jnp.int32, (tm, tm), 0)
           >= lax.broadcasted_iota(jnp.int32, (tm, tm), 1)).astype(BF16)
    tot = base_scr[0:1, :] + _dot(tri, (oh0 + oh1).astype(BF16))
    rank0 = jnp.sum(oh0 * (tot - 1.0), axis=-1, keepdims=True)
    rank1 = jnp.sum(oh1 * (tot - 1.0), axis=-1, keepdims=True)
    rank_ref[...] = jnp.where(lane == 0, rank0, jnp.where(lane == 1, rank1, 0.0))
    base_scr[0:1, :] = tot[tm - 1:tm, :]
    cnt_ref[...] = jnp.broadcast_to(tot[tm - 1:tm, :], cnt_ref.shape)


def _rank(route, *, tm):
    T = route.shape[0]
    return pl.pallas_call(
        _rank_kernel, grid=(T // tm,),
        in_specs=[pl.BlockSpec((tm, LANES), lambda i: (i, 0))],
        out_specs=(pl.BlockSpec((tm, LANES), lambda i: (i, 0)), pl.BlockSpec((8, LANES), lambda i: (0, 0))),
        out_shape=(jax.ShapeDtypeStruct((T, LANES), F32), jax.ShapeDtypeStruct((8, LANES), F32)),
        scratch_shapes=[pltpu.VMEM((8, LANES), F32)],
        compiler_params=_params(("arbitrary",)), name="moe_rank")(route)


def _dispatch(route, n_tok):
    tm = MOE_ROW_TILE
    n_asg = n_tok * MOE_TOPK
    n_rows = n_asg + MOE_EXPERTS * tm
    rank, cnt = _rank(route, tm=512)
    counts = cnt[0, :MOE_EXPERTS].astype(jnp.int32)
    padded = ((counts + tm - 1) // tm) * tm
    pend = jnp.cumsum(padded)
    pstart = pend - padded
    start = jnp.cumsum(counts) - counts
    eid = route[:, 0:MOE_TOPK].astype(jnp.int32)
    pos = pstart[eid] + rank[:, 0:MOE_TOPK].astype(jnp.int32)
    order = jnp.argsort(eid.reshape(-1), stable=True)
    tile_start = jnp.arange(n_rows // tm, dtype=jnp.int32) * tm
    tile_expert = jnp.minimum(jnp.sum((tile_start[:, None] >= pend[None, :]).astype(jnp.int32), axis=1),
                              MOE_EXPERTS - 1)
    e_row = jnp.repeat(tile_expert, tm)
    k = jnp.arange(n_rows, dtype=jnp.int32) - pstart[e_row]
    valid = k < counts[e_row]
    src = jnp.clip(start[e_row] + jnp.where(valid, k, 0), 0, n_asg - 1)
    row_token = jnp.where(valid, order[src] // MOE_TOPK, 0)
    return row_token, pos, tile_expert


def kernel(x, norm_mix, norm_ffn, w_in, nsa_cmp_pos, nsa_cmp_w1, nsa_cmp_b1, nsa_cmp_w2, ssm_conv_w,
           ssm_conv_b, ssm_dt_bias, ssm_a_log, ssm_d, ssm_norm, swa_sinks, proj_nsa, proj_ssm, proj_swa,
           w_out, moe_group_router, moe_group_bias, moe_expert_router, moe_expert_bias, moe_w_gate,
           moe_w_up, moe_w_down, final_norm):
    B, S, D = x.shape
    T = B * S
    depth = w_in.shape[0]
    tm = 512
    tabs_a, tabs_c = _rope_tables(S)
    ovt = _overlap_t(S)
    eh = _head_expand()
    scale_a = jnp.concatenate([jnp.full((NSA_Q_DIM,), NSA_HEAD_DIM ** -0.5, F32),
                               jnp.ones((512,), F32)])[None, :]
    scale_c = jnp.concatenate([jnp.full((SWA_Q_DIM,), SWA_HEAD_DIM ** -0.5, F32),
                               jnp.ones((LANES,), F32)])[None, :]
    xc = x.reshape(T, D)
    hn_mix = _prenorm(xc, norm_mix[0][None, :], tm=tm)
    for l in range(depth):
        w_a, w_c, w_n, w_s = _split_w_in(w_in[l])
        qr, qp = _inproj(hn_mix, w_a, seq=S, tm=1024, tn=512, rope='a', scale=scale_a, tabs=tabs_a,
                         out_dtypes=(BF16, BF16))
        (c_out,) = _inproj(hn_mix, w_c, seq=S, tm=tm, tn=w_c.shape[1], rope='c', scale=scale_c,
                           tabs=tabs_c)
        (n_out,) = _inproj(hn_mix, w_n, seq=S, tm=1024, tn=896)
        (small,) = _inproj(hn_mix, w_s, seq=S, tm=1024, tn=LANES, out_dtypes=(F32,))

        w1r = nsa_cmp_w1[l].reshape(2, NSA_CMP_LEN, NSA_HEAD_DIM, NSA_CMP_HIDDEN).astype(BF16)
        kc, vct = _cmp_mlp(n_out, nsa_cmp_pos[l], w1r, nsa_cmp_b1[l][:, None, :],
                           nsa_cmp_w2[l, 0].astype(BF16), nsa_cmp_w2[l, 1].T.astype(BF16), batch=B, seq=S)
        o_a = _nsa_attn(qp, qr, small, kc, vct, ovt, _kv_transposed(n_out, N_SLCV, B, S),
                        _kv_transposed(n_out, N_WINV, B, S), batch=B, seq=S)

        dexp = jnp.repeat(ssm_d[l], SSM_HEAD_DIM)[None, :]
        o_b = _ssd(n_out, small, ssm_conv_w[l], ssm_conv_b[l][None, :], _pad_lanes(ssm_dt_bias[l]),
                   _pad_lanes(ssm_a_log[l]), dexp, ssm_norm[l][None, :], eh, batch=B, seq=S)
        swa_vt = jnp.transpose(n_out[:, N_SWAV:N_SWAV + LANES].reshape(B, S, LANES), (0, 2, 1))
        o_c = _swa(swa_sinks[l], c_out, swa_vt, batch=B, seq=S)

        y = _merge(o_a, o_b, o_c, n_out, proj_nsa[l].astype(BF16), proj_ssm[l].astype(BF16),
                   proj_swa[l].astype(BF16), tm=tm)
        wr = jnp.pad(jnp.concatenate([moe_group_router[l], moe_expert_router[l]], 1),
                     ((0, 0), (0, LANES - MOE_GROUPS - MOE_EXPERTS)))
        wrh = wr.astype(BF16)
        wrl = (wr - wrh.astype(F32)).astype(BF16)
        rb = _pad_lanes(jnp.concatenate([moe_group_bias[l], moe_expert_bias[l]]))
        x_mid, hn, route = _outproj(xc, y, w_out[l].astype(BF16), norm_ffn[l][None, :], wrh, wrl, rb,
                                    tm=tm)

        row_token, pos, tile_expert = _dispatch(route, T)
        ys = _experts(tile_expert, hn[row_token], moe_w_gate[l].reshape(MOE_EXPERTS, D, MOE_D_FF),
                      moe_w_up[l].reshape(MOE_EXPERTS, D, MOE_D_FF),
                      moe_w_down[l].reshape(MOE_EXPERTS, MOE_D_FF, D))
        final = l == depth - 1
        nw_next = final_norm if final else norm_mix[l + 1]
        outs = _combine(x_mid, ys[pos[:, 0]], ys[pos[:, 1]], route, nw_next[None, :], final=final, tm=tm)
        xc = outs[0]
        if not final:
            hn_mix = outs[1]
    return xc.reshape(B, S, D)
```

```python
import functools

import jax
import jax.numpy as jnp
import numpy as np
from jax import lax
from jax.experimental import pallas as pl
from jax.experimental.pallas import tpu as pltpu

F32 = jnp.float32
BF16 = jnp.bfloat16

D_MODEL = 2048
ROPE_THETA = 10000.0
NORM_EPS = 1e-6
MASK_VALUE = -1e30
FORCE_SCORE = 1e6

NSA_HEADS = 8
NSA_KV_HEADS = 2
NSA_REP = NSA_HEADS // NSA_KV_HEADS
NSA_HEAD_DIM = 128
NSA_CMP_STRIDE = 16
NSA_CMP_LEN = 32
NSA_CMP_HIDDEN = 256
NSA_SLC_BLOCK = 64
NSA_SLC_TOPK = 16
NSA_WINDOW = 512
NSA_Q_DIM = NSA_HEADS * NSA_HEAD_DIM

SSM_D_INNER = 1024
SSM_HEAD_DIM = 64
SSM_HEADS = 16
SSM_GROUPS = 2
SSM_D_STATE = 128
SSM_CONV = 4
SSM_CHUNK = 128
SSM_BC_DIM = 2 * SSM_GROUPS * SSM_D_STATE

SWA_HEADS = 16
SWA_KV_HEADS = 2
SWA_HEAD_DIM = 64
SWA_WINDOW = 128
SWA_Q_DIM = SWA_HEADS * SWA_HEAD_DIM

MOE_GROUPS = 4
MOE_EXPERTS_PER_GROUP = 8
MOE_EXPERTS = MOE_GROUPS * MOE_EXPERTS_PER_GROUP
MOE_TOPK = 2
MOE_D_FF = 512

LANES = 128
ATT_Q_TILE = 128
SLC_K_TILE = 512
MOE_ROW_TILE = 256
VMEM_LIMIT = 56 * 1024 * 1024

N_MERGE = 0
N_Z = 6144
N_XS = 7168
N_BC = 8192
N_CMP = 8704
N_SLCV = 9216
N_WINV = 9472
N_SWAV = 9728
N_TOTAL = 9856
SMALL_GATE0 = 16


def _params(sem):
    return pltpu.CompilerParams(dimension_semantics=sem, vmem_limit_bytes=VMEM_LIMIT)


def _dot(a, b):
    return jnp.dot(a, b, preferred_element_type=F32)


def _dot_nt(a, b):
    return lax.dot_general(a, b, (((1,), (1,)), ((), ())), preferred_element_type=F32)


def _split_bf16(v, n):
    parts = []
    for _ in range(n):
        p = v.astype(BF16)
        parts.append(p)
        v = v - p.astype(F32)
    return parts


def _expand(v, e):
    hi, lo = _split_bf16(v, 2)
    return _dot(hi, e) + _dot(lo, e)


def _sigmoid(v):
    return 1.0 / (1.0 + jnp.exp(-v))


def _silu(v):
    return v * _sigmoid(v)


def _prenorm_kernel(x_ref, nw_ref, hn_ref):
    x = x_ref[...]
    ms = jnp.mean(x * x, axis=-1, keepdims=True)
    hn_ref[...] = (x * lax.rsqrt(ms + NORM_EPS) * nw_ref[...]).astype(hn_ref.dtype)


def _prenorm(x2d, nw, *, tm):
    T, D = x2d.shape
    return pl.pallas_call(
        _prenorm_kernel, grid=(T // tm,),
        in_specs=[pl.BlockSpec((tm, D), lambda i: (i, 0)), pl.BlockSpec((1, D), lambda i: (0, 0))],
        out_specs=pl.BlockSpec((tm, D), lambda i: (i, 0)),
        out_shape=jax.ShapeDtypeStruct((T, D), BF16),
        compiler_params=_params(("parallel",)), name="prenorm")(x2d, nw)


def _inproj_kernel(*refs, rope, has_scale, n_out):
    it = iter(refs)
    hn_ref, w_ref = next(it), next(it)
    cs_ref = next(it) if has_scale else None
    tabs = [next(it) for _ in range({None: 0, 'a': 2, 'c': 3}[rope])]
    outs = [next(it) for _ in range(n_out)]

    acc = _dot_nt(hn_ref[...], w_ref[...])
    if has_scale:
        acc = acc * cs_ref[...]
    if rope is None:
        outs[0][...] = acc.astype(outs[0].dtype)
        return
    if n_out == 2:
        outs[1][...] = acc.astype(outs[1].dtype)
    for c in range(acc.shape[1] // LANES):
        a = acc[:, c * LANES:(c + 1) * LANES]
        if rope == 'a':
            r = a * tabs[0][...] + pltpu.roll(a, 64, 1) * tabs[1][...]
        else:
            r = (a * tabs[0][...] + pltpu.roll(a, 96, 1) * tabs[1][...]
                 + pltpu.roll(a, 32, 1) * tabs[2][...])
        outs[0][:, c * LANES:(c + 1) * LANES] = r.astype(outs[0].dtype)


def _inproj(hn, w, *, seq, tm, tn, rope=None, scale=None, tabs=(), out_dtypes=(BF16,)):
    T, D = hn.shape
    N = w.shape[0]
    nrow = seq // tm
    in_specs = [pl.BlockSpec((tm, D), lambda i, j: (i, 0)),
                pl.BlockSpec((tn, D), lambda i, j: (j, 0))]
    args = [hn, w]
    if scale is not None:
        in_specs.append(pl.BlockSpec((1, tn), lambda i, j: (0, j)))
        args.append(scale)
    for t in tabs:
        in_specs.append(pl.BlockSpec((tm, LANES), lambda i, j: (i % nrow, 0)))
        args.append(t)
    out_shape = tuple(jax.ShapeDtypeStruct((T, N), dt) for dt in out_dtypes)
    out_specs = tuple(pl.BlockSpec((tm, tn), lambda i, j: (i, j)) for _ in out_dtypes)
    kern = functools.partial(_inproj_kernel, rope=rope, has_scale=scale is not None,
                             n_out=len(out_dtypes))
    return pl.pallas_call(
        kern, grid=(T // tm, N // tn), in_specs=in_specs, out_specs=out_specs, out_shape=out_shape,
        compiler_params=_params(("parallel", "arbitrary")), name="inproj_" + str(rope))(*args)


def _cmp_mlp_kernel(k_ref, v_ref, pos_ref, w1_ref, b1_ref, w2k_ref, w2vt_ref, kc_ref, vct_ref, f32_scr):
    nch = kc_ref.shape[2]
    for c, src in enumerate((k_ref, v_ref)):
        f32_scr[...] = src[...].astype(F32)
        first = jnp.zeros((nch, NSA_CMP_HIDDEN), F32)
        second = jnp.zeros((nch, NSA_CMP_HIDDEN), F32)
        for t in range(NSA_CMP_STRIDE):
            xt = f32_scr[pl.ds(t, nch, stride=NSA_CMP_STRIDE), :]
            first += _dot((xt + pos_ref[c, t:t + 1, :]).astype(BF16), w1_ref[c, t])
            t2 = NSA_CMP_STRIDE + t
            second += _dot((xt + pos_ref[c, t2:t2 + 1, :]).astype(BF16), w1_ref[c, t2])
        hid = _silu(first + pltpu.roll(second, nch - 1, 0) + b1_ref[c]).astype(BF16)
        if c == 0:
            kc_ref[0, 0] = _dot(hid, w2k_ref[...]).astype(kc_ref.dtype)
        else:
            vct_ref[0, 0] = _dot_nt(w2vt_ref[...], hid).astype(vct_ref.dtype)


def _cmp_mlp(n_out, pos, w1r, b1, w2k, w2vt, *, batch, seq):
    nch = seq // NSA_CMP_STRIDE
    cb0 = N_CMP // LANES
    full = lambda a: pl.BlockSpec(a.shape, lambda b, g: (0,) * a.ndim)
    return pl.pallas_call(
        _cmp_mlp_kernel, grid=(batch, NSA_KV_HEADS),
        in_specs=[pl.BlockSpec((seq, LANES), lambda b, g: (b, cb0 + g)),
                  pl.BlockSpec((seq, LANES), lambda b, g: (b, cb0 + NSA_KV_HEADS + g)),
                  full(pos), full(w1r), full(b1), full(w2k), full(w2vt)],
        out_specs=(pl.BlockSpec((1, 1, nch, LANES), lambda b, g: (b, g, 0, 0)),
                   pl.BlockSpec((1, 1, LANES, nch), lambda b, g: (b, g, 0, 0))),
        out_shape=(jax.ShapeDtypeStruct((batch, NSA_KV_HEADS, nch, LANES), BF16),
                   jax.ShapeDtypeStruct((batch, NSA_KV_HEADS, LANES, nch), BF16)),
        scratch_shapes=[pltpu.VMEM((seq, LANES), F32)],
        compiler_params=_params(("parallel", "arbitrary")), name="nsa_cmp_mlp")(
            n_out, n_out, pos, w1r, b1, w2k, w2vt)


def _nsa_attn_kernel(qp_ref, qr_ref, small_ref, kc_ref, vct_ref, ovt_ref, slck_ref, slcvt_ref, wink_ref,
                     winvt_ref, o_ref, q4_scr, sel_scr, m_scr, l_scr, acc_scr, part_scr, *, topk):
    tq = qp_ref.shape[0]
    nch = kc_ref.shape[2]
    n_sel = ovt_ref.shape[0]
    rep = NSA_REP
    tk = SLC_K_TILE
    wlen = NSA_WINDOW + tq
    qt = pl.program_id(1)
    start = qt * tq

    def tile4(v):
        return jnp.concatenate([v] * rep, axis=1)

    def qpos(rows):
        return start + lax.broadcasted_iota(jnp.int32, (rows, tq), 1)

    def sub(rows):
        return lax.broadcasted_iota(jnp.int32, (rows, tq), 0)

    for g in range(NSA_KV_HEADS):
        for r in range(rep):
            h = g * rep + r
            q4_scr[0, g, r * tq:(r + 1) * tq, :] = qp_ref[:, h * LANES:(h + 1) * LANES]
            q4_scr[1, g, r * tq:(r + 1) * tq, :] = qr_ref[:, h * LANES:(h + 1) * LANES]

    gates = _sigmoid(small_ref[...]).T

    def gate4(g, br):
        return jnp.concatenate([gates[SMALL_GATE0 + 3 * (g * rep + r) + br:SMALL_GATE0 + 3 * (g * rep + r) + br + 1, :]
                                for r in range(rep)], axis=1)

    vis = sub(nch) * NSA_CMP_STRIDE + (NSA_CMP_LEN - 1) <= qpos(nch)
    vis_bias = tile4(jnp.where(vis, 0.0, MASK_VALUE))
    vis_one = tile4(jnp.where(vis, 1.0, 0.0))
    jj = sub(n_sel)
    pos_t = qpos(n_sel)
    qblk = pos_t // NSA_SLC_BLOCK
    causal_blk = jj * NSA_SLC_BLOCK <= pos_t
    forced = (jj == 0) | (jj == qblk) | (jj == qblk - 1)
    for g in range(NSA_KV_HEADS):
        s = _dot_nt(kc_ref[0, g], q4_scr[0, g]) + vis_bias
        e = jnp.exp(s - jnp.max(s, axis=0, keepdims=True))
        p = e * (1.0 / jnp.sum(e, axis=0, keepdims=True)) * vis_one
        part_scr[g] = gate4(g, 0) * _dot(vct_ref[0, g], p.astype(BF16))
        psum = p[:, 0:tq]
        for r in range(1, rep):
            psum = psum + p[:, r * tq:(r + 1) * tq]
        imp = sum(_dot(ovt_ref[...], part) for part in _split_bf16(psum, 3))
        imp = jnp.where(causal_blk, imp, MASK_VALUE)
        imp = jnp.where(forced, FORCE_SCORE, imp)
        rank = jnp.zeros((n_sel, tq), F32)
        for jp in range(n_sel):
            row = imp[jp:jp + 1, :]
            rank += jnp.where(jj > jp, jnp.where(row >= imp, 1.0, 0.0), jnp.where(row > imp, 1.0, 0.0))
        sel_scr[g * n_sel:(g + 1) * n_sel, :] = jnp.where(rank < topk, 1.0, 0.0)

    wstart = pl.multiple_of(jnp.maximum(qt - NSA_WINDOW // tq, 0) * tq, tq)
    diff = qpos(wlen) - (wstart + sub(wlen))
    wbias = tile4(jnp.where(diff >= 0, jnp.where(diff < NSA_WINDOW, 0.0, MASK_VALUE), MASK_VALUE))
    for g in range(NSA_KV_HEADS):
        s = _dot_nt(wink_ref[pl.ds(wstart, wlen), g * LANES:(g + 1) * LANES], q4_scr[1, g]) + wbias
        e = jnp.exp(s - jnp.max(s, axis=0, keepdims=True))
        o = _dot(winvt_ref[0, g * LANES:(g + 1) * LANES, pl.ds(wstart, wlen)], e.astype(BF16))
        part_scr[g] = part_scr[g] + gate4(g, 2) * (o * (1.0 / jnp.sum(e, axis=0, keepdims=True)))

    m_scr[...] = jnp.full(m_scr.shape, MASK_VALUE, F32)
    l_scr[...] = jnp.zeros(l_scr.shape, F32)
    acc_scr[...] = jnp.zeros(acc_scr.shape, F32)
    blocks_per_tile = tk // NSA_SLC_BLOCK

    def body(kt, carry):
        base = pl.multiple_of(kt * tk, tk)
        causal_bias = jnp.where(base + sub(tk) <= qpos(tk), 0.0, MASK_VALUE)
        for g in range(NSA_KV_HEADS):
            s = _dot_nt(slck_ref[pl.ds(base, tk), g * LANES:(g + 1) * LANES], q4_scr[1, g])
            picked = jnp.concatenate(
                [jnp.broadcast_to(sel_scr[pl.ds(g * n_sel + kt * blocks_per_tile + i, 1), :],
                                  (NSA_SLC_BLOCK, tq)) for i in range(blocks_per_tile)], axis=0)
            s = s + tile4(jnp.where(picked > 0.5, causal_bias, MASK_VALUE))
            m_prev = m_scr[g]
            m_next = jnp.maximum(m_prev, jnp.max(s, axis=0, keepdims=True))
            alpha = jnp.exp(m_prev - m_next)
            p = jnp.exp(s - m_next)
            l_scr[g] = alpha * l_scr[g] + jnp.sum(p, axis=0, keepdims=True)
            acc_scr[g] = acc_scr[g] * alpha + _dot(slcvt_ref[0, g * LANES:(g + 1) * LANES, pl.ds(base, tk)],
                                                   p.astype(BF16))
            m_scr[g] = m_next
        return carry

    lax.fori_loop(0, (start + tq + tk - 1) // tk, body, 0)

    for g in range(NSA_KV_HEADS):
        o = part_scr[g] + gate4(g, 1) * (acc_scr[g] * (1.0 / l_scr[g]))
        for r in range(rep):
            h = g * rep + r
            o_ref[:, h * LANES:(h + 1) * LANES] = o[:, r * tq:(r + 1) * tq].T.astype(o_ref.dtype)


def _nsa_attn(qp, qr, small, kc, vct, ovt, vt, *, batch, seq):
    tq = ATT_Q_TILE
    assert seq >= NSA_WINDOW + tq and seq % SLC_K_TILE == 0
    nq = seq // tq
    nch = seq // NSA_CMP_STRIDE
    n_sel = seq // NSA_SLC_BLOCK
    T = batch * seq
    row = lambda b, t: (b * nq + t, 0)
    per_b4 = lambda b, t: (b, 0, 0, 0)
    kcol = NSA_Q_DIM // 256
    rows = tq * NSA_REP
    kern = functools.partial(_nsa_attn_kernel, topk=min(NSA_SLC_TOPK, n_sel))
    return pl.pallas_call(
        kern, grid=(batch, nq),
        in_specs=[pl.BlockSpec((tq, NSA_Q_DIM), row), pl.BlockSpec((tq, NSA_Q_DIM), row),
                  pl.BlockSpec((tq, LANES), row),
                  pl.BlockSpec((1, NSA_KV_HEADS, nch, LANES), per_b4),
                  pl.BlockSpec((1, NSA_KV_HEADS, LANES, nch), per_b4),
                  pl.BlockSpec((n_sel, nch), lambda b, t: (0, 0)),
                  pl.BlockSpec((seq, 256), lambda b, t: (b, kcol)),
                  pl.BlockSpec((1, NSA_KV_HEADS * LANES, seq), lambda b, t: (b, 0, 0)),
                  pl.BlockSpec((seq, 256), lambda b, t: (b, kcol + 1)),
                  pl.BlockSpec((1, NSA_KV_HEADS * LANES, seq), lambda b, t: (b, 1, 0))],
        out_specs=pl.BlockSpec((tq, NSA_Q_DIM), row),
        out_shape=jax.ShapeDtypeStruct((T, NSA_Q_DIM), BF16),
        scratch_shapes=[pltpu.VMEM((2, NSA_KV_HEADS, rows, LANES), BF16),
                        pltpu.VMEM((NSA_KV_HEADS * n_sel, tq), F32),
                        pltpu.VMEM((NSA_KV_HEADS, 1, rows), F32), pltpu.VMEM((NSA_KV_HEADS, 1, rows), F32),
                        pltpu.VMEM((NSA_KV_HEADS, LANES, rows), F32),
                        pltpu.VMEM((NSA_KV_HEADS, LANES, rows), F32)],
        compiler_params=_params(("parallel", "arbitrary")), name="nsa_attn")(
            qp, qr, small, kc, vct, ovt, qr, vt, qr, vt)


def _ssd_kernel(xs_ref, bc_ref, z_ref, small_ref, cw_ref, cb_ref, dtb_ref, alog_ref, dexp_ref,
                nw_ref, eh_ref, o_ref, xs_scr, bc_scr, h_scr):
    L = SSM_CHUNK
    P2 = SSM_D_INNER // SSM_GROUPS
    N = SSM_D_STATE
    c = pl.program_id(1)

    @pl.when(c == 0)
    def _():
        xs_scr[0:8, :] = jnp.zeros((8, SSM_D_INNER), F32)
        bc_scr[0:8, :] = jnp.zeros((8, SSM_BC_DIM), F32)
        h_scr[...] = jnp.zeros(h_scr.shape, F32)

    xs_scr[8:8 + L, :] = xs_ref[...].astype(F32)
    bc_scr[8:8 + L, :] = bc_ref[...].astype(F32)

    def conv(scr, col0, width):
        acc = jnp.zeros((L, width), F32) + cb_ref[:, col0:col0 + width]
        for k in range(SSM_CONV):
            acc += scr[8 - (SSM_CONV - 1) + k:8 - (SSM_CONV - 1) + k + L, :] * cw_ref[k:k + 1, col0:col0 + width]
        return _silu(acc)

    xs = conv(xs_scr, 0, SSM_D_INNER)
    bcm = conv(bc_scr, SSM_D_INNER, SSM_BC_DIM)
    xs_scr[0:8, :] = xs_scr[L:L + 8, :]
    bc_scr[0:8, :] = bc_scr[L:L + 8, :]

    lane = lax.broadcasted_iota(jnp.int32, (L, LANES), 1)
    pre = small_ref[...] + dtb_ref[...]
    dt = jnp.maximum(pre, 0.0) + jnp.log(1.0 + jnp.exp(-jnp.abs(pre)))
    dt = jnp.where(lane < SSM_HEADS, dt, 0.0)
    a = dt * (-jnp.exp(alog_ref[...]))
    tri = (lax.broadcasted_iota(jnp.int32, (L, L), 0)
           >= lax.broadcasted_iota(jnp.int32, (L, L), 1))
    tri_b = tri.astype(BF16)
    a_cs = sum(_dot(tri_b, part) for part in _split_bf16(a, 3))
    a_cs_t = a_cs.T
    a_end = a_cs[L - 1:L, :]
    eh = eh_ref[...]
    dt_x = _expand(dt, eh)
    ea_x = _expand(jnp.exp(a_cs), eh)
    de_x = _expand(jnp.exp(a_end - a_cs), eh)
    cd_x = _expand(jnp.broadcast_to(jnp.exp(a_end), (8, LANES)), eh)[0:1]

    X = xs * dt_x
    Xb = X.astype(BF16)
    Xe = (X * de_x).astype(BF16)
    lane_lo = lax.broadcasted_iota(jnp.int32, (L, LANES), 1) < SSM_HEAD_DIM
    y_parts = []
    for g in range(SSM_GROUPS):
        Bg = bcm[:, g * N:(g + 1) * N]
        Cg = bcm[:, (SSM_GROUPS + g) * N:(SSM_GROUPS + g + 1) * N]
        Cb = Cg.astype(BF16)
        cbm = _dot_nt(Cb, Bg.astype(BF16))
        hT = h_scr[g]
        y_off = _dot(Cb, hT.astype(BF16)) * ea_x[:, g * P2:(g + 1) * P2]
        y_dg = []
        for pp in range(P2 // LANES):
            h0 = g * (SSM_HEADS // SSM_GROUPS) + 2 * pp
            acc = None
            for e in range(2):
                h = h0 + e
                seg = a_cs[:, h:h + 1] - a_cs_t[h:h + 1, :]
                dec = jnp.exp(jnp.where(tri, seg, MASK_VALUE))
                m = (cbm * dec).astype(BF16)
                col = g * P2 + pp * LANES
                xh = jnp.where(lane_lo if e == 0 else jnp.logical_not(lane_lo), Xb[:, col:col + LANES],
                               jnp.zeros((), BF16))
                t = _dot(m, xh)
                acc = t if acc is None else acc + t
            y_dg.append(acc)
        y_parts.append(jnp.concatenate(y_dg, axis=1) + y_off)
        st = _dot(Bg.T.astype(BF16), Xe[:, g * P2:(g + 1) * P2])
        h_scr[g] = hT * cd_x[:, g * P2:(g + 1) * P2] + st
    y = jnp.concatenate(y_parts, axis=1) + xs * dexp_ref[...]
    y = y * _silu(z_ref[...].astype(F32))
    outs = []
    for g in range(SSM_GROUPS):
        yg = y[:, g * P2:(g + 1) * P2]
        outs.append(yg * lax.rsqrt(jnp.mean(yg * yg, axis=-1, keepdims=True) + NORM_EPS))
    o_ref[...] = (jnp.concatenate(outs, axis=1) * nw_ref[...]).astype(o_ref.dtype)


def _ssd(n_out, small, cw, cb, dtb, alog, dexp, nw, eh, *, batch, seq):
    L = SSM_CHUNK
    nc = seq // L
    T = batch * seq
    row = lambda b, c: b * nc + c
    full = lambda shape: pl.BlockSpec(shape, lambda b, c: (0,) * len(shape))
    return pl.pallas_call(
        _ssd_kernel, grid=(batch, nc),
        in_specs=[pl.BlockSpec((L, SSM_D_INNER), lambda b, c: (row(b, c), N_XS // SSM_D_INNER)),
                  pl.BlockSpec((L, SSM_BC_DIM), lambda b, c: (row(b, c), N_BC // SSM_BC_DIM)),
                  pl.BlockSpec((L, SSM_D_INNER), lambda b, c: (row(b, c), N_Z // SSM_D_INNER)),
                  pl.BlockSpec((L, LANES), lambda b, c: (row(b, c), 0)),
                  full(cw.shape), full(cb.shape), full(dtb.shape), full(alog.shape),
                  full(dexp.shape), full(nw.shape), full(eh.shape)],
        out_specs=pl.BlockSpec((L, SSM_D_INNER), lambda b, c: (row(b, c), 0)),
        out_shape=jax.ShapeDtypeStruct((T, SSM_D_INNER), BF16),
        scratch_shapes=[pltpu.VMEM((L + 8, SSM_D_INNER), F32), pltpu.VMEM((L + 8, SSM_BC_DIM), F32),
                        pltpu.VMEM((SSM_GROUPS, SSM_D_STATE, SSM_D_INNER // SSM_GROUPS), F32)],
        compiler_params=_params(("parallel", "arbitrary")), name="ssd")(
            n_out, n_out, n_out, small, cw, cb, dtb, alog, dexp, nw, eh)


def _swa_kernel(sink_ref, q_ref, kp_ref, kc_ref, vtp_ref, vtc_ref, o_ref):
    tq = q_ref.shape[0]
    qt = pl.program_id(1)
    rep = SWA_HEADS // SWA_KV_HEADS
    npair = rep // 2
    nk = 2 * tq
    c_i = lax.broadcasted_iota(jnp.int32, (nk, tq), 0)
    diff = tq + lax.broadcasted_iota(jnp.int32, (nk, tq), 1) - c_i
    first_key = jnp.where(qt > 0, 0, tq)
    bias = jnp.where(diff >= 0, jnp.where(diff < SWA_WINDOW, jnp.where(c_i >= first_key, 0.0, MASK_VALUE),
                                          MASK_VALUE), MASK_VALUE)
    bias = jnp.concatenate([bias] * npair, axis=1)
    kf = jnp.concatenate([kp_ref[...], kc_ref[...]], axis=0).astype(F32)
    ks = pltpu.roll(kf, SWA_HEAD_DIM, 1)
    lane_lo = lax.broadcasted_iota(jnp.int32, (nk, LANES), 1) < SWA_HEAD_DIM
    vt = jnp.concatenate([vtp_ref[0], vtc_ref[0]], axis=1).astype(F32)
    vts = pltpu.roll(vt, SWA_HEAD_DIM, 0)
    row_lo = lax.broadcasted_iota(jnp.int32, (LANES, nk), 0) < SWA_HEAD_DIM
    for g in range(SWA_KV_HEADS):
        k_own, k_swp = (kf, ks) if g == 0 else (ks, kf)
        v_own, v_swp = (vt, vts) if g == 0 else (vts, vt)
        k_e = (jnp.where(lane_lo, k_own, 0.0).astype(BF16), jnp.where(lane_lo, 0.0, k_swp).astype(BF16))
        v_e = (jnp.where(row_lo, v_own, 0.0).astype(BF16), jnp.where(row_lo, 0.0, v_swp).astype(BF16))
        q4 = jnp.concatenate([q_ref[:, (g * npair + pp) * LANES:(g * npair + pp + 1) * LANES]
                              for pp in range(npair)], axis=0)
        o2 = None
        for e in range(2):
            sink = jnp.concatenate([jnp.full((1, tq), sink_ref[g * rep + 2 * pp + e], F32)
                                    for pp in range(npair)], axis=1)
            s = _dot_nt(k_e[e], q4) + bias
            m = jnp.maximum(jnp.max(s, axis=0, keepdims=True), sink)
            ex = jnp.exp(s - m)
            inv = 1.0 / (jnp.sum(ex, axis=0, keepdims=True) + jnp.exp(sink - m))
            t = _dot(v_e[e], ex.astype(BF16)) * inv
            o2 = t if o2 is None else o2 + t
        for pp in range(npair):
            col = (g * npair + pp) * LANES
            o_ref[:, col:col + LANES] = o2[:, pp * tq:(pp + 1) * tq].T.astype(o_ref.dtype)


def _swa(sinks, c_out, vt, *, batch, seq):
    tq = SWA_WINDOW
    nq = seq // tq
    T = batch * seq
    kcol = SWA_Q_DIM // LANES
    vrow = VT_ROWS // LANES - 1
    cur = lambda b, t: b * nq + t
    prev = lambda b, t: b * nq + jnp.maximum(t - 1, 0)
    return pl.pallas_call(
        _swa_kernel, grid=(batch, nq),
        in_specs=[pl.BlockSpec(memory_space=pltpu.SMEM),
                  pl.BlockSpec((tq, SWA_Q_DIM), lambda b, t: (cur(b, t), 0)),
                  pl.BlockSpec((tq, LANES), lambda b, t: (prev(b, t), kcol)),
                  pl.BlockSpec((tq, LANES), lambda b, t: (cur(b, t), kcol)),
                  pl.BlockSpec((1, LANES, tq), lambda b, t: (b, vrow, jnp.maximum(t - 1, 0))),
                  pl.BlockSpec((1, LANES, tq), lambda b, t: (b, vrow, t))],
        out_specs=pl.BlockSpec((tq, SWA_Q_DIM), lambda b, t: (cur(b, t), 0)),
        out_shape=jax.ShapeDtypeStruct((T, SWA_Q_DIM), BF16),
        compiler_params=_params(("parallel", "arbitrary")), name="swa")(
            sinks, c_out, c_out, c_out, vt, vt)


def _merge_kernel(oa_ref, ob_ref, oc_ref, mg_ref, pa_ref, pb_ref, pc_ref, y_ref):
    D = D_MODEL
    y = _sigmoid(mg_ref[:, 0:D].astype(F32)) * _dot(oa_ref[...], pa_ref[...])
    y += _sigmoid(mg_ref[:, D:2 * D].astype(F32)) * _dot(ob_ref[...], pb_ref[...])
    y += _sigmoid(mg_ref[:, 2 * D:3 * D].astype(F32)) * _dot(oc_ref[...], pc_ref[...])
    y_ref[...] = y.astype(y_ref.dtype)


def _merge(oa, ob, oc, n_out, pa, pb, pc, *, tm):
    T = oa.shape[0]
    rowblk = lambda w: pl.BlockSpec((tm, w), lambda i: (i, 0))
    const = lambda a: pl.BlockSpec(a.shape, lambda i: (0,) * a.ndim, pipeline_mode=pl.Buffered(1))
    return pl.pallas_call(
        _merge_kernel, grid=(T // tm,),
        in_specs=[rowblk(NSA_Q_DIM), rowblk(SSM_D_INNER), rowblk(SWA_Q_DIM), rowblk(3 * D_MODEL),
                  const(pa), const(pb), const(pc)],
        out_specs=rowblk(D_MODEL),
        out_shape=jax.ShapeDtypeStruct((T, D_MODEL), BF16),
        compiler_params=_params(("parallel",)), name="merge")(oa, ob, oc, n_out, pa, pb, pc)


def _outproj_kernel(x_ref, y_ref, wo_ref, nw_ref, wrh_ref, wrl_ref, rb_ref, xo_ref, hn_ref, route_ref):
    x = x_ref[...] + _dot(y_ref[...], wo_ref[...])
    xo_ref[...] = x
    ms = jnp.mean(x * x, axis=-1, keepdims=True)
    hn = x * lax.rsqrt(ms + NORM_EPS) * nw_ref[...]
    hn_hi, hn_lo = _split_bf16(hn, 2)
    hn_ref[...] = hn_hi
    logit = (_dot(hn_hi, wrh_ref[...]) + _dot(hn_lo, wrh_ref[...]) + _dot(hn_hi, wrl_ref[...])
             + rb_ref[...])
    tm = logit.shape[0]
    lane = lax.broadcasted_iota(jnp.int32, (tm, LANES), 1)
    big = jnp.int32(LANES)
    gl = jnp.where(lane < MOE_GROUPS, logit, -jnp.inf)
    gmax = jnp.max(gl, axis=-1, keepdims=True)
    gidx = jnp.min(jnp.where(gl == gmax, lane, big), axis=-1, keepdims=True)
    gw = 1.0 / jnp.sum(jnp.exp(gl - gmax), axis=-1, keepdims=True)
    lo = MOE_GROUPS + MOE_EXPERTS_PER_GROUP * gidx
    el = jnp.where((lane >= lo) & (lane < lo + MOE_EXPERTS_PER_GROUP), logit, -jnp.inf)
    m1 = jnp.max(el, axis=-1, keepdims=True)
    i1 = jnp.min(jnp.where(el == m1, lane, big), axis=-1, keepdims=True)
    el2 = jnp.where(lane == i1, -jnp.inf, el)
    m2 = jnp.max(el2, axis=-1, keepdims=True)
    i2 = jnp.min(jnp.where(el2 == m2, lane, big), axis=-1, keepdims=True)
    e2 = jnp.exp(m2 - m1)
    w1 = gw / (1.0 + e2)
    w2 = gw * e2 / (1.0 + e2)
    route = jnp.where(lane == 0, (i1 - MOE_GROUPS).astype(F32),
                      jnp.where(lane == 1, (i2 - MOE_GROUPS).astype(F32),
                                jnp.where(lane == 2, w1, jnp.where(lane == 3, w2, 0.0))))
    route_ref[...] = route


def _outproj(x2d, y, wo, nw, wrh, wrl, rb, *, tm):
    T, D = x2d.shape
    rowblk = lambda w: pl.BlockSpec((tm, w), lambda i: (i, 0))
    const = lambda a: pl.BlockSpec(a.shape, lambda i: (0,) * a.ndim, pipeline_mode=pl.Buffered(1))
    return pl.pallas_call(
        _outproj_kernel, grid=(T // tm,),
        in_specs=[rowblk(D), rowblk(D), const(wo), const(nw), const(wrh), const(wrl), const(rb)],
        out_specs=(rowblk(D), rowblk(D), rowblk(LANES)),
        out_shape=(jax.ShapeDtypeStruct((T, D), F32), jax.ShapeDtypeStruct((T, D), BF16),
                   jax.ShapeDtypeStruct((T, LANES), F32)),
        compiler_params=_params(("parallel",)), name="outproj")(x2d, y, wo, nw, wrh, wrl, rb)


def _expert_kernel(te_ref, x_ref, wg_ref, wu_ref, wd_ref, y_ref, wgu_scr, wd_scr):
    i = pl.program_id(0)

    @pl.when((i == 0) | (te_ref[i] != te_ref[jnp.maximum(i - 1, 0)]))
    def _():
        wgu_scr[:, :MOE_D_FF] = wg_ref[0].astype(BF16)
        wgu_scr[:, MOE_D_FF:] = wu_ref[0].astype(BF16)
        wd_scr[...] = wd_ref[0].astype(BF16)

    gu = _dot(x_ref[...], wgu_scr[...])
    act = (_silu(gu[:, :MOE_D_FF]) * gu[:, MOE_D_FF:]).astype(BF16)
    y_ref[...] = _dot(act, wd_scr[...]).astype(y_ref.dtype)


def _experts(tile_expert, xs, wg, wu, wd):
    P, D = xs.shape
    tm = MOE_ROW_TILE
    grid_spec = pltpu.PrefetchScalarGridSpec(
        num_scalar_prefetch=1, grid=(P // tm,),
        in_specs=[pl.BlockSpec((tm, D), lambda i, te: (i, 0)),
                  pl.BlockSpec((1, D, MOE_D_FF), lambda i, te: (te[i], 0, 0)),
                  pl.BlockSpec((1, D, MOE_D_FF), lambda i, te: (te[i], 0, 0)),
                  pl.BlockSpec((1, MOE_D_FF, D), lambda i, te: (te[i], 0, 0))],
        out_specs=pl.BlockSpec((tm, D), lambda i, te: (i, 0)),
        scratch_shapes=[pltpu.VMEM((D, 2 * MOE_D_FF), BF16), pltpu.VMEM((MOE_D_FF, D), BF16)])
    return pl.pallas_call(
        _expert_kernel, grid_spec=grid_spec,
        out_shape=jax.ShapeDtypeStruct((P, D), BF16),
        compiler_params=_params(("arbitrary",)), name="experts")(tile_expert, xs, wg, wu, wd)


def _combine_kernel(x_ref, y0_ref, y1_ref, route_ref, nw_ref, *o_refs, final):
    r = route_ref[...]
    x = x_ref[...] + r[:, 2:3] * y0_ref[...].astype(F32) + r[:, 3:4] * y1_ref[...].astype(F32)
    ms = jnp.mean(x * x, axis=-1, keepdims=True)
    hn = x * lax.rsqrt(ms + NORM_EPS) * nw_ref[...]
    if final:
        o_refs[0][...] = hn
    else:
        o_refs[0][...] = x
        o_refs[1][...] = hn.astype(BF16)


def _combine(x2d, y0, y1, route, nw, *, final, tm):
    T, D = x2d.shape
    rowblk = lambda w: pl.BlockSpec((tm, w), lambda i: (i, 0))
    out_shape = [jax.ShapeDtypeStruct((T, D), F32)] + ([] if final else [jax.ShapeDtypeStruct((T, D), BF16)])
    return pl.pallas_call(
        functools.partial(_combine_kernel, final=final), grid=(T // tm,),
        in_specs=[rowblk(D), rowblk(D), rowblk(D), rowblk(LANES),
                  pl.BlockSpec((1, D), lambda i: (0, 0))],
        out_specs=tuple(rowblk(D) for _ in out_shape), out_shape=tuple(out_shape),
        compiler_params=_params(("parallel",)), name="combine")(x2d, y0, y1, route, nw)


def _rope_tables(seq):
    def tab(dim):
        inv = 1.0 / (ROPE_THETA ** (jnp.arange(0, dim, 2, dtype=F32) / dim))
        ang = jnp.arange(seq, dtype=F32)[:, None] * inv[None, :]
        return jnp.cos(ang), jnp.sin(ang)
    ca, sa = tab(NSA_HEAD_DIM)
    cc, sc = tab(SWA_HEAD_DIM)
    z = jnp.zeros_like(sc)
    tabs_a = (jnp.concatenate([ca, ca], 1), jnp.concatenate([-sa, sa], 1))
    tabs_c = (jnp.concatenate([cc] * 4, 1), jnp.concatenate([-sc, z, -sc, z], 1),
              jnp.concatenate([z, sc, z, sc], 1))
    return tabs_a, tabs_c


def _overlap_t(seq):
    nch = seq // NSA_CMP_STRIDE
    n_sel = seq // NSA_SLC_BLOCK
    cs = np.arange(nch) * NSA_CMP_STRIDE
    ce = cs + NSA_CMP_LEN - 1
    ss = np.arange(n_sel) * NSA_SLC_BLOCK
    ov = (cs[None, :] <= ss[:, None] + NSA_SLC_BLOCK - 1) & (ce[None, :] >= ss[:, None])
    ov[:, nch - 1] = False
    return jnp.asarray(ov.astype(np.float32), BF16)


def _head_expand():
    e = np.zeros((LANES, SSM_D_INNER), np.float32)
    for h in range(SSM_HEADS):
        e[h, h * SSM_HEAD_DIM:(h + 1) * SSM_HEAD_DIM] = 1.0
    return jnp.asarray(e, BF16)


def _split_w_in(w_in):
    o = np.cumsum([0, NSA_Q_DIM, 1536, 24, SSM_D_INNER, SSM_D_INNER + SSM_BC_DIM, SSM_HEADS,
                   SWA_Q_DIM, 256, 3 * D_MODEL])
    w_t = w_in.T
    seg = lambda a, b: w_t[a:b]
    nsa_q = seg(o[0], o[1])
    kv = o[1]
    cmp_kv, slc_k, slc_v = seg(kv, kv + 512), seg(kv + 512, kv + 768), seg(kv + 768, kv + 1024)
    win_k, win_v = seg(kv + 1024, kv + 1280), seg(kv + 1280, kv + 1536)
    nsa_g = seg(o[2], o[3])
    ssm_z = seg(o[3], o[4])
    ssm_xs, ssm_bc = seg(o[4], o[4] + SSM_D_INNER), seg(o[4] + SSM_D_INNER, o[5])
    ssm_dt = seg(o[5], o[6])
    swa_q = seg(o[6], o[7])
    swa_k, swa_v = seg(o[7], o[7] + 128), seg(o[7] + 128, o[8])
    merge_g = seg(o[8], o[9])
    w_a = jnp.concatenate([nsa_q, slc_k, win_k], 0).astype(BF16)
    w_c = jnp.concatenate([swa_q, swa_k], 0).astype(BF16)
    w_n = jnp.concatenate([merge_g, ssm_z, ssm_xs, ssm_bc, cmp_kv, slc_v, win_v, swa_v], 0).astype(BF16)
    pad = jnp.zeros((LANES - SSM_HEADS - 24, w_in.shape[0]), w_in.dtype)
    w_s = jnp.concatenate([ssm_dt, nsa_g, pad], 0).astype(BF16)
    return w_a, w_c, w_n, w_s


def _pad_lanes(v):
    return jnp.pad(v, (0, LANES - v.shape[0]))[None, :]


VT_ROWS = 2 * NSA_KV_HEADS * LANES + LANES
VT_SEQ_TILE = 512


def _vt_kernel(slc_ref, win_ref, swa_ref, o_ref):
    r = 0
    for ref in (slc_ref, win_ref, swa_ref):
        for c in range(ref.shape[1] // LANES):
            blk = ref[:, c * LANES:(c + 1) * LANES].astype(F32)
            o_ref[0, r:r + LANES, :] = blk.T.astype(o_ref.dtype)
            r += LANES


def _values_transposed(n_out, *, batch, seq):
    ts = VT_SEQ_TILE
    ns = seq // ts
    row = lambda b, s: b * ns + s
    return pl.pallas_call(
        _vt_kernel, grid=(batch, ns),
        in_specs=[pl.BlockSpec((ts, 256), lambda b, s: (row(b, s), N_SLCV // 256)),
                  pl.BlockSpec((ts, 256), lambda b, s: (row(b, s), N_WINV // 256)),
                  pl.BlockSpec((ts, LANES), lambda b, s: (row(b, s), N_SWAV // LANES))],
        out_specs=pl.BlockSpec((1, VT_ROWS, ts), lambda b, s: (b, 0, s)),
        out_shape=jax.ShapeDtypeStruct((batch, VT_ROWS, seq), BF16),
        compiler_params=_params(("parallel", "arbitrary")), name="values_t")(n_out, n_out, n_out)


def _rank_kernel(route_ref, rank_ref, cnt_ref, base_scr):
    tm = route_ref.shape[0]

    @pl.when(pl.program_id(0) == 0)
    def _():
        base_scr[...] = jnp.zeros(base_scr.shape, F32)

    r = route_ref[...]
    lane = lax.broadcasted_iota(jnp.int32, (tm, LANES), 1)
    lanef = lane.astype(F32)
    oh0 = jnp.where(r[:, 0:1] == lanef, 1.0, 0.0)
    oh1 = jnp.where(r[:, 1:2] == lanef, 1.0, 0.0)
    tri = (lax.broadcasted_iota(jnp.int32, (tm, tm), 0)
           >= lax.broadcasted_iota(jnp.int32, (tm, tm), 1)).astype(BF16)
    tot = base_scr[0:1, :] + _dot(tri, (oh0 + oh1).astype(BF16))
    rank0 = jnp.sum(oh0 * (tot - 1.0), axis=-1, keepdims=True)
    rank1 = jnp.sum(oh1 * (tot - 1.0), axis=-1, keepdims=True)
    rank_ref[...] = jnp.where(lane == 0, rank0, jnp.where(lane == 1, rank1, 0.0))
    base_scr[0:1, :] = tot[tm - 1:tm, :]
    cnt_ref[...] = jnp.broadcast_to(tot[tm - 1:tm, :], cnt_ref.shape)


def _rank(route, *, tm):
    T = route.shape[0]
    return pl.pallas_call(
        _rank_kernel, grid=(T // tm,),
        in_specs=[pl.BlockSpec((tm, LANES), lambda i: (i, 0))],
        out_specs=(pl.BlockSpec((tm, LANES), lambda i: (i, 0)), pl.BlockSpec((8, LANES), lambda i: (0, 0))),
        out_shape=(jax.ShapeDtypeStruct((T, LANES), F32), jax.ShapeDtypeStruct((8, LANES), F32)),
        scratch_shapes=[pltpu.VMEM((8, LANES), F32)],
        compiler_params=_params(("arbitrary",)), name="moe_rank")(route)


def _dispatch(route, n_tok):
    tm = MOE_ROW_TILE
    n_asg = n_tok * MOE_TOPK
    n_rows = n_asg + MOE_EXPERTS * tm
    rank, cnt = _rank(route, tm=512)
    counts = cnt[0, :MOE_EXPERTS].astype(jnp.int32)
    padded = ((counts + tm - 1) // tm) * tm
    pend = jnp.cumsum(padded)
    pstart = pend - padded
    start = jnp.cumsum(counts) - counts
    eid = route[:, 0:MOE_TOPK].astype(jnp.int32)
    experts = jnp.arange(MOE_EXPERTS, dtype=jnp.int32)
    pstart_tok = jnp.sum(jnp.where(eid[..., None] == experts, pstart, 0), axis=-1)
    pos = pstart_tok + rank[:, 0:MOE_TOPK].astype(jnp.int32)
    order = jnp.argsort(eid.reshape(-1), stable=True)
    tile_start = jnp.arange(n_rows // tm, dtype=jnp.int32) * tm
    tile_expert = jnp.minimum(jnp.sum((tile_start[:, None] >= pend[None, :]).astype(jnp.int32), axis=1),
                              MOE_EXPERTS - 1)
    per_row = lambda table: jnp.repeat(table[tile_expert], tm)
    k = jnp.arange(n_rows, dtype=jnp.int32) - per_row(pstart)
    valid = k < per_row(counts)
    src = jnp.clip(per_row(start) + jnp.where(valid, k, 0), 0, n_asg - 1)
    row_token = jnp.where(valid, order[src] // MOE_TOPK, 0)
    return row_token, pos, tile_expert


def kernel(x, norm_mix, norm_ffn, w_in, nsa_cmp_pos, nsa_cmp_w1, nsa_cmp_b1, nsa_cmp_w2, ssm_conv_w,
           ssm_conv_b, ssm_dt_bias, ssm_a_log, ssm_d, ssm_norm, swa_sinks, proj_nsa, proj_ssm, proj_swa,
           w_out, moe_group_router, moe_group_bias, moe_expert_router, moe_expert_bias, moe_w_gate,
           moe_w_up, moe_w_down, final_norm):
    B, S, D = x.shape
    T = B * S
    depth = w_in.shape[0]
    tm = 512
    tabs_a, tabs_c = _rope_tables(S)
    ovt = _overlap_t(S)
    eh = _head_expand()
    scale_a = jnp.concatenate([jnp.full((NSA_Q_DIM,), NSA_HEAD_DIM ** -0.5, F32),
                               jnp.ones((512,), F32)])[None, :]
    scale_c = jnp.concatenate([jnp.full((SWA_Q_DIM,), SWA_HEAD_DIM ** -0.5, F32),
                               jnp.ones((LANES,), F32)])[None, :]
    xc = x.reshape(T, D)
    hn_mix = _prenorm(xc, norm_mix[0][None, :], tm=tm)
    for l in range(depth):
        w_a, w_c, w_n, w_s = _split_w_in(w_in[l])
        qr, qp = _inproj(hn_mix, w_a, seq=S, tm=1024, tn=512, rope='a', scale=scale_a, tabs=tabs_a,
                         out_dtypes=(BF16, BF16))
        (c_out,) = _inproj(hn_mix, w_c, seq=S, tm=tm, tn=w_c.shape[0], rope='c', scale=scale_c,
                           tabs=tabs_c)
        (n_out,) = _inproj(hn_mix, w_n, seq=S, tm=1024, tn=896)
        (small,) = _inproj(hn_mix, w_s, seq=S, tm=1024, tn=LANES, out_dtypes=(F32,))

        w1r = nsa_cmp_w1[l].reshape(2, NSA_CMP_LEN, NSA_HEAD_DIM, NSA_CMP_HIDDEN).astype(BF16)
        kc, vct = _cmp_mlp(n_out, nsa_cmp_pos[l], w1r, nsa_cmp_b1[l][:, None, :],
                           nsa_cmp_w2[l, 0].astype(BF16), nsa_cmp_w2[l, 1].T.astype(BF16), batch=B, seq=S)
        vt = _values_transposed(n_out, batch=B, seq=S)
        o_a = _nsa_attn(qp, qr, small, kc, vct, ovt, vt, batch=B, seq=S)

        dexp = jnp.repeat(ssm_d[l], SSM_HEAD_DIM)[None, :]
        o_b = _ssd(n_out, small, ssm_conv_w[l], ssm_conv_b[l][None, :], _pad_lanes(ssm_dt_bias[l]),
                   _pad_lanes(ssm_a_log[l]), dexp, ssm_norm[l][None, :], eh, batch=B, seq=S)
        o_c = _swa(swa_sinks[l], c_out, vt, batch=B, seq=S)

        y = _merge(o_a, o_b, o_c, n_out, proj_nsa[l].astype(BF16), proj_ssm[l].astype(BF16),
                   proj_swa[l].astype(BF16), tm=tm)
        wr = jnp.pad(jnp.concatenate([moe_group_router[l], moe_expert_router[l]], 1),
                     ((0, 0), (0, LANES - MOE_GROUPS - MOE_EXPERTS)))
        wrh = wr.astype(BF16)
        wrl = (wr - wrh.astype(F32)).astype(BF16)
        rb = _pad_lanes(jnp.concatenate([moe_group_bias[l], moe_expert_bias[l]]))
        x_mid, hn, route = _outproj(xc, y, w_out[l].astype(BF16), norm_ffn[l][None, :], wrh, wrl, rb,
                                    tm=tm)

        row_token, pos, tile_expert = _dispatch(route, T)
        ys = _experts(tile_expert, hn[row_token], moe_w_gate[l].reshape(MOE_EXPERTS, D, MOE_D_FF),
                      moe_w_up[l].reshape(MOE_EXPERTS, D, MOE_D_FF),
                      moe_w_down[l].reshape(MOE_EXPERTS, MOE_D_FF, D))
        final = l == depth - 1
        nw_next = final_norm if final else norm_mix[l + 1]
        outs = _combine(x_mid, ys[pos[:, 0]], ys[pos[:, 1]], route, nw_next[None, :], final=final, tm=tm)
        xc = outs[0]
        if not final:
            hn_mix = outs[1]
    return xc.reshape(B, S, D)
```

```python
import functools

import jax
import jax.numpy as jnp
import numpy as np
from jax import lax
from jax.experimental import pallas as pl
from jax.experimental.pallas import tpu as pltpu

F32 = jnp.float32
BF16 = jnp.bfloat16

D_MODEL = 2048
ROPE_THETA = 10000.0
NORM_EPS = 1e-6
MASK_VALUE = -1e30
FORCE_SCORE = 1e6

NSA_HEADS = 8
NSA_KV_HEADS = 2
NSA_REP = NSA_HEADS // NSA_KV_HEADS
NSA_HEAD_DIM = 128
NSA_CMP_STRIDE = 16
NSA_CMP_LEN = 32
NSA_CMP_HIDDEN = 256
NSA_SLC_BLOCK = 64
NSA_SLC_TOPK = 16
NSA_WINDOW = 512
NSA_Q_DIM = NSA_HEADS * NSA_HEAD_DIM

SSM_D_INNER = 1024
SSM_HEAD_DIM = 64
SSM_HEADS = 16
SSM_GROUPS = 2
SSM_D_STATE = 128
SSM_CONV = 4
SSM_CHUNK = 128
SSM_BC_DIM = 2 * SSM_GROUPS * SSM_D_STATE

SWA_HEADS = 16
SWA_KV_HEADS = 2
SWA_HEAD_DIM = 64
SWA_WINDOW = 128
SWA_Q_DIM = SWA_HEADS * SWA_HEAD_DIM

MOE_GROUPS = 4
MOE_EXPERTS_PER_GROUP = 8
MOE_EXPERTS = MOE_GROUPS * MOE_EXPERTS_PER_GROUP
MOE_TOPK = 2
MOE_D_FF = 512

LANES = 128
ATT_Q_TILE = 128
SLC_K_TILE = 512
MOE_ROW_TILE = 512
VMEM_LIMIT = 56 * 1024 * 1024

N_MERGE = 0
N_Z = 6144
N_XS = 7168
N_BC = 8192
N_CMP = 8704
N_SLCV = 9216
N_WINV = 9472
N_SWAV = 9728
N_TOTAL = 9856
SMALL_GATE0 = 16


def _params(sem):
    return pltpu.CompilerParams(dimension_semantics=sem, vmem_limit_bytes=VMEM_LIMIT)


def _dot(a, b):
    return jnp.dot(a, b, preferred_element_type=F32)


def _dot_nt(a, b):
    return lax.dot_general(a, b, (((1,), (1,)), ((), ())), preferred_element_type=F32)


def _split_bf16(v, n):
    parts = []
    for _ in range(n):
        p = v.astype(BF16)
        parts.append(p)
        v = v - p.astype(F32)
    return parts


def _expand(v, e):
    hi, lo = _split_bf16(v, 2)
    return _dot(hi, e) + _dot(lo, e)


def _sigmoid(v):
    return 1.0 / (1.0 + jnp.exp(-v))


def _silu(v):
    return v * _sigmoid(v)


def _prenorm_kernel(x_ref, nw_ref, hn_ref):
    x = x_ref[...]
    ms = jnp.mean(x * x, axis=-1, keepdims=True)
    hn_ref[...] = (x * lax.rsqrt(ms + NORM_EPS) * nw_ref[...]).astype(hn_ref.dtype)


def _prenorm(x2d, nw, *, tm):
    T, D = x2d.shape
    return pl.pallas_call(
        _prenorm_kernel, grid=(T // tm,),
        in_specs=[pl.BlockSpec((tm, D), lambda i: (i, 0)), pl.BlockSpec((1, D), lambda i: (0, 0))],
        out_specs=pl.BlockSpec((tm, D), lambda i: (i, 0)),
        out_shape=jax.ShapeDtypeStruct((T, D), BF16),
        compiler_params=_params(("parallel",)), name="prenorm")(x2d, nw)


def _inproj_kernel(*refs, rope, has_scale, n_out):
    it = iter(refs)
    hn_ref, w_ref = next(it), next(it)
    cs_ref = next(it) if has_scale else None
    tabs = [next(it) for _ in range({None: 0, 'a': 2, 'c': 3}[rope])]
    outs = [next(it) for _ in range(n_out)]

    acc = _dot_nt(hn_ref[...], w_ref[...])
    if has_scale:
        acc = acc * cs_ref[...]
    if rope is None:
        outs[0][...] = acc.astype(outs[0].dtype)
        return
    if n_out == 2:
        outs[1][...] = acc.astype(outs[1].dtype)
    for c in range(acc.shape[1] // LANES):
        a = acc[:, c * LANES:(c + 1) * LANES]
        if rope == 'a':
            r = a * tabs[0][...] + pltpu.roll(a, 64, 1) * tabs[1][...]
        else:
            r = (a * tabs[0][...] + pltpu.roll(a, 96, 1) * tabs[1][...]
                 + pltpu.roll(a, 32, 1) * tabs[2][...])
        outs[0][:, c * LANES:(c + 1) * LANES] = r.astype(outs[0].dtype)


def _inproj(hn, w, *, seq, tm, tn, rope=None, scale=None, tabs=(), out_dtypes=(BF16,)):
    T, D = hn.shape
    N = w.shape[0]
    nrow = seq // tm
    in_specs = [pl.BlockSpec((tm, D), lambda i, j: (i, 0)),
                pl.BlockSpec((tn, D), lambda i, j: (j, 0))]
    args = [hn, w]
    if scale is not None:
        in_specs.append(pl.BlockSpec((1, tn), lambda i, j: (0, j)))
        args.append(scale)
    for t in tabs:
        in_specs.append(pl.BlockSpec((tm, LANES), lambda i, j: (i % nrow, 0)))
        args.append(t)
    out_shape = tuple(jax.ShapeDtypeStruct((T, N), dt) for dt in out_dtypes)
    out_specs = tuple(pl.BlockSpec((tm, tn), lambda i, j: (i, j)) for _ in out_dtypes)
    kern = functools.partial(_inproj_kernel, rope=rope, has_scale=scale is not None,
                             n_out=len(out_dtypes))
    return pl.pallas_call(
        kern, grid=(T // tm, N // tn), in_specs=in_specs, out_specs=out_specs, out_shape=out_shape,
        compiler_params=_params(("parallel", "arbitrary")), name="inproj_" + str(rope))(*args)


def _cmp_mlp_kernel(k_ref, v_ref, pos_ref, w1_ref, b1_ref, w2k_ref, w2vt_ref, kc_ref, vct_ref, f32_scr):
    nch = kc_ref.shape[2]
    for c, src in enumerate((k_ref, v_ref)):
        f32_scr[...] = src[...].astype(F32)
        first = jnp.zeros((nch, NSA_CMP_HIDDEN), F32)
        second = jnp.zeros((nch, NSA_CMP_HIDDEN), F32)
        for t in range(NSA_CMP_STRIDE):
            xt = f32_scr[pl.ds(t, nch, stride=NSA_CMP_STRIDE), :]
            first += _dot((xt + pos_ref[c, t:t + 1, :]).astype(BF16), w1_ref[c, t])
            t2 = NSA_CMP_STRIDE + t
            second += _dot((xt + pos_ref[c, t2:t2 + 1, :]).astype(BF16), w1_ref[c, t2])
        hid = _silu(first + pltpu.roll(second, nch - 1, 0) + b1_ref[c]).astype(BF16)
        if c == 0:
            kc_ref[0, 0] = _dot(hid, w2k_ref[...]).astype(kc_ref.dtype)
        else:
            vct_ref[0, 0] = _dot_nt(w2vt_ref[...], hid).astype(vct_ref.dtype)


def _cmp_mlp(n_out, pos, w1r, b1, w2k, w2vt, *, batch, seq):
    nch = seq // NSA_CMP_STRIDE
    cb0 = N_CMP // LANES
    full = lambda a: pl.BlockSpec(a.shape, lambda b, g: (0,) * a.ndim)
    return pl.pallas_call(
        _cmp_mlp_kernel, grid=(batch, NSA_KV_HEADS),
        in_specs=[pl.BlockSpec((seq, LANES), lambda b, g: (b, cb0 + g)),
                  pl.BlockSpec((seq, LANES), lambda b, g: (b, cb0 + NSA_KV_HEADS + g)),
                  full(pos), full(w1r), full(b1), full(w2k), full(w2vt)],
        out_specs=(pl.BlockSpec((1, 1, nch, LANES), lambda b, g: (b, g, 0, 0)),
                   pl.BlockSpec((1, 1, LANES, nch), lambda b, g: (b, g, 0, 0))),
        out_shape=(jax.ShapeDtypeStruct((batch, NSA_KV_HEADS, nch, LANES), BF16),
                   jax.ShapeDtypeStruct((batch, NSA_KV_HEADS, LANES, nch), BF16)),
        scratch_shapes=[pltpu.VMEM((seq, LANES), F32)],
        compiler_params=_params(("parallel", "arbitrary")), name="nsa_cmp_mlp")(
            n_out, n_out, pos, w1r, b1, w2k, w2vt)


def _nsa_attn_kernel(qp_ref, qr_ref, small_ref, kc_ref, vct_ref, ovt_ref, slck_ref, slcvt_ref, wink_ref,
                     winvt_ref, o_ref, q4_scr, sel_scr, m_scr, l_scr, acc_scr, part_scr, *, topk):
    tq = qp_ref.shape[0]
    nch = kc_ref.shape[2]
    n_sel = ovt_ref.shape[0]
    rep = NSA_REP
    tk = SLC_K_TILE
    wlen = NSA_WINDOW + tq
    qt = pl.program_id(1)
    start = qt * tq

    def tile4(v):
        return jnp.concatenate([v] * rep, axis=1)

    def qpos(rows):
        return start + lax.broadcasted_iota(jnp.int32, (rows, tq), 1)

    def sub(rows):
        return lax.broadcasted_iota(jnp.int32, (rows, tq), 0)

    for g in range(NSA_KV_HEADS):
        for r in range(rep):
            h = g * rep + r
            q4_scr[0, g, r * tq:(r + 1) * tq, :] = qp_ref[:, h * LANES:(h + 1) * LANES]
            q4_scr[1, g, r * tq:(r + 1) * tq, :] = qr_ref[:, h * LANES:(h + 1) * LANES]

    gates = _sigmoid(small_ref[...]).T

    def gate4(g, br):
        return jnp.concatenate([gates[SMALL_GATE0 + 3 * (g * rep + r) + br:SMALL_GATE0 + 3 * (g * rep + r) + br + 1, :]
                                for r in range(rep)], axis=1)

    vis = sub(nch) * NSA_CMP_STRIDE + (NSA_CMP_LEN - 1) <= qpos(nch)
    vis_bias = tile4(jnp.where(vis, 0.0, MASK_VALUE))
    vis_one = tile4(jnp.where(vis, 1.0, 0.0))
    jj = sub(n_sel)
    pos_t = qpos(n_sel)
    qblk = pos_t // NSA_SLC_BLOCK
    causal_blk = jj * NSA_SLC_BLOCK <= pos_t
    forced = (jj == 0) | (jj == qblk) | (jj == qblk - 1)
    for g in range(NSA_KV_HEADS):
        s = _dot_nt(kc_ref[0, g], q4_scr[0, g]) + vis_bias
        e = jnp.exp(s - jnp.max(s, axis=0, keepdims=True))
        p = e * (1.0 / jnp.sum(e, axis=0, keepdims=True)) * vis_one
        part_scr[g] = gate4(g, 0) * _dot(vct_ref[0, g], p.astype(BF16))
        psum = p[:, 0:tq]
        for r in range(1, rep):
            psum = psum + p[:, r * tq:(r + 1) * tq]
        imp = sum(_dot(ovt_ref[...], part) for part in _split_bf16(psum, 3))
        imp = jnp.where(causal_blk, imp, MASK_VALUE)
        imp = jnp.where(forced, FORCE_SCORE, imp)
        groups = [imp[r0:r0 + 8, :] for r0 in range(0, n_sel, 8)]
        ranks = [jnp.zeros((8, tq), F32) for _ in groups]
        for jp in range(n_sel):
            row = imp[jp:jp + 1, :]
            for gi, blk in enumerate(groups):
                r0 = gi * 8
                if r0 > jp:
                    beats = jnp.where(row >= blk, 1.0, 0.0)
                elif r0 + 8 <= jp + 1:
                    beats = jnp.where(row > blk, 1.0, 0.0)
                else:
                    beats = jnp.where(jj[r0:r0 + 8, :] > jp, jnp.where(row >= blk, 1.0, 0.0),
                                      jnp.where(row > blk, 1.0, 0.0))
                ranks[gi] = ranks[gi] + beats
        rank = jnp.concatenate(ranks, axis=0)
        sel_scr[g * n_sel:(g + 1) * n_sel, :] = jnp.where(rank < topk, 1.0, 0.0)

    wstart = pl.multiple_of(jnp.maximum(qt - NSA_WINDOW // tq, 0) * tq, tq)
    diff = qpos(wlen) - (wstart + sub(wlen))
    wbias = tile4(jnp.where(diff >= 0, jnp.where(diff < NSA_WINDOW, 0.0, MASK_VALUE), MASK_VALUE))
    for g in range(NSA_KV_HEADS):
        s = _dot_nt(wink_ref[pl.ds(wstart, wlen), g * LANES:(g + 1) * LANES], q4_scr[1, g]) + wbias
        e = jnp.exp(s - jnp.max(s, axis=0, keepdims=True))
        o = _dot(winvt_ref[0, g * LANES:(g + 1) * LANES, pl.ds(wstart, wlen)], e.astype(BF16))
        part_scr[g] = part_scr[g] + gate4(g, 2) * (o * (1.0 / jnp.sum(e, axis=0, keepdims=True)))

    m_scr[...] = jnp.full(m_scr.shape, MASK_VALUE, F32)
    l_scr[...] = jnp.zeros(l_scr.shape, F32)
    acc_scr[...] = jnp.zeros(acc_scr.shape, F32)
    blocks_per_tile = tk // NSA_SLC_BLOCK

    def body(kt, carry):
        base = pl.multiple_of(kt * tk, tk)
        causal_bias = jnp.where(base + sub(tk) <= qpos(tk), 0.0, MASK_VALUE)
        for g in range(NSA_KV_HEADS):
            s = _dot_nt(slck_ref[pl.ds(base, tk), g * LANES:(g + 1) * LANES], q4_scr[1, g])
            picked = jnp.concatenate(
                [jnp.broadcast_to(sel_scr[pl.ds(g * n_sel + kt * blocks_per_tile + i, 1), :],
                                  (NSA_SLC_BLOCK, tq)) for i in range(blocks_per_tile)], axis=0)
            s = s + tile4(jnp.where(picked > 0.5, causal_bias, MASK_VALUE))
            m_prev = m_scr[g]
            m_next = jnp.maximum(m_prev, jnp.max(s, axis=0, keepdims=True))
            alpha = jnp.exp(m_prev - m_next)
            p = jnp.exp(s - m_next)
            l_scr[g] = alpha * l_scr[g] + jnp.sum(p, axis=0, keepdims=True)
            acc_scr[g] = acc_scr[g] * alpha + _dot(slcvt_ref[0, g * LANES:(g + 1) * LANES, pl.ds(base, tk)],
                                                   p.astype(BF16))
            m_scr[g] = m_next
        return carry

    lax.fori_loop(0, (start + tq + tk - 1) // tk, body, 0)

    for g in range(NSA_KV_HEADS):
        o = part_scr[g] + gate4(g, 1) * (acc_scr[g] * (1.0 / l_scr[g]))
        for r in range(rep):
            h = g * rep + r
            o_ref[:, h * LANES:(h + 1) * LANES] = o[:, r * tq:(r + 1) * tq].T.astype(o_ref.dtype)


def _nsa_attn(qp, qr, small, kc, vct, ovt, vt, *, batch, seq):
    tq = ATT_Q_TILE
    assert seq >= NSA_WINDOW + tq and seq % SLC_K_TILE == 0
    nq = seq // tq
    nch = seq // NSA_CMP_STRIDE
    n_sel = seq // NSA_SLC_BLOCK
    T = batch * seq
    row = lambda b, t: (b * nq + t, 0)
    per_b4 = lambda b, t: (b, 0, 0, 0)
    kcol = NSA_Q_DIM // 256
    rows = tq * NSA_REP
    kern = functools.partial(_nsa_attn_kernel, topk=min(NSA_SLC_TOPK, n_sel))
    return pl.pallas_call(
        kern, grid=(batch, nq),
        in_specs=[pl.BlockSpec((tq, NSA_Q_DIM), row), pl.BlockSpec((tq, NSA_Q_DIM), row),
                  pl.BlockSpec((tq, LANES), row),
                  pl.BlockSpec((1, NSA_KV_HEADS, nch, LANES), per_b4),
                  pl.BlockSpec((1, NSA_KV_HEADS, LANES, nch), per_b4),
                  pl.BlockSpec((n_sel, nch), lambda b, t: (0, 0)),
                  pl.BlockSpec((seq, 256), lambda b, t: (b, kcol)),
                  pl.BlockSpec((1, NSA_KV_HEADS * LANES, seq), lambda b, t: (b, 0, 0)),
                  pl.BlockSpec((seq, 256), lambda b, t: (b, kcol + 1)),
                  pl.BlockSpec((1, NSA_KV_HEADS * LANES, seq), lambda b, t: (b, 1, 0))],
        out_specs=pl.BlockSpec((tq, NSA_Q_DIM), row),
        out_shape=jax.ShapeDtypeStruct((T, NSA_Q_DIM), BF16),
        scratch_shapes=[pltpu.VMEM((2, NSA_KV_HEADS, rows, LANES), BF16),
                        pltpu.VMEM((NSA_KV_HEADS * n_sel, tq), F32),
                        pltpu.VMEM((NSA_KV_HEADS, 1, rows), F32), pltpu.VMEM((NSA_KV_HEADS, 1, rows), F32),
                        pltpu.VMEM((NSA_KV_HEADS, LANES, rows), F32),
                        pltpu.VMEM((NSA_KV_HEADS, LANES, rows), F32)],
        compiler_params=_params(("parallel", "arbitrary")), name="nsa_attn")(
            qp, qr, small, kc, vct, ovt, qr, vt, qr, vt)


def _ssd_kernel(xs_ref, bc_ref, z_ref, small_ref, cw_ref, cb_ref, dtb_ref, alog_ref, dexp_ref,
                nw_ref, eh_ref, o_ref, xs_scr, bc_scr, h_scr):
    L = SSM_CHUNK
    P2 = SSM_D_INNER // SSM_GROUPS
    N = SSM_D_STATE
    c = pl.program_id(1)

    @pl.when(c == 0)
    def _():
        xs_scr[0:8, :] = jnp.zeros((8, SSM_D_INNER), F32)
        bc_scr[0:8, :] = jnp.zeros((8, SSM_BC_DIM), F32)
        h_scr[...] = jnp.zeros(h_scr.shape, F32)

    xs_scr[8:8 + L, :] = xs_ref[...].astype(F32)
    bc_scr[8:8 + L, :] = bc_ref[...].astype(F32)

    def conv(scr, col0, width):
        acc = jnp.zeros((L, width), F32) + cb_ref[:, col0:col0 + width]
        for k in range(SSM_CONV):
            acc += scr[8 - (SSM_CONV - 1) + k:8 - (SSM_CONV - 1) + k + L, :] * cw_ref[k:k + 1, col0:col0 + width]
        return _silu(acc)

    xs = conv(xs_scr, 0, SSM_D_INNER)
    bcm = conv(bc_scr, SSM_D_INNER, SSM_BC_DIM)
    xs_scr[0:8, :] = xs_scr[L:L + 8, :]
    bc_scr[0:8, :] = bc_scr[L:L + 8, :]

    lane = lax.broadcasted_iota(jnp.int32, (L, LANES), 1)
    pre = small_ref[...] + dtb_ref[...]
    dt = jnp.maximum(pre, 0.0) + jnp.log(1.0 + jnp.exp(-jnp.abs(pre)))
    dt = jnp.where(lane < SSM_HEADS, dt, 0.0)
    a = dt * (-jnp.exp(alog_ref[...]))
    tri = (lax.broadcasted_iota(jnp.int32, (L, L), 0)
           >= lax.broadcasted_iota(jnp.int32, (L, L), 1))
    tri_b = tri.astype(BF16)
    a_cs = sum(_dot(tri_b, part) for part in _split_bf16(a, 3))
    a_cs_t = a_cs.T
    a_end = a_cs[L - 1:L, :]
    eh = eh_ref[...]
    dt_x = _expand(dt, eh)
    ea_x = _expand(jnp.exp(a_cs), eh)
    de_x = _expand(jnp.exp(a_end - a_cs), eh)
    cd_x = _expand(jnp.broadcast_to(jnp.exp(a_end), (8, LANES)), eh)[0:1]

    X = xs * dt_x
    Xb = X.astype(BF16)
    Xe = (X * de_x).astype(BF16)
    lane_lo = lax.broadcasted_iota(jnp.int32, (L, LANES), 1) < SSM_HEAD_DIM
    y_parts = []
    for g in range(SSM_GROUPS):
        Bg = bcm[:, g * N:(g + 1) * N]
        Cg = bcm[:, (SSM_GROUPS + g) * N:(SSM_GROUPS + g + 1) * N]
        Cb = Cg.astype(BF16)
        cbm = _dot_nt(Cb, Bg.astype(BF16))
        hT = h_scr[g]
        y_off = _dot(Cb, hT.astype(BF16)) * ea_x[:, g * P2:(g + 1) * P2]
        y_dg = []
        for pp in range(P2 // LANES):
            h0 = g * (SSM_HEADS // SSM_GROUPS) + 2 * pp
            acc = None
            for e in range(2):
                h = h0 + e
                seg = a_cs[:, h:h + 1] - a_cs_t[h:h + 1, :]
                dec = jnp.exp(jnp.where(tri, seg, MASK_VALUE))
                m = (cbm * dec).astype(BF16)
                col = g * P2 + pp * LANES
                xh = jnp.where(lane_lo if e == 0 else jnp.logical_not(lane_lo), Xb[:, col:col + LANES],
                               jnp.zeros((), BF16))
                t = _dot(m, xh)
                acc = t if acc is None else acc + t
            y_dg.append(acc)
        y_parts.append(jnp.concatenate(y_dg, axis=1) + y_off)
        st = _dot(Bg.T.astype(BF16), Xe[:, g * P2:(g + 1) * P2])
        h_scr[g] = hT * cd_x[:, g * P2:(g + 1) * P2] + st
    y = jnp.concatenate(y_parts, axis=1) + xs * dexp_ref[...]
    y = y * _silu(z_ref[...].astype(F32))
    outs = []
    for g in range(SSM_GROUPS):
        yg = y[:, g * P2:(g + 1) * P2]
        outs.append(yg * lax.rsqrt(jnp.mean(yg * yg, axis=-1, keepdims=True) + NORM_EPS))
    o_ref[...] = (jnp.concatenate(outs, axis=1) * nw_ref[...]).astype(o_ref.dtype)


def _ssd(n_out, small, cw, cb, dtb, alog, dexp, nw, eh, *, batch, seq):
    L = SSM_CHUNK
    nc = seq // L
    T = batch * seq
    row = lambda b, c: b * nc + c
    full = lambda shape: pl.BlockSpec(shape, lambda b, c: (0,) * len(shape))
    return pl.pallas_call(
        _ssd_kernel, grid=(batch, nc),
        in_specs=[pl.BlockSpec((L, SSM_D_INNER), lambda b, c: (row(b, c), N_XS // SSM_D_INNER)),
                  pl.BlockSpec((L, SSM_BC_DIM), lambda b, c: (row(b, c), N_BC // SSM_BC_DIM)),
                  pl.BlockSpec((L, SSM_D_INNER), lambda b, c: (row(b, c), N_Z // SSM_D_INNER)),
                  pl.BlockSpec((L, LANES), lambda b, c: (row(b, c), 0)),
                  full(cw.shape), full(cb.shape), full(dtb.shape), full(alog.shape),
                  full(dexp.shape), full(nw.shape), full(eh.shape)],
        out_specs=pl.BlockSpec((L, SSM_D_INNER), lambda b, c: (row(b, c), 0)),
        out_shape=jax.ShapeDtypeStruct((T, SSM_D_INNER), BF16),
        scratch_shapes=[pltpu.VMEM((L + 8, SSM_D_INNER), F32), pltpu.VMEM((L + 8, SSM_BC_DIM), F32),
                        pltpu.VMEM((SSM_GROUPS, SSM_D_STATE, SSM_D_INNER // SSM_GROUPS), F32)],
        compiler_params=_params(("parallel", "arbitrary")), name="ssd")(
            n_out, n_out, n_out, small, cw, cb, dtb, alog, dexp, nw, eh)


def _swa_kernel(sink_ref, q_ref, kp_ref, kc_ref, vtp_ref, vtc_ref, o_ref):
    tq = q_ref.shape[0]
    qt = pl.program_id(1)
    rep = SWA_HEADS // SWA_KV_HEADS
    npair = rep // 2
    nk = 2 * tq
    c_i = lax.broadcasted_iota(jnp.int32, (nk, tq), 0)
    diff = tq + lax.broadcasted_iota(jnp.int32, (nk, tq), 1) - c_i
    first_key = jnp.where(qt > 0, 0, tq)
    bias = jnp.where(diff >= 0, jnp.where(diff < SWA_WINDOW, jnp.where(c_i >= first_key, 0.0, MASK_VALUE),
                                          MASK_VALUE), MASK_VALUE)
    bias = jnp.concatenate([bias] * npair, axis=1)
    kf = jnp.concatenate([kp_ref[...], kc_ref[...]], axis=0).astype(F32)
    ks = pltpu.roll(kf, SWA_HEAD_DIM, 1)
    lane_lo = lax.broadcasted_iota(jnp.int32, (nk, LANES), 1) < SWA_HEAD_DIM
    vt = jnp.concatenate([vtp_ref[0], vtc_ref[0]], axis=1).astype(F32)
    vts = pltpu.roll(vt, SWA_HEAD_DIM, 0)
    row_lo = lax.broadcasted_iota(jnp.int32, (LANES, nk), 0) < SWA_HEAD_DIM
    for g in range(SWA_KV_HEADS):
        k_own, k_swp = (kf, ks) if g == 0 else (ks, kf)
        v_own, v_swp = (vt, vts) if g == 0 else (vts, vt)
        k_e = (jnp.where(lane_lo, k_own, 0.0).astype(BF16), jnp.where(lane_lo, 0.0, k_swp).astype(BF16))
        v_e = (jnp.where(row_lo, v_own, 0.0).astype(BF16), jnp.where(row_lo, 0.0, v_swp).astype(BF16))
        q4 = jnp.concatenate([q_ref[:, (g * npair + pp) * LANES:(g * npair + pp + 1) * LANES]
                              for pp in range(npair)], axis=0)
        o2 = None
        for e in range(2):
            sink = jnp.concatenate([jnp.full((1, tq), sink_ref[g * rep + 2 * pp + e], F32)
                                    for pp in range(npair)], axis=1)
            s = _dot_nt(k_e[e], q4) + bias
            m = jnp.maximum(jnp.max(s, axis=0, keepdims=True), sink)
            ex = jnp.exp(s - m)
            inv = 1.0 / (jnp.sum(ex, axis=0, keepdims=True) + jnp.exp(sink - m))
            t = _dot(v_e[e], ex.astype(BF16)) * inv
            o2 = t if o2 is None else o2 + t
        for pp in range(npair):
            col = (g * npair + pp) * LANES
            o_ref[:, col:col + LANES] = o2[:, pp * tq:(pp + 1) * tq].T.astype(o_ref.dtype)


def _swa(sinks, c_out, vt, *, batch, seq):
    tq = SWA_WINDOW
    nq = seq // tq
    T = batch * seq
    kcol = SWA_Q_DIM // LANES
    vrow = VT_ROWS // LANES - 1
    cur = lambda b, t: b * nq + t
    prev = lambda b, t: b * nq + jnp.maximum(t - 1, 0)
    return pl.pallas_call(
        _swa_kernel, grid=(batch, nq),
        in_specs=[pl.BlockSpec(memory_space=pltpu.SMEM),
                  pl.BlockSpec((tq, SWA_Q_DIM), lambda b, t: (cur(b, t), 0)),
                  pl.BlockSpec((tq, LANES), lambda b, t: (prev(b, t), kcol)),
                  pl.BlockSpec((tq, LANES), lambda b, t: (cur(b, t), kcol)),
                  pl.BlockSpec((1, LANES, tq), lambda b, t: (b, vrow, jnp.maximum(t - 1, 0))),
                  pl.BlockSpec((1, LANES, tq), lambda b, t: (b, vrow, t))],
        out_specs=pl.BlockSpec((tq, SWA_Q_DIM), lambda b, t: (cur(b, t), 0)),
        out_shape=jax.ShapeDtypeStruct((T, SWA_Q_DIM), BF16),
        compiler_params=_params(("parallel", "arbitrary")), name="swa")(
            sinks, c_out, c_out, c_out, vt, vt)


def _merge_kernel(oa_ref, ob_ref, oc_ref, mg_ref, pa_ref, pb_ref, pc_ref, y_ref):
    D = D_MODEL
    y = _sigmoid(mg_ref[:, 0:D].astype(F32)) * _dot(oa_ref[...], pa_ref[...])
    y += _sigmoid(mg_ref[:, D:2 * D].astype(F32)) * _dot(ob_ref[...], pb_ref[...])
    y += _sigmoid(mg_ref[:, 2 * D:3 * D].astype(F32)) * _dot(oc_ref[...], pc_ref[...])
    y_ref[...] = y.astype(y_ref.dtype)


def _merge(oa, ob, oc, n_out, pa, pb, pc, *, tm):
    T = oa.shape[0]
    rowblk = lambda w: pl.BlockSpec((tm, w), lambda i: (i, 0))
    const = lambda a: pl.BlockSpec(a.shape, lambda i: (0,) * a.ndim, pipeline_mode=pl.Buffered(1))
    return pl.pallas_call(
        _merge_kernel, grid=(T // tm,),
        in_specs=[rowblk(NSA_Q_DIM), rowblk(SSM_D_INNER), rowblk(SWA_Q_DIM), rowblk(3 * D_MODEL),
                  const(pa), const(pb), const(pc)],
        out_specs=rowblk(D_MODEL),
        out_shape=jax.ShapeDtypeStruct((T, D_MODEL), BF16),
        compiler_params=_params(("parallel",)), name="merge")(oa, ob, oc, n_out, pa, pb, pc)


def _outproj_kernel(x_ref, y_ref, wo_ref, nw_ref, wrh_ref, wrl_ref, rb_ref, xo_ref, hn_ref, route_ref):
    x = x_ref[...] + _dot(y_ref[...], wo_ref[...])
    xo_ref[...] = x
    ms = jnp.mean(x * x, axis=-1, keepdims=True)
    hn = x * lax.rsqrt(ms + NORM_EPS) * nw_ref[...]
    hn_hi, hn_lo = _split_bf16(hn, 2)
    hn_ref[...] = hn_hi
    logit = (_dot(hn_hi, wrh_ref[...]) + _dot(hn_lo, wrh_ref[...]) + _dot(hn_hi, wrl_ref[...])
             + rb_ref[...])
    tm = logit.shape[0]
    lane = lax.broadcasted_iota(jnp.int32, (tm, LANES), 1)
    big = jnp.int32(LANES)
    gl = jnp.where(lane < MOE_GROUPS, logit, -jnp.inf)
    gmax = jnp.max(gl, axis=-1, keepdims=True)
    gidx = jnp.min(jnp.where(gl == gmax, lane, big), axis=-1, keepdims=True)
    gw = 1.0 / jnp.sum(jnp.exp(gl - gmax), axis=-1, keepdims=True)
    lo = MOE_GROUPS + MOE_EXPERTS_PER_GROUP * gidx
    el = jnp.where((lane >= lo) & (lane < lo + MOE_EXPERTS_PER_GROUP), logit, -jnp.inf)
    m1 = jnp.max(el, axis=-1, keepdims=True)
    i1 = jnp.min(jnp.where(el == m1, lane, big), axis=-1, keepdims=True)
    el2 = jnp.where(lane == i1, -jnp.inf, el)
    m2 = jnp.max(el2, axis=-1, keepdims=True)
    i2 = jnp.min(jnp.where(el2 == m2, lane, big), axis=-1, keepdims=True)
    e2 = jnp.exp(m2 - m1)
    w1 = gw / (1.0 + e2)
    w2 = gw * e2 / (1.0 + e2)
    route = jnp.where(lane == 0, (i1 - MOE_GROUPS).astype(F32),
                      jnp.where(lane == 1, (i2 - MOE_GROUPS).astype(F32),
                                jnp.where(lane == 2, w1, jnp.where(lane == 3, w2, 0.0))))
    route_ref[...] = route


def _outproj(x2d, y, wo, nw, wrh, wrl, rb, *, tm):
    T, D = x2d.shape
    rowblk = lambda w: pl.BlockSpec((tm, w), lambda i: (i, 0))
    const = lambda a: pl.BlockSpec(a.shape, lambda i: (0,) * a.ndim, pipeline_mode=pl.Buffered(1))
    return pl.pallas_call(
        _outproj_kernel, grid=(T // tm,),
        in_specs=[rowblk(D), rowblk(D), const(wo), const(nw), const(wrh), const(wrl), const(rb)],
        out_specs=(rowblk(D), rowblk(D), rowblk(LANES)),
        out_shape=(jax.ShapeDtypeStruct((T, D), F32), jax.ShapeDtypeStruct((T, D), BF16),
                   jax.ShapeDtypeStruct((T, LANES), F32)),
        compiler_params=_params(("parallel",)), name="outproj")(x2d, y, wo, nw, wrh, wrl, rb)


def _expert_kernel(te_ref, x_ref, wg_ref, wu_ref, wd_ref, y_ref, wgu_scr, wd_scr):
    i = pl.program_id(0)

    @pl.when((i == 0) | (te_ref[i] != te_ref[jnp.maximum(i - 1, 0)]))
    def _():
        wgu_scr[:, :MOE_D_FF] = wg_ref[0].astype(BF16)
        wgu_scr[:, MOE_D_FF:] = wu_ref[0].astype(BF16)
        wd_scr[...] = wd_ref[0].astype(BF16)

    gu = _dot(x_ref[...], wgu_scr[...])
    act = (_silu(gu[:, :MOE_D_FF]) * gu[:, MOE_D_FF:]).astype(BF16)
    y_ref[...] = _dot(act, wd_scr[...]).astype(y_ref.dtype)


def _experts(tile_expert, xs, wg, wu, wd, *, layer):
    P, D = xs.shape
    tm = MOE_ROW_TILE
    first = layer * MOE_EXPERTS
    grid_spec = pltpu.PrefetchScalarGridSpec(
        num_scalar_prefetch=1, grid=(P // tm,),
        in_specs=[pl.BlockSpec((tm, D), lambda i, te: (i, 0)),
                  pl.BlockSpec((1, D, MOE_D_FF), lambda i, te: (first + te[i], 0, 0)),
                  pl.BlockSpec((1, D, MOE_D_FF), lambda i, te: (first + te[i], 0, 0)),
                  pl.BlockSpec((1, MOE_D_FF, D), lambda i, te: (first + te[i], 0, 0))],
        out_specs=pl.BlockSpec((tm, D), lambda i, te: (i, 0)),
        scratch_shapes=[pltpu.VMEM((D, 2 * MOE_D_FF), BF16), pltpu.VMEM((MOE_D_FF, D), BF16)])
    return pl.pallas_call(
        _expert_kernel, grid_spec=grid_spec,
        out_shape=jax.ShapeDtypeStruct((P, D), BF16),
        compiler_params=_params(("arbitrary",)), name="experts")(tile_expert, xs, wg, wu, wd)


def _combine_kernel(x_ref, y0_ref, y1_ref, route_ref, nw_ref, *o_refs, final):
    r = route_ref[...]
    x = x_ref[...] + r[:, 2:3] * y0_ref[...].astype(F32) + r[:, 3:4] * y1_ref[...].astype(F32)
    ms = jnp.mean(x * x, axis=-1, keepdims=True)
    hn = x * lax.rsqrt(ms + NORM_EPS) * nw_ref[...]
    if final:
        o_refs[0][...] = hn
    else:
        o_refs[0][...] = x
        o_refs[1][...] = hn.astype(BF16)


def _combine(x2d, y0, y1, route, nw, *, final, tm):
    T, D = x2d.shape
    rowblk = lambda w: pl.BlockSpec((tm, w), lambda i: (i, 0))
    out_shape = [jax.ShapeDtypeStruct((T, D), F32)] + ([] if final else [jax.ShapeDtypeStruct((T, D), BF16)])
    return pl.pallas_call(
        functools.partial(_combine_kernel, final=final), grid=(T // tm,),
        in_specs=[rowblk(D), rowblk(D), rowblk(D), rowblk(LANES),
                  pl.BlockSpec((1, D), lambda i: (0, 0))],
        out_specs=tuple(rowblk(D) for _ in out_shape), out_shape=tuple(out_shape),
        compiler_params=_params(("parallel",)), name="combine")(x2d, y0, y1, route, nw)


def _rope_tables(seq):
    def tab(dim):
        inv = 1.0 / (ROPE_THETA ** (jnp.arange(0, dim, 2, dtype=F32) / dim))
        ang = jnp.arange(seq, dtype=F32)[:, None] * inv[None, :]
        return jnp.cos(ang), jnp.sin(ang)
    ca, sa = tab(NSA_HEAD_DIM)
    cc, sc = tab(SWA_HEAD_DIM)
    z = jnp.zeros_like(sc)
    tabs_a = (jnp.concatenate([ca, ca], 1), jnp.concatenate([-sa, sa], 1))
    tabs_c = (jnp.concatenate([cc] * 4, 1), jnp.concatenate([-sc, z, -sc, z], 1),
              jnp.concatenate([z, sc, z, sc], 1))
    return tabs_a, tabs_c


def _overlap_t(seq):
    nch = seq // NSA_CMP_STRIDE
    n_sel = seq // NSA_SLC_BLOCK
    cs = np.arange(nch) * NSA_CMP_STRIDE
    ce = cs + NSA_CMP_LEN - 1
    ss = np.arange(n_sel) * NSA_SLC_BLOCK
    ov = (cs[None, :] <= ss[:, None] + NSA_SLC_BLOCK - 1) & (ce[None, :] >= ss[:, None])
    ov[:, nch - 1] = False
    return jnp.asarray(ov.astype(np.float32), BF16)


def _head_expand():
    e = np.zeros((LANES, SSM_D_INNER), np.float32)
    for h in range(SSM_HEADS):
        e[h, h * SSM_HEAD_DIM:(h + 1) * SSM_HEAD_DIM] = 1.0
    return jnp.asarray(e, BF16)


def _split_w_in(w_in):
    o = np.cumsum([0, NSA_Q_DIM, 1536, 24, SSM_D_INNER, SSM_D_INNER + SSM_BC_DIM, SSM_HEADS,
                   SWA_Q_DIM, 256, 3 * D_MODEL])
    w_t = w_in.T
    seg = lambda a, b: w_t[a:b]
    nsa_q = seg(o[0], o[1])
    kv = o[1]
    cmp_kv, slc_k, slc_v = seg(kv, kv + 512), seg(kv + 512, kv + 768), seg(kv + 768, kv + 1024)
    win_k, win_v = seg(kv + 1024, kv + 1280), seg(kv + 1280, kv + 1536)
    nsa_g = seg(o[2], o[3])
    ssm_z = seg(o[3], o[4])
    ssm_xs, ssm_bc = seg(o[4], o[4] + SSM_D_INNER), seg(o[4] + SSM_D_INNER, o[5])
    ssm_dt = seg(o[5], o[6])
    swa_q = seg(o[6], o[7])
    swa_k, swa_v = seg(o[7], o[7] + 128), seg(o[7] + 128, o[8])
    merge_g = seg(o[8], o[9])
    w_a = jnp.concatenate([nsa_q, slc_k, win_k], 0).astype(BF16)
    w_c = jnp.concatenate([swa_q, swa_k], 0).astype(BF16)
    w_n = jnp.concatenate([merge_g, ssm_z, ssm_xs, ssm_bc, cmp_kv, slc_v, win_v, swa_v], 0).astype(BF16)
    pad = jnp.zeros((LANES - SSM_HEADS - 24, w_in.shape[0]), w_in.dtype)
    w_s = jnp.concatenate([ssm_dt, nsa_g, pad], 0).astype(BF16)
    return w_a, w_c, w_n, w_s


def _pad_lanes(v):
    return jnp.pad(v, (0, LANES - v.shape[0]))[None, :]


VT_ROWS = 2 * NSA_KV_HEADS * LANES + LANES
VT_SEQ_TILE = 512


def _vt_kernel(slc_ref, win_ref, swa_ref, o_ref):
    r = 0
    for ref in (slc_ref, win_ref, swa_ref):
        for c in range(ref.shape[1] // LANES):
            blk = ref[:, c * LANES:(c + 1) * LANES].astype(F32)
            o_ref[0, r:r + LANES, :] = blk.T.astype(o_ref.dtype)
            r += LANES


def _values_transposed(n_out, *, batch, seq):
    ts = VT_SEQ_TILE
    ns = seq // ts
    row = lambda b, s: b * ns + s
    return pl.pallas_call(
        _vt_kernel, grid=(batch, ns),
        in_specs=[pl.BlockSpec((ts, 256), lambda b, s: (row(b, s), N_SLCV // 256)),
                  pl.BlockSpec((ts, 256), lambda b, s: (row(b, s), N_WINV // 256)),
                  pl.BlockSpec((ts, LANES), lambda b, s: (row(b, s), N_SWAV // LANES))],
        out_specs=pl.BlockSpec((1, VT_ROWS, ts), lambda b, s: (b, 0, s)),
        out_shape=jax.ShapeDtypeStruct((batch, VT_ROWS, seq), BF16),
        compiler_params=_params(("parallel", "arbitrary")), name="values_t")(n_out, n_out, n_out)


def _rank_kernel(route_ref, rank_ref, cnt_ref, base_scr):
    tm = route_ref.shape[0]

    @pl.when(pl.program_id(0) == 0)
    def _():
        base_scr[...] = jnp.zeros(base_scr.shape, F32)

    r = route_ref[...]
    lane = lax.broadcasted_iota(jnp.int32, (tm, LANES), 1)
    lanef = lane.astype(F32)
    oh0 = jnp.where(r[:, 0:1] == lanef, 1.0, 0.0)
    oh1 = jnp.where(r[:, 1:2] == lanef, 1.0, 0.0)
    tri = (lax.broadcasted_iota(jnp.int32, (tm, tm), 0)
           >= lax.broadcasted_iota(jnp.int32, (tm, tm), 1)).astype(BF16)
    tot = base_scr[0:1, :] + _dot(tri, (oh0 + oh1).astype(BF16))
    rank0 = jnp.sum(oh0 * (tot - 1.0), axis=-1, keepdims=True)
    rank1 = jnp.sum(oh1 * (tot - 1.0), axis=-1, keepdims=True)
    rank_ref[...] = jnp.where(lane == 0, rank0, jnp.where(lane == 1, rank1, 0.0))
    base_scr[0:1, :] = tot[tm - 1:tm, :]
    cnt_ref[...] = jnp.broadcast_to(tot[tm - 1:tm, :], cnt_ref.shape)


def _rank(route, *, tm):
    T = route.shape[0]
    return pl.pallas_call(
        _rank_kernel, grid=(T // tm,),
        in_specs=[pl.BlockSpec((tm, LANES), lambda i: (i, 0))],
        out_specs=(pl.BlockSpec((tm, LANES), lambda i: (i, 0)), pl.BlockSpec((8, LANES), lambda i: (0, 0))),
        out_shape=(jax.ShapeDtypeStruct((T, LANES), F32), jax.ShapeDtypeStruct((8, LANES), F32)),
        scratch_shapes=[pltpu.VMEM((8, LANES), F32)],
        compiler_params=_params(("arbitrary",)), name="moe_rank")(route)


def _dispatch(route, n_tok):
    tm = MOE_ROW_TILE
    n_asg = n_tok * MOE_TOPK
    n_rows = n_asg + MOE_EXPERTS * tm
    rank, cnt = _rank(route, tm=512)
    counts = cnt[0, :MOE_EXPERTS].astype(jnp.int32)
    padded = ((counts + tm - 1) // tm) * tm
    pend = jnp.cumsum(padded)
    pstart = pend - padded
    start = jnp.cumsum(counts) - counts
    eid = route[:, 0:MOE_TOPK].astype(jnp.int32)
    experts = jnp.arange(MOE_EXPERTS, dtype=jnp.int32)
    pstart_tok = jnp.sum(jnp.where(eid[..., None] == experts, pstart, 0), axis=-1)
    pos = pstart_tok + rank[:, 0:MOE_TOPK].astype(jnp.int32)
    order = jnp.argsort(eid.reshape(-1), stable=True)
    tile_start = jnp.arange(n_rows // tm, dtype=jnp.int32) * tm
    tile_expert = jnp.minimum(jnp.sum((tile_start[:, None] >= pend[None, :]).astype(jnp.int32), axis=1),
                              MOE_EXPERTS - 1)
    per_row = lambda table: jnp.repeat(table[tile_expert], tm)
    k = jnp.arange(n_rows, dtype=jnp.int32) - per_row(pstart)
    valid = k < per_row(counts)
    src = jnp.clip(per_row(start) + jnp.where(valid, k, 0), 0, n_asg - 1)
    row_token = jnp.where(valid, order[src] // MOE_TOPK, 0)
    return row_token, pos, tile_expert


def kernel(x, norm_mix, norm_ffn, w_in, nsa_cmp_pos, nsa_cmp_w1, nsa_cmp_b1, nsa_cmp_w2, ssm_conv_w,
           ssm_conv_b, ssm_dt_bias, ssm_a_log, ssm_d, ssm_norm, swa_sinks, proj_nsa, proj_ssm, proj_swa,
           w_out, moe_group_router, moe_group_bias, moe_expert_router, moe_expert_bias, moe_w_gate,
           moe_w_up, moe_w_down, final_norm):
    B, S, D = x.shape
    T = B * S
    depth = w_in.shape[0]
    tm = 512
    tabs_a, tabs_c = _rope_tables(S)
    ovt = _overlap_t(S)
    eh = _head_expand()
    scale_a = jnp.concatenate([jnp.full((NSA_Q_DIM,), NSA_HEAD_DIM ** -0.5, F32),
                               jnp.ones((512,), F32)])[None, :]
    scale_c = jnp.concatenate([jnp.full((SWA_Q_DIM,), SWA_HEAD_DIM ** -0.5, F32),
                               jnp.ones((LANES,), F32)])[None, :]
    xc = x.reshape(T, D)
    hn_mix = _prenorm(xc, norm_mix[0][None, :], tm=tm)
    for l in range(depth):
        w_a, w_c, w_n, w_s = _split_w_in(w_in[l])
        qr, qp = _inproj(hn_mix, w_a, seq=S, tm=1024, tn=512, rope='a', scale=scale_a, tabs=tabs_a,
                         out_dtypes=(BF16, BF16))
        (c_out,) = _inproj(hn_mix, w_c, seq=S, tm=tm, tn=w_c.shape[0], rope='c', scale=scale_c,
                           tabs=tabs_c)
        (n_out,) = _inproj(hn_mix, w_n, seq=S, tm=1024, tn=896)
        (small,) = _inproj(hn_mix, w_s, seq=S, tm=1024, tn=LANES, out_dtypes=(F32,))

        w1r = nsa_cmp_w1[l].reshape(2, NSA_CMP_LEN, NSA_HEAD_DIM, NSA_CMP_HIDDEN).astype(BF16)
        kc, vct = _cmp_mlp(n_out, nsa_cmp_pos[l], w1r, nsa_cmp_b1[l][:, None, :],
                           nsa_cmp_w2[l, 0].astype(BF16), nsa_cmp_w2[l, 1].T.astype(BF16), batch=B, seq=S)
        vt = _values_transposed(n_out, batch=B, seq=S)
        o_a = _nsa_attn(qp, qr, small, kc, vct, ovt, vt, batch=B, seq=S)

        dexp = jnp.repeat(ssm_d[l], SSM_HEAD_DIM)[None, :]
        o_b = _ssd(n_out, small, ssm_conv_w[l], ssm_conv_b[l][None, :], _pad_lanes(ssm_dt_bias[l]),
                   _pad_lanes(ssm_a_log[l]), dexp, ssm_norm[l][None, :], eh, batch=B, seq=S)
        o_c = _swa(swa_sinks[l], c_out, vt, batch=B, seq=S)

        y = _merge(o_a, o_b, o_c, n_out, proj_nsa[l].astype(BF16), proj_ssm[l].astype(BF16),
                   proj_swa[l].astype(BF16), tm=tm)
        wr = jnp.pad(jnp.concatenate([moe_group_router[l], moe_expert_router[l]], 1),
                     ((0, 0), (0, LANES - MOE_GROUPS - MOE_EXPERTS)))
        wrh = wr.astype(BF16)
        wrl = (wr - wrh.astype(F32)).astype(BF16)
        rb = _pad_lanes(jnp.concatenate([moe_group_bias[l], moe_expert_bias[l]]))
        x_mid, hn, route = _outproj(xc, y, w_out[l].astype(BF16), norm_ffn[l][None, :], wrh, wrl, rb,
                                    tm=tm)

        row_token, pos, tile_expert = _dispatch(route, T)
        ys = _experts(tile_expert, hn[row_token], moe_w_gate.reshape(depth * MOE_EXPERTS, D, MOE_D_FF),
                      moe_w_up.reshape(depth * MOE_EXPERTS, D, MOE_D_FF),
                      moe_w_down.reshape(depth * MOE_EXPERTS, MOE_D_FF, D), layer=l)
        final = l == depth - 1
        nw_next = final_norm if final else norm_mix[l + 1]
        outs = _combine(x_mid, ys[pos[:, 0]], ys[pos[:, 1]], route, nw_next[None, :], final=final, tm=tm)
        xc = outs[0]
        if not final:
            hn_mix = outs[1]
    return xc.reshape(B, S, D)
```

```python
import functools

import jax
import jax.numpy as jnp
import numpy as np
from jax import lax
from jax.experimental import pallas as pl
from jax.experimental.pallas import tpu as pltpu

F32 = jnp.float32
BF16 = jnp.bfloat16

D_MODEL = 2048
ROPE_THETA = 10000.0
NORM_EPS = 1e-6
MASK_VALUE = -1e30
FORCE_SCORE = 1e6

NSA_HEADS = 8
NSA_KV_HEADS = 2
NSA_REP = NSA_HEADS // NSA_KV_HEADS
NSA_HEAD_DIM = 128
NSA_CMP_STRIDE = 16
NSA_CMP_LEN = 32
NSA_CMP_HIDDEN = 256
NSA_SLC_BLOCK = 64
NSA_SLC_TOPK = 16
NSA_WINDOW = 512
NSA_Q_DIM = NSA_HEADS * NSA_HEAD_DIM

SSM_D_INNER = 1024
SSM_HEAD_DIM = 64
SSM_HEADS = 16
SSM_GROUPS = 2
SSM_D_STATE = 128
SSM_CONV = 4
SSM_CHUNK = 128
SSM_BC_DIM = 2 * SSM_GROUPS * SSM_D_STATE

SWA_HEADS = 16
SWA_KV_HEADS = 2
SWA_HEAD_DIM = 64
SWA_WINDOW = 128
SWA_Q_DIM = SWA_HEADS * SWA_HEAD_DIM

MOE_GROUPS = 4
MOE_EXPERTS_PER_GROUP = 8
MOE_EXPERTS = MOE_GROUPS * MOE_EXPERTS_PER_GROUP
MOE_TOPK = 2
MOE_D_FF = 512

LANES = 128
ROW_CHUNKS = D_MODEL // LANES
ATT_Q_TILE = 128
SLC_K_TILE = 512
MOE_ROW_TILE = 512
VMEM_LIMIT = 56 * 1024 * 1024

N_MERGE = 0
N_Z = 6144
N_XS = 7168
N_BC = 8192
N_CMP = 8704
N_SLCV = 9216
N_WINV = 9472
N_SWAV = 9728
N_TOTAL = 9856
SMALL_GATE0 = 16


def _params(sem):
    return pltpu.CompilerParams(dimension_semantics=sem, vmem_limit_bytes=VMEM_LIMIT)


def _dot(a, b):
    return jnp.dot(a, b, preferred_element_type=F32)


def _dot_nt(a, b):
    return lax.dot_general(a, b, (((1,), (1,)), ((), ())), preferred_element_type=F32)


def _split_bf16(v, n):
    parts = []
    for _ in range(n):
        p = v.astype(BF16)
        parts.append(p)
        v = v - p.astype(F32)
    return parts


def _expand(v, e):
    hi, lo = _split_bf16(v, 2)
    return _dot(hi, e) + _dot(lo, e)


def _sigmoid(v):
    return 1.0 / (1.0 + jnp.exp(-v))


def _silu(v):
    return v * _sigmoid(v)


def _prenorm_kernel(x_ref, nw_ref, hn_ref):
    x = x_ref[...]
    ms = jnp.mean(x * x, axis=-1, keepdims=True)
    hn_ref[...] = (x * lax.rsqrt(ms + NORM_EPS) * nw_ref[...]).astype(hn_ref.dtype)


def _prenorm(x2d, nw, *, tm):
    T, D = x2d.shape
    return pl.pallas_call(
        _prenorm_kernel, grid=(T // tm,),
        in_specs=[pl.BlockSpec((tm, D), lambda i: (i, 0)), pl.BlockSpec((1, D), lambda i: (0, 0))],
        out_specs=pl.BlockSpec((tm, D), lambda i: (i, 0)),
        out_shape=jax.ShapeDtypeStruct((T, D), BF16),
        compiler_params=_params(("parallel",)), name="prenorm")(x2d, nw)


def _inproj_kernel(*refs, rope, has_scale, n_out):
    it = iter(refs)
    hn_ref, w_ref = next(it), next(it)
    cs_ref = next(it) if has_scale else None
    tabs = [next(it) for _ in range({None: 0, 'a': 2, 'c': 3}[rope])]
    outs = [next(it) for _ in range(n_out)]

    acc = _dot_nt(hn_ref[...], w_ref[...])
    if has_scale:
        acc = acc * cs_ref[...]
    if rope is None:
        outs[0][...] = acc.astype(outs[0].dtype)
        return
    if n_out == 2:
        outs[1][...] = acc.astype(outs[1].dtype)
    for c in range(acc.shape[1] // LANES):
        a = acc[:, c * LANES:(c + 1) * LANES]
        if rope == 'a':
            r = a * tabs[0][...] + pltpu.roll(a, 64, 1) * tabs[1][...]
        else:
            r = (a * tabs[0][...] + pltpu.roll(a, 96, 1) * tabs[1][...]
                 + pltpu.roll(a, 32, 1) * tabs[2][...])
        outs[0][:, c * LANES:(c + 1) * LANES] = r.astype(outs[0].dtype)


def _inproj(hn, w, *, seq, tm, tn, rope=None, scale=None, tabs=(), out_dtypes=(BF16,)):
    T, D = hn.shape
    N = w.shape[0]
    nrow = seq // tm
    in_specs = [pl.BlockSpec((tm, D), lambda i, j: (i, 0)),
                pl.BlockSpec((tn, D), lambda i, j: (j, 0))]
    args = [hn, w]
    if scale is not None:
        in_specs.append(pl.BlockSpec((1, tn), lambda i, j: (0, j)))
        args.append(scale)
    for t in tabs:
        in_specs.append(pl.BlockSpec((tm, LANES), lambda i, j: (i % nrow, 0)))
        args.append(t)
    out_shape = tuple(jax.ShapeDtypeStruct((T, N), dt) for dt in out_dtypes)
    out_specs = tuple(pl.BlockSpec((tm, tn), lambda i, j: (i, j)) for _ in out_dtypes)
    kern = functools.partial(_inproj_kernel, rope=rope, has_scale=scale is not None,
                             n_out=len(out_dtypes))
    return pl.pallas_call(
        kern, grid=(T // tm, N // tn), in_specs=in_specs, out_specs=out_specs, out_shape=out_shape,
        compiler_params=_params(("parallel", "arbitrary")), name="inproj_" + str(rope))(*args)


def _cmp_mlp_kernel(k_ref, v_ref, pos_ref, w1_ref, b1_ref, w2k_ref, w2vt_ref, kc_ref, vct_ref, f32_scr):
    nch = kc_ref.shape[2]
    for c, src in enumerate((k_ref, v_ref)):
        f32_scr[...] = src[...].astype(F32)
        first = jnp.zeros((nch, NSA_CMP_HIDDEN), F32)
        second = jnp.zeros((nch, NSA_CMP_HIDDEN), F32)
        for t in range(NSA_CMP_STRIDE):
            xt = f32_scr[pl.ds(t, nch, stride=NSA_CMP_STRIDE), :]
            first += _dot((xt + pos_ref[c, t:t + 1, :]).astype(BF16), w1_ref[c, t])
            t2 = NSA_CMP_STRIDE + t
            second += _dot((xt + pos_ref[c, t2:t2 + 1, :]).astype(BF16), w1_ref[c, t2])
        hid = _silu(first + pltpu.roll(second, nch - 1, 0) + b1_ref[c]).astype(BF16)
        if c == 0:
            kc_ref[0, 0] = _dot(hid, w2k_ref[...]).astype(kc_ref.dtype)
        else:
            vct_ref[0, 0] = _dot_nt(w2vt_ref[...], hid).astype(vct_ref.dtype)


def _cmp_mlp(n_out, pos, w1r, b1, w2k, w2vt, *, batch, seq):
    nch = seq // NSA_CMP_STRIDE
    cb0 = N_CMP // LANES
    full = lambda a: pl.BlockSpec(a.shape, lambda b, g: (0,) * a.ndim)
    return pl.pallas_call(
        _cmp_mlp_kernel, grid=(batch, NSA_KV_HEADS),
        in_specs=[pl.BlockSpec((seq, LANES), lambda b, g: (b, cb0 + g)),
                  pl.BlockSpec((seq, LANES), lambda b, g: (b, cb0 + NSA_KV_HEADS + g)),
                  full(pos), full(w1r), full(b1), full(w2k), full(w2vt)],
        out_specs=(pl.BlockSpec((1, 1, nch, LANES), lambda b, g: (b, g, 0, 0)),
                   pl.BlockSpec((1, 1, LANES, nch), lambda b, g: (b, g, 0, 0))),
        out_shape=(jax.ShapeDtypeStruct((batch, NSA_KV_HEADS, nch, LANES), BF16),
                   jax.ShapeDtypeStruct((batch, NSA_KV_HEADS, LANES, nch), BF16)),
        scratch_shapes=[pltpu.VMEM((seq, LANES), F32)],
        compiler_params=_params(("parallel", "arbitrary")), name="nsa_cmp_mlp")(
            n_out, n_out, pos, w1r, b1, w2k, w2vt)


def _nsa_attn_kernel(qp_ref, qr_ref, small_ref, kc_ref, vct_ref, ovt_ref, slck_ref, slcvt_ref, wink_ref,
                     winvt_ref, o_ref, q4_scr, sel_scr, m_scr, l_scr, acc_scr, part_scr, *, topk):
    tq = qp_ref.shape[0]
    nch = kc_ref.shape[2]
    n_sel = ovt_ref.shape[0]
    rep = NSA_REP
    tk = SLC_K_TILE
    wlen = NSA_WINDOW + tq
    qt = pl.program_id(1)
    start = qt * tq

    def tile4(v):
        return jnp.concatenate([v] * rep, axis=1)

    def qpos(rows):
        return start + lax.broadcasted_iota(jnp.int32, (rows, tq), 1)

    def sub(rows):
        return lax.broadcasted_iota(jnp.int32, (rows, tq), 0)

    for g in range(NSA_KV_HEADS):
        for r in range(rep):
            h = g * rep + r
            q4_scr[0, g, r * tq:(r + 1) * tq, :] = qp_ref[:, h * LANES:(h + 1) * LANES]
            q4_scr[1, g, r * tq:(r + 1) * tq, :] = qr_ref[:, h * LANES:(h + 1) * LANES]

    gates = _sigmoid(small_ref[...]).T

    def gate4(g, br):
        return jnp.concatenate([gates[SMALL_GATE0 + 3 * (g * rep + r) + br:SMALL_GATE0 + 3 * (g * rep + r) + br + 1, :]
                                for r in range(rep)], axis=1)

    vis = sub(nch) * NSA_CMP_STRIDE + (NSA_CMP_LEN - 1) <= qpos(nch)
    vis_bias = tile4(jnp.where(vis, 0.0, MASK_VALUE))
    vis_one = tile4(jnp.where(vis, 1.0, 0.0))
    jj = sub(n_sel)
    pos_t = qpos(n_sel)
    qblk = pos_t // NSA_SLC_BLOCK
    causal_blk = jj * NSA_SLC_BLOCK <= pos_t
    forced = (jj == 0) | (jj == qblk) | (jj == qblk - 1)
    for g in range(NSA_KV_HEADS):
        s = _dot_nt(kc_ref[0, g], q4_scr[0, g]) + vis_bias
        e = jnp.exp(s - jnp.max(s, axis=0, keepdims=True))
        p = e * (1.0 / jnp.sum(e, axis=0, keepdims=True)) * vis_one
        part_scr[g] = gate4(g, 0) * _dot(vct_ref[0, g], p.astype(BF16))
        psum = p[:, 0:tq]
        for r in range(1, rep):
            psum = psum + p[:, r * tq:(r + 1) * tq]
        imp = sum(_dot(ovt_ref[...], part) for part in _split_bf16(psum, 3))
        imp = jnp.where(causal_blk, imp, MASK_VALUE)
        imp = jnp.where(forced, FORCE_SCORE, imp)
        groups = [imp[r0:r0 + 8, :] for r0 in range(0, n_sel, 8)]
        ranks = [jnp.zeros((8, tq), F32) for _ in groups]
        for jp in range(n_sel):
            row = imp[jp:jp + 1, :]
            for gi, blk in enumerate(groups):
                r0 = gi * 8
                if r0 > jp:
                    beats = jnp.where(row >= blk, 1.0, 0.0)
                elif r0 + 8 <= jp + 1:
                    beats = jnp.where(row > blk, 1.0, 0.0)
                else:
                    beats = jnp.where(jj[r0:r0 + 8, :] > jp, jnp.where(row >= blk, 1.0, 0.0),
                                      jnp.where(row > blk, 1.0, 0.0))
                ranks[gi] = ranks[gi] + beats
        rank = jnp.concatenate(ranks, axis=0)
        sel_scr[g * n_sel:(g + 1) * n_sel, :] = jnp.where(rank < topk, 1.0, 0.0)

    wstart = pl.multiple_of(jnp.maximum(qt - NSA_WINDOW // tq, 0) * tq, tq)
    diff = qpos(wlen) - (wstart + sub(wlen))
    wbias = tile4(jnp.where(diff >= 0, jnp.where(diff < NSA_WINDOW, 0.0, MASK_VALUE), MASK_VALUE))
    for g in range(NSA_KV_HEADS):
        s = _dot_nt(wink_ref[pl.ds(wstart, wlen), g * LANES:(g + 1) * LANES], q4_scr[1, g]) + wbias
        e = jnp.exp(s - jnp.max(s, axis=0, keepdims=True))
        o = _dot(winvt_ref[0, g * LANES:(g + 1) * LANES, pl.ds(wstart, wlen)], e.astype(BF16))
        part_scr[g] = part_scr[g] + gate4(g, 2) * (o * (1.0 / jnp.sum(e, axis=0, keepdims=True)))

    m_scr[...] = jnp.full(m_scr.shape, MASK_VALUE, F32)
    l_scr[...] = jnp.zeros(l_scr.shape, F32)
    acc_scr[...] = jnp.zeros(acc_scr.shape, F32)
    blocks_per_tile = tk // NSA_SLC_BLOCK

    def body(kt, carry):
        base = pl.multiple_of(kt * tk, tk)
        causal_bias = jnp.where(base + sub(tk) <= qpos(tk), 0.0, MASK_VALUE)
        for g in range(NSA_KV_HEADS):
            s = _dot_nt(slck_ref[pl.ds(base, tk), g * LANES:(g + 1) * LANES], q4_scr[1, g])
            picked = jnp.concatenate(
                [jnp.broadcast_to(sel_scr[pl.ds(g * n_sel + kt * blocks_per_tile + i, 1), :],
                                  (NSA_SLC_BLOCK, tq)) for i in range(blocks_per_tile)], axis=0)
            s = s + tile4(jnp.where(picked > 0.5, causal_bias, MASK_VALUE))
            m_prev = m_scr[g]
            m_next = jnp.maximum(m_prev, jnp.max(s, axis=0, keepdims=True))
            alpha = jnp.exp(m_prev - m_next)
            p = jnp.exp(s - m_next)
            l_scr[g] = alpha * l_scr[g] + jnp.sum(p, axis=0, keepdims=True)
            acc_scr[g] = acc_scr[g] * alpha + _dot(slcvt_ref[0, g * LANES:(g + 1) * LANES, pl.ds(base, tk)],
                                                   p.astype(BF16))
            m_scr[g] = m_next
        return carry

    lax.fori_loop(0, (start + tq + tk - 1) // tk, body, 0)

    for g in range(NSA_KV_HEADS):
        o = part_scr[g] + gate4(g, 1) * (acc_scr[g] * (1.0 / l_scr[g]))
        for r in range(rep):
            h = g * rep + r
            o_ref[:, h * LANES:(h + 1) * LANES] = o[:, r * tq:(r + 1) * tq].T.astype(o_ref.dtype)


def _nsa_attn(qp, qr, small, kc, vct, ovt, vt, *, batch, seq):
    tq = ATT_Q_TILE
    assert seq >= NSA_WINDOW + tq and seq % SLC_K_TILE == 0
    nq = seq // tq
    nch = seq // NSA_CMP_STRIDE
    n_sel = seq // NSA_SLC_BLOCK
    T = batch * seq
    row = lambda b, t: (b * nq + t, 0)
    per_b4 = lambda b, t: (b, 0, 0, 0)
    kcol = NSA_Q_DIM // 256
    rows = tq * NSA_REP
    kern = functools.partial(_nsa_attn_kernel, topk=min(NSA_SLC_TOPK, n_sel))
    return pl.pallas_call(
        kern, grid=(batch, nq),
        in_specs=[pl.BlockSpec((tq, NSA_Q_DIM), row), pl.BlockSpec((tq, NSA_Q_DIM), row),
                  pl.BlockSpec((tq, LANES), row),
                  pl.BlockSpec((1, NSA_KV_HEADS, nch, LANES), per_b4),
                  pl.BlockSpec((1, NSA_KV_HEADS, LANES, nch), per_b4),
                  pl.BlockSpec((n_sel, nch), lambda b, t: (0, 0)),
                  pl.BlockSpec((seq, 256), lambda b, t: (b, kcol)),
                  pl.BlockSpec((1, NSA_KV_HEADS * LANES, seq), lambda b, t: (b, 0, 0)),
                  pl.BlockSpec((seq, 256), lambda b, t: (b, kcol + 1)),
                  pl.BlockSpec((1, NSA_KV_HEADS * LANES, seq), lambda b, t: (b, 1, 0))],
        out_specs=pl.BlockSpec((tq, NSA_Q_DIM), row),
        out_shape=jax.ShapeDtypeStruct((T, NSA_Q_DIM), BF16),
        scratch_shapes=[pltpu.VMEM((2, NSA_KV_HEADS, rows, LANES), BF16),
                        pltpu.VMEM((NSA_KV_HEADS * n_sel, tq), F32),
                        pltpu.VMEM((NSA_KV_HEADS, 1, rows), F32), pltpu.VMEM((NSA_KV_HEADS, 1, rows), F32),
                        pltpu.VMEM((NSA_KV_HEADS, LANES, rows), F32),
                        pltpu.VMEM((NSA_KV_HEADS, LANES, rows), F32)],
        compiler_params=_params(("parallel", "arbitrary")), name="nsa_attn")(
            qp, qr, small, kc, vct, ovt, qr, vt, qr, vt)


def _ssd_kernel(xs_ref, bc_ref, z_ref, small_ref, cw_ref, cb_ref, dtb_ref, alog_ref, dexp_ref,
                nw_ref, eh_ref, o_ref, xs_scr, bc_scr, h_scr):
    L = SSM_CHUNK
    P2 = SSM_D_INNER // SSM_GROUPS
    N = SSM_D_STATE
    c = pl.program_id(1)

    @pl.when(c == 0)
    def _():
        xs_scr[0:8, :] = jnp.zeros((8, SSM_D_INNER), F32)
        bc_scr[0:8, :] = jnp.zeros((8, SSM_BC_DIM), F32)
        h_scr[...] = jnp.zeros(h_scr.shape, F32)

    xs_scr[8:8 + L, :] = xs_ref[...].astype(F32)
    bc_scr[8:8 + L, :] = bc_ref[...].astype(F32)

    def conv(scr, col0, width):
        acc = jnp.zeros((L, width), F32) + cb_ref[:, col0:col0 + width]
        for k in range(SSM_CONV):
            acc += scr[8 - (SSM_CONV - 1) + k:8 - (SSM_CONV - 1) + k + L, :] * cw_ref[k:k + 1, col0:col0 + width]
        return _silu(acc)

    xs = conv(xs_scr, 0, SSM_D_INNER)
    bcm = conv(bc_scr, SSM_D_INNER, SSM_BC_DIM)
    xs_scr[0:8, :] = xs_scr[L:L + 8, :]
    bc_scr[0:8, :] = bc_scr[L:L + 8, :]

    lane = lax.broadcasted_iota(jnp.int32, (L, LANES), 1)
    pre = small_ref[...] + dtb_ref[...]
    dt = jnp.maximum(pre, 0.0) + jnp.log(1.0 + jnp.exp(-jnp.abs(pre)))
    dt = jnp.where(lane < SSM_HEADS, dt, 0.0)
    a = dt * (-jnp.exp(alog_ref[...]))
    tri = (lax.broadcasted_iota(jnp.int32, (L, L), 0)
           >= lax.broadcasted_iota(jnp.int32, (L, L), 1))
    tri_b = tri.astype(BF16)
    a_cs = sum(_dot(tri_b, part) for part in _split_bf16(a, 3))
    a_cs_t = a_cs.T
    a_end = a_cs[L - 1:L, :]
    eh = eh_ref[...]
    dt_x = _expand(dt, eh)
    ea_x = _expand(jnp.exp(a_cs), eh)
    de_x = _expand(jnp.exp(a_end - a_cs), eh)
    cd_x = _expand(jnp.broadcast_to(jnp.exp(a_end), (8, LANES)), eh)[0:1]

    X = xs * dt_x
    Xb = X.astype(BF16)
    Xe = (X * de_x).astype(BF16)
    lane_lo = lax.broadcasted_iota(jnp.int32, (L, LANES), 1) < SSM_HEAD_DIM
    y_parts = []
    for g in range(SSM_GROUPS):
        Bg = bcm[:, g * N:(g + 1) * N]
        Cg = bcm[:, (SSM_GROUPS + g) * N:(SSM_GROUPS + g + 1) * N]
        Cb = Cg.astype(BF16)
        cbm = _dot_nt(Cb, Bg.astype(BF16))
        hT = h_scr[g]
        y_off = _dot(Cb, hT.astype(BF16)) * ea_x[:, g * P2:(g + 1) * P2]
        y_dg = []
        for pp in range(P2 // LANES):
            h0 = g * (SSM_HEADS // SSM_GROUPS) + 2 * pp
            acc = None
            for e in range(2):
                h = h0 + e
                seg = a_cs[:, h:h + 1] - a_cs_t[h:h + 1, :]
                dec = jnp.exp(jnp.where(tri, seg, MASK_VALUE))
                m = (cbm * dec).astype(BF16)
                col = g * P2 + pp * LANES
                xh = jnp.where(lane_lo if e == 0 else jnp.logical_not(lane_lo), Xb[:, col:col + LANES],
                               jnp.zeros((), BF16))
                t = _dot(m, xh)
                acc = t if acc is None else acc + t
            y_dg.append(acc)
        y_parts.append(jnp.concatenate(y_dg, axis=1) + y_off)
        st = _dot(Bg.T.astype(BF16), Xe[:, g * P2:(g + 1) * P2])
        h_scr[g] = hT * cd_x[:, g * P2:(g + 1) * P2] + st
    y = jnp.concatenate(y_parts, axis=1) + xs * dexp_ref[...]
    y = y * _silu(z_ref[...].astype(F32))
    outs = []
    for g in range(SSM_GROUPS):
        yg = y[:, g * P2:(g + 1) * P2]
        outs.append(yg * lax.rsqrt(jnp.mean(yg * yg, axis=-1, keepdims=True) + NORM_EPS))
    o_ref[...] = (jnp.concatenate(outs, axis=1) * nw_ref[...]).astype(o_ref.dtype)


def _ssd(n_out, small, cw, cb, dtb, alog, dexp, nw, eh, *, batch, seq):
    L = SSM_CHUNK
    nc = seq // L
    T = batch * seq
    row = lambda b, c: b * nc + c
    full = lambda shape: pl.BlockSpec(shape, lambda b, c: (0,) * len(shape))
    return pl.pallas_call(
        _ssd_kernel, grid=(batch, nc),
        in_specs=[pl.BlockSpec((L, SSM_D_INNER), lambda b, c: (row(b, c), N_XS // SSM_D_INNER)),
                  pl.BlockSpec((L, SSM_BC_DIM), lambda b, c: (row(b, c), N_BC // SSM_BC_DIM)),
                  pl.BlockSpec((L, SSM_D_INNER), lambda b, c: (row(b, c), N_Z // SSM_D_INNER)),
                  pl.BlockSpec((L, LANES), lambda b, c: (row(b, c), 0)),
                  full(cw.shape), full(cb.shape), full(dtb.shape), full(alog.shape),
                  full(dexp.shape), full(nw.shape), full(eh.shape)],
        out_specs=pl.BlockSpec((L, SSM_D_INNER), lambda b, c: (row(b, c), 0)),
        out_shape=jax.ShapeDtypeStruct((T, SSM_D_INNER), BF16),
        scratch_shapes=[pltpu.VMEM((L + 8, SSM_D_INNER), F32), pltpu.VMEM((L + 8, SSM_BC_DIM), F32),
                        pltpu.VMEM((SSM_GROUPS, SSM_D_STATE, SSM_D_INNER // SSM_GROUPS), F32)],
        compiler_params=_params(("parallel", "arbitrary")), name="ssd")(
            n_out, n_out, n_out, small, cw, cb, dtb, alog, dexp, nw, eh)


def _swa_kernel(sink_ref, q_ref, kp_ref, kc_ref, vtp_ref, vtc_ref, o_ref):
    tq = q_ref.shape[0]
    qt = pl.program_id(1)
    rep = SWA_HEADS // SWA_KV_HEADS
    npair = rep // 2
    nk = 2 * tq
    c_i = lax.broadcasted_iota(jnp.int32, (nk, tq), 0)
    diff = tq + lax.broadcasted_iota(jnp.int32, (nk, tq), 1) - c_i
    first_key = jnp.where(qt > 0, 0, tq)
    bias = jnp.where(diff >= 0, jnp.where(diff < SWA_WINDOW, jnp.where(c_i >= first_key, 0.0, MASK_VALUE),
                                          MASK_VALUE), MASK_VALUE)
    bias = jnp.concatenate([bias] * npair, axis=1)
    kf = jnp.concatenate([kp_ref[...], kc_ref[...]], axis=0).astype(F32)
    ks = pltpu.roll(kf, SWA_HEAD_DIM, 1)
    lane_lo = lax.broadcasted_iota(jnp.int32, (nk, LANES), 1) < SWA_HEAD_DIM
    vt = jnp.concatenate([vtp_ref[0], vtc_ref[0]], axis=1).astype(F32)
    vts = pltpu.roll(vt, SWA_HEAD_DIM, 0)
    row_lo = lax.broadcasted_iota(jnp.int32, (LANES, nk), 0) < SWA_HEAD_DIM
    for g in range(SWA_KV_HEADS):
        k_own, k_swp = (kf, ks) if g == 0 else (ks, kf)
        v_own, v_swp = (vt, vts) if g == 0 else (vts, vt)
        k_e = (jnp.where(lane_lo, k_own, 0.0).astype(BF16), jnp.where(lane_lo, 0.0, k_swp).astype(BF16))
        v_e = (jnp.where(row_lo, v_own, 0.0).astype(BF16), jnp.where(row_lo, 0.0, v_swp).astype(BF16))
        q4 = jnp.concatenate([q_ref[:, (g * npair + pp) * LANES:(g * npair + pp + 1) * LANES]
                              for pp in range(npair)], axis=0)
        o2 = None
        for e in range(2):
            sink = jnp.concatenate([jnp.full((1, tq), sink_ref[g * rep + 2 * pp + e], F32)
                                    for pp in range(npair)], axis=1)
            s = _dot_nt(k_e[e], q4) + bias
            m = jnp.maximum(jnp.max(s, axis=0, keepdims=True), sink)
            ex = jnp.exp(s - m)
            inv = 1.0 / (jnp.sum(ex, axis=0, keepdims=True) + jnp.exp(sink - m))
            t = _dot(v_e[e], ex.astype(BF16)) * inv
            o2 = t if o2 is None else o2 + t
        for pp in range(npair):
            col = (g * npair + pp) * LANES
            o_ref[:, col:col + LANES] = o2[:, pp * tq:(pp + 1) * tq].T.astype(o_ref.dtype)


def _swa(sinks, c_out, vt, *, batch, seq):
    tq = SWA_WINDOW
    nq = seq // tq
    T = batch * seq
    kcol = SWA_Q_DIM // LANES
    vrow = VT_ROWS // LANES - 1
    cur = lambda b, t: b * nq + t
    prev = lambda b, t: b * nq + jnp.maximum(t - 1, 0)
    return pl.pallas_call(
        _swa_kernel, grid=(batch, nq),
        in_specs=[pl.BlockSpec(memory_space=pltpu.SMEM),
                  pl.BlockSpec((tq, SWA_Q_DIM), lambda b, t: (cur(b, t), 0)),
                  pl.BlockSpec((tq, LANES), lambda b, t: (prev(b, t), kcol)),
                  pl.BlockSpec((tq, LANES), lambda b, t: (cur(b, t), kcol)),
                  pl.BlockSpec((1, LANES, tq), lambda b, t: (b, vrow, jnp.maximum(t - 1, 0))),
                  pl.BlockSpec((1, LANES, tq), lambda b, t: (b, vrow, t))],
        out_specs=pl.BlockSpec((tq, SWA_Q_DIM), lambda b, t: (cur(b, t), 0)),
        out_shape=jax.ShapeDtypeStruct((T, SWA_Q_DIM), BF16),
        compiler_params=_params(("parallel", "arbitrary")), name="swa")(
            sinks, c_out, c_out, c_out, vt, vt)


def _merge_kernel(oa_ref, ob_ref, oc_ref, mg_ref, pa_ref, pb_ref, pc_ref, y_ref):
    D = D_MODEL
    y = _sigmoid(mg_ref[:, 0:D].astype(F32)) * _dot(oa_ref[...], pa_ref[...])
    y += _sigmoid(mg_ref[:, D:2 * D].astype(F32)) * _dot(ob_ref[...], pb_ref[...])
    y += _sigmoid(mg_ref[:, 2 * D:3 * D].astype(F32)) * _dot(oc_ref[...], pc_ref[...])
    y_ref[...] = y.astype(y_ref.dtype)


def _merge(oa, ob, oc, n_out, pa, pb, pc, *, tm):
    T = oa.shape[0]
    rowblk = lambda w: pl.BlockSpec((tm, w), lambda i: (i, 0))
    const = lambda a: pl.BlockSpec(a.shape, lambda i: (0,) * a.ndim, pipeline_mode=pl.Buffered(1))
    return pl.pallas_call(
        _merge_kernel, grid=(T // tm,),
        in_specs=[rowblk(NSA_Q_DIM), rowblk(SSM_D_INNER), rowblk(SWA_Q_DIM), rowblk(3 * D_MODEL),
                  const(pa), const(pb), const(pc)],
        out_specs=rowblk(D_MODEL),
        out_shape=jax.ShapeDtypeStruct((T, D_MODEL), BF16),
        compiler_params=_params(("parallel",)), name="merge")(oa, ob, oc, n_out, pa, pb, pc)


def _outproj_kernel(x_ref, y_ref, wo_ref, nw_ref, wrh_ref, wrl_ref, rb_ref, xo_ref, hn_ref, route_ref):
    x = x_ref[...] + _dot(y_ref[...], wo_ref[...])
    xo_ref[...] = x
    ms = jnp.mean(x * x, axis=-1, keepdims=True)
    hn = x * lax.rsqrt(ms + NORM_EPS) * nw_ref[...]
    hn_hi, hn_lo = _split_bf16(hn, 2)
    hn_rows = hn_hi.astype(F32)
    for c in range(ROW_CHUNKS):
        hn_ref[pl.ds(c, hn.shape[0], stride=ROW_CHUNKS), :] = hn_rows[:, c * LANES:(c + 1) * LANES]
    logit = (_dot(hn_hi, wrh_ref[...]) + _dot(hn_lo, wrh_ref[...]) + _dot(hn_hi, wrl_ref[...])
             + rb_ref[...])
    tm = logit.shape[0]
    lane = lax.broadcasted_iota(jnp.int32, (tm, LANES), 1)
    big = jnp.int32(LANES)
    gl = jnp.where(lane < MOE_GROUPS, logit, -jnp.inf)
    gmax = jnp.max(gl, axis=-1, keepdims=True)
    gidx = jnp.min(jnp.where(gl == gmax, lane, big), axis=-1, keepdims=True)
    gw = 1.0 / jnp.sum(jnp.exp(gl - gmax), axis=-1, keepdims=True)
    lo = MOE_GROUPS + MOE_EXPERTS_PER_GROUP * gidx
    el = jnp.where((lane >= lo) & (lane < lo + MOE_EXPERTS_PER_GROUP), logit, -jnp.inf)
    m1 = jnp.max(el, axis=-1, keepdims=True)
    i1 = jnp.min(jnp.where(el == m1, lane, big), axis=-1, keepdims=True)
    el2 = jnp.where(lane == i1, -jnp.inf, el)
    m2 = jnp.max(el2, axis=-1, keepdims=True)
    i2 = jnp.min(jnp.where(el2 == m2, lane, big), axis=-1, keepdims=True)
    e2 = jnp.exp(m2 - m1)
    w1 = gw / (1.0 + e2)
    w2 = gw * e2 / (1.0 + e2)
    route = jnp.where(lane == 0, (i1 - MOE_GROUPS).astype(F32),
                      jnp.where(lane == 1, (i2 - MOE_GROUPS).astype(F32),
                                jnp.where(lane == 2, w1, jnp.where(lane == 3, w2, 0.0))))
    route_ref[...] = route


def _outproj(x2d, y, wo, nw, wrh, wrl, rb, *, tm):
    T, D = x2d.shape
    rowblk = lambda w: pl.BlockSpec((tm, w), lambda i: (i, 0))
    const = lambda a: pl.BlockSpec(a.shape, lambda i: (0,) * a.ndim, pipeline_mode=pl.Buffered(1))
    return pl.pallas_call(
        _outproj_kernel, grid=(T // tm,),
        in_specs=[rowblk(D), rowblk(D), const(wo), const(nw), const(wrh), const(wrl), const(rb)],
        out_specs=(rowblk(D), pl.BlockSpec((tm * ROW_CHUNKS, LANES), lambda i: (i, 0)), rowblk(LANES)),
        out_shape=(jax.ShapeDtypeStruct((T, D), F32), jax.ShapeDtypeStruct((T * ROW_CHUNKS, LANES), F32),
                   jax.ShapeDtypeStruct((T, LANES), F32)),
        compiler_params=_params(("parallel",)), name="outproj")(x2d, y, wo, nw, wrh, wrl, rb)


def _expert_kernel(te_ref, tok_ref, tok_next_ref, hn_hbm, wg_ref, wu_ref, wd_ref, y_ref,
                   xbuf, sem, wgu_scr, wd_scr):
    i = pl.program_id(0)
    tm = y_ref.shape[0]
    slot = i % 2

    def token_copy(tok, r, s):
        return pltpu.make_async_copy(
            hn_hbm.at[pl.ds(pl.multiple_of(tok * ROW_CHUNKS, ROW_CHUNKS), ROW_CHUNKS), :],
            xbuf.at[s, pl.ds(pl.multiple_of(r * ROW_CHUNKS, ROW_CHUNKS), ROW_CHUNKS), :], sem.at[s])

    def request(idx_ref, s):
        def body(r, carry):
            token_copy(idx_ref[0, 0, r], r, s).start()
            return carry
        lax.fori_loop(0, tm, body, 0, unroll=8)

    @pl.when(i == 0)
    def _():
        request(tok_ref, 0)

    @pl.when(i + 1 < pl.num_programs(0))
    def _():
        request(tok_next_ref, 1 - slot)

    @pl.when((i == 0) | (te_ref[i] != te_ref[jnp.maximum(i - 1, 0)]))
    def _():
        wgu_scr[:, :MOE_D_FF] = wg_ref[0].astype(BF16)
        wgu_scr[:, MOE_D_FF:] = wu_ref[0].astype(BF16)
        wd_scr[...] = wd_ref[0].astype(BF16)

    pltpu.make_async_copy(hn_hbm.at[pl.ds(0, tm * ROW_CHUNKS), :], xbuf.at[slot], sem.at[slot]).wait()
    x = jnp.concatenate([xbuf[slot, pl.ds(c, tm, stride=ROW_CHUNKS), :].astype(BF16)
                         for c in range(ROW_CHUNKS)], axis=1)
    gu = _dot(x, wgu_scr[...])
    act = (_silu(gu[:, :MOE_D_FF]) * gu[:, MOE_D_FF:]).astype(BF16)
    y_ref[...] = _dot(act, wd_scr[...]).astype(y_ref.dtype)


def _experts(tile_expert, row_token, hn_rows, wg, wu, wd, *, layer):
    tm = MOE_ROW_TILE
    n_tiles = row_token.shape[0] // tm
    D = D_MODEL
    first = layer * MOE_EXPERTS
    tok3 = row_token.reshape(n_tiles, 1, tm)
    smem_tile = lambda idx: pl.BlockSpec((1, 1, tm), idx, memory_space=pltpu.SMEM)
    grid_spec = pltpu.PrefetchScalarGridSpec(
        num_scalar_prefetch=1, grid=(n_tiles,),
        in_specs=[smem_tile(lambda i, te: (i, 0, 0)),
                  smem_tile(lambda i, te: (jnp.minimum(i + 1, n_tiles - 1), 0, 0)),
                  pl.BlockSpec(memory_space=pl.ANY),
                  pl.BlockSpec((1, D, MOE_D_FF), lambda i, te: (first + te[i], 0, 0)),
                  pl.BlockSpec((1, D, MOE_D_FF), lambda i, te: (first + te[i], 0, 0)),
                  pl.BlockSpec((1, MOE_D_FF, D), lambda i, te: (first + te[i], 0, 0))],
        out_specs=pl.BlockSpec((tm, D), lambda i, te: (i, 0)),
        scratch_shapes=[pltpu.VMEM((2, tm * ROW_CHUNKS, LANES), F32), pltpu.SemaphoreType.DMA((2,)),
                        pltpu.VMEM((D, 2 * MOE_D_FF), BF16), pltpu.VMEM((MOE_D_FF, D), BF16)])
    params = pltpu.CompilerParams(dimension_semantics=("arbitrary",), vmem_limit_bytes=VMEM_LIMIT,
                                  disable_bounds_checks=True)
    return pl.pallas_call(
        _expert_kernel, grid_spec=grid_spec,
        out_shape=jax.ShapeDtypeStruct((n_tiles * tm, D), BF16),
        compiler_params=params, name="experts")(tile_expert, tok3, tok3, hn_rows, wg, wu, wd)


def _combine_kernel(x_ref, y0_ref, y1_ref, route_ref, nw_ref, *o_refs, final):
    r = route_ref[...]
    x = x_ref[...] + r[:, 2:3] * y0_ref[...].astype(F32) + r[:, 3:4] * y1_ref[...].astype(F32)
    ms = jnp.mean(x * x, axis=-1, keepdims=True)
    hn = x * lax.rsqrt(ms + NORM_EPS) * nw_ref[...]
    if final:
        o_refs[0][...] = hn
    else:
        o_refs[0][...] = x
        o_refs[1][...] = hn.astype(BF16)


def _combine(x2d, y0, y1, route, nw, *, final, tm):
    T, D = x2d.shape
    rowblk = lambda w: pl.BlockSpec((tm, w), lambda i: (i, 0))
    out_shape = [jax.ShapeDtypeStruct((T, D), F32)] + ([] if final else [jax.ShapeDtypeStruct((T, D), BF16)])
    return pl.pallas_call(
        functools.partial(_combine_kernel, final=final), grid=(T // tm,),
        in_specs=[rowblk(D), rowblk(D), rowblk(D), rowblk(LANES),
                  pl.BlockSpec((1, D), lambda i: (0, 0))],
        out_specs=tuple(rowblk(D) for _ in out_shape), out_shape=tuple(out_shape),
        compiler_params=_params(("parallel",)), name="combine")(x2d, y0, y1, route, nw)


def _rope_tables(seq):
    def tab(dim):
        inv = 1.0 / (ROPE_THETA ** (jnp.arange(0, dim, 2, dtype=F32) / dim))
        ang = jnp.arange(seq, dtype=F32)[:, None] * inv[None, :]
        return jnp.cos(ang), jnp.sin(ang)
    ca, sa = tab(NSA_HEAD_DIM)
    cc, sc = tab(SWA_HEAD_DIM)
    z = jnp.zeros_like(sc)
    tabs_a = (jnp.concatenate([ca, ca], 1), jnp.concatenate([-sa, sa], 1))
    tabs_c = (jnp.concatenate([cc] * 4, 1), jnp.concatenate([-sc, z, -sc, z], 1),
              jnp.concatenate([z, sc, z, sc], 1))
    return tabs_a, tabs_c


def _overlap_t(seq):
    nch = seq // NSA_CMP_STRIDE
    n_sel = seq // NSA_SLC_BLOCK
    cs = np.arange(nch) * NSA_CMP_STRIDE
    ce = cs + NSA_CMP_LEN - 1
    ss = np.arange(n_sel) * NSA_SLC_BLOCK
    ov = (cs[None, :] <= ss[:, None] + NSA_SLC_BLOCK - 1) & (ce[None, :] >= ss[:, None])
    ov[:, nch - 1] = False
    return jnp.asarray(ov.astype(np.float32), BF16)


def _head_expand():
    e = np.zeros((LANES, SSM_D_INNER), np.float32)
    for h in range(SSM_HEADS):
        e[h, h * SSM_HEAD_DIM:(h + 1) * SSM_HEAD_DIM] = 1.0
    return jnp.asarray(e, BF16)


def _split_w_in(w_in):
    o = np.cumsum([0, NSA_Q_DIM, 1536, 24, SSM_D_INNER, SSM_D_INNER + SSM_BC_DIM, SSM_HEADS,
                   SWA_Q_DIM, 256, 3 * D_MODEL])
    w_t = w_in.T
    seg = lambda a, b: w_t[a:b]
    nsa_q = seg(o[0], o[1])
    kv = o[1]
    cmp_kv, slc_k, slc_v = seg(kv, kv + 512), seg(kv + 512, kv + 768), seg(kv + 768, kv + 1024)
    win_k, win_v = seg(kv + 1024, kv + 1280), seg(kv + 1280, kv + 1536)
    nsa_g = seg(o[2], o[3])
    ssm_z = seg(o[3], o[4])
    ssm_xs, ssm_bc = seg(o[4], o[4] + SSM_D_INNER), seg(o[4] + SSM_D_INNER, o[5])
    ssm_dt = seg(o[5], o[6])
    swa_q = seg(o[6], o[7])
    swa_k, swa_v = seg(o[7], o[7] + 128), seg(o[7] + 128, o[8])
    merge_g = seg(o[8], o[9])
    w_a = jnp.concatenate([nsa_q, slc_k, win_k], 0).astype(BF16)
    w_c = jnp.concatenate([swa_q, swa_k], 0).astype(BF16)
    w_n = jnp.concatenate([merge_g, ssm_z, ssm_xs, ssm_bc, cmp_kv, slc_v, win_v, swa_v], 0).astype(BF16)
    pad = jnp.zeros((LANES - SSM_HEADS - 24, w_in.shape[0]), w_in.dtype)
    w_s = jnp.concatenate([ssm_dt, nsa_g, pad], 0).astype(BF16)
    return w_a, w_c, w_n, w_s


def _pad_lanes(v):
    return jnp.pad(v, (0, LANES - v.shape[0]))[None, :]


VT_ROWS = 2 * NSA_KV_HEADS * LANES + LANES
VT_SEQ_TILE = 512


def _vt_kernel(slc_ref, win_ref, swa_ref, o_ref):
    r = 0
    for ref in (slc_ref, win_ref, swa_ref):
        for c in range(ref.shape[1] // LANES):
            blk = ref[:, c * LANES:(c + 1) * LANES].astype(F32)
            o_ref[0, r:r + LANES, :] = blk.T.astype(o_ref.dtype)
            r += LANES


def _values_transposed(n_out, *, batch, seq):
    ts = VT_SEQ_TILE
    ns = seq // ts
    row = lambda b, s: b * ns + s
    return pl.pallas_call(
        _vt_kernel, grid=(batch, ns),
        in_specs=[pl.BlockSpec((ts, 256), lambda b, s: (row(b, s), N_SLCV // 256)),
                  pl.BlockSpec((ts, 256), lambda b, s: (row(b, s), N_WINV // 256)),
                  pl.BlockSpec((ts, LANES), lambda b, s: (row(b, s), N_SWAV // LANES))],
        out_specs=pl.BlockSpec((1, VT_ROWS, ts), lambda b, s: (b, 0, s)),
        out_shape=jax.ShapeDtypeStruct((batch, VT_ROWS, seq), BF16),
        compiler_params=_params(("parallel", "arbitrary")), name="values_t")(n_out, n_out, n_out)


def _rank_kernel(route_ref, rank_ref, cnt_ref, base_scr):
    tm = route_ref.shape[0]

    @pl.when(pl.program_id(0) == 0)
    def _():
        base_scr[...] = jnp.zeros(base_scr.shape, F32)

    r = route_ref[...]
    lane = lax.broadcasted_iota(jnp.int32, (tm, LANES), 1)
    lanef = lane.astype(F32)
    oh0 = jnp.where(r[:, 0:1] == lanef, 1.0, 0.0)
    oh1 = jnp.where(r[:, 1:2] == lanef, 1.0, 0.0)
    tri = (lax.broadcasted_iota(jnp.int32, (tm, tm), 0)
           >= lax.broadcasted_iota(jnp.int32, (tm, tm), 1)).astype(BF16)
    tot = base_scr[0:1, :] + _dot(tri, (oh0 + oh1).astype(BF16))
    rank0 = jnp.sum(oh0 * (tot - 1.0), axis=-1, keepdims=True)
    rank1 = jnp.sum(oh1 * (tot - 1.0), axis=-1, keepdims=True)
    rank_ref[...] = jnp.where(lane == 0, rank0, jnp.where(lane == 1, rank1, 0.0))
    base_scr[0:1, :] = tot[tm - 1:tm, :]
    cnt_ref[...] = jnp.broadcast_to(tot[tm - 1:tm, :], cnt_ref.shape)


def _rank(route, *, tm):
    T = route.shape[0]
    return pl.pallas_call(
        _rank_kernel, grid=(T // tm,),
        in_specs=[pl.BlockSpec((tm, LANES), lambda i: (i, 0))],
        out_specs=(pl.BlockSpec((tm, LANES), lambda i: (i, 0)), pl.BlockSpec((8, LANES), lambda i: (0, 0))),
        out_shape=(jax.ShapeDtypeStruct((T, LANES), F32), jax.ShapeDtypeStruct((8, LANES), F32)),
        scratch_shapes=[pltpu.VMEM((8, LANES), F32)],
        compiler_params=_params(("arbitrary",)), name="moe_rank")(route)


def _dispatch(route, n_tok):
    tm = MOE_ROW_TILE
    n_asg = n_tok * MOE_TOPK
    n_rows = n_asg + MOE_EXPERTS * tm
    rank, cnt = _rank(route, tm=512)
    counts = cnt[0, :MOE_EXPERTS].astype(jnp.int32)
    padded = ((counts + tm - 1) // tm) * tm
    pend = jnp.cumsum(padded)
    pstart = pend - padded
    start = jnp.cumsum(counts) - counts
    eid = route[:, 0:MOE_TOPK].astype(jnp.int32)
    experts = jnp.arange(MOE_EXPERTS, dtype=jnp.int32)
    pstart_tok = jnp.sum(jnp.where(eid[..., None] == experts, pstart, 0), axis=-1)
    pos = pstart_tok + rank[:, 0:MOE_TOPK].astype(jnp.int32)
    order = jnp.argsort(eid.reshape(-1), stable=True)
    tile_start = jnp.arange(n_rows // tm, dtype=jnp.int32) * tm
    tile_expert = jnp.minimum(jnp.sum((tile_start[:, None] >= pend[None, :]).astype(jnp.int32), axis=1),
                              MOE_EXPERTS - 1)
    per_row = lambda table: jnp.repeat(table[tile_expert], tm)
    k = jnp.arange(n_rows, dtype=jnp.int32) - per_row(pstart)
    valid = k < per_row(counts)
    src = jnp.clip(per_row(start) + jnp.where(valid, k, 0), 0, n_asg - 1)
    row_token = jnp.where(valid, order[src] // MOE_TOPK, 0)
    return row_token, pos, tile_expert


def kernel(x, norm_mix, norm_ffn, w_in, nsa_cmp_pos, nsa_cmp_w1, nsa_cmp_b1, nsa_cmp_w2, ssm_conv_w,
           ssm_conv_b, ssm_dt_bias, ssm_a_log, ssm_d, ssm_norm, swa_sinks, proj_nsa, proj_ssm, proj_swa,
           w_out, moe_group_router, moe_group_bias, moe_expert_router, moe_expert_bias, moe_w_gate,
           moe_w_up, moe_w_down, final_norm):
    B, S, D = x.shape
    T = B * S
    depth = w_in.shape[0]
    tm = 512
    tabs_a, tabs_c = _rope_tables(S)
    ovt = _overlap_t(S)
    eh = _head_expand()
    scale_a = jnp.concatenate([jnp.full((NSA_Q_DIM,), NSA_HEAD_DIM ** -0.5, F32),
                               jnp.ones((512,), F32)])[None, :]
    scale_c = jnp.concatenate([jnp.full((SWA_Q_DIM,), SWA_HEAD_DIM ** -0.5, F32),
                               jnp.ones((LANES,), F32)])[None, :]
    xc = x.reshape(T, D)
    hn_mix = _prenorm(xc, norm_mix[0][None, :], tm=tm)
    for l in range(depth):
        w_a, w_c, w_n, w_s = _split_w_in(w_in[l])
        qr, qp = _inproj(hn_mix, w_a, seq=S, tm=1024, tn=512, rope='a', scale=scale_a, tabs=tabs_a,
                         out_dtypes=(BF16, BF16))
        (c_out,) = _inproj(hn_mix, w_c, seq=S, tm=tm, tn=w_c.shape[0], rope='c', scale=scale_c,
                           tabs=tabs_c)
        (n_out,) = _inproj(hn_mix, w_n, seq=S, tm=1024, tn=896)
        (small,) = _inproj(hn_mix, w_s, seq=S, tm=1024, tn=LANES, out_dtypes=(F32,))

        w1r = nsa_cmp_w1[l].reshape(2, NSA_CMP_LEN, NSA_HEAD_DIM, NSA_CMP_HIDDEN).astype(BF16)
        kc, vct = _cmp_mlp(n_out, nsa_cmp_pos[l], w1r, nsa_cmp_b1[l][:, None, :],
                           nsa_cmp_w2[l, 0].astype(BF16), nsa_cmp_w2[l, 1].T.astype(BF16), batch=B, seq=S)
        vt = _values_transposed(n_out, batch=B, seq=S)
        o_a = _nsa_attn(qp, qr, small, kc, vct, ovt, vt, batch=B, seq=S)

        dexp = jnp.repeat(ssm_d[l], SSM_HEAD_DIM)[None, :]
        o_b = _ssd(n_out, small, ssm_conv_w[l], ssm_conv_b[l][None, :], _pad_lanes(ssm_dt_bias[l]),
                   _pad_lanes(ssm_a_log[l]), dexp, ssm_norm[l][None, :], eh, batch=B, seq=S)
        o_c = _swa(swa_sinks[l], c_out, vt, batch=B, seq=S)

        y = _merge(o_a, o_b, o_c, n_out, proj_nsa[l].astype(BF16), proj_ssm[l].astype(BF16),
                   proj_swa[l].astype(BF16), tm=tm)
        wr = jnp.pad(jnp.concatenate([moe_group_router[l], moe_expert_router[l]], 1),
                     ((0, 0), (0, LANES - MOE_GROUPS - MOE_EXPERTS)))
        wrh = wr.astype(BF16)
        wrl = (wr - wrh.astype(F32)).astype(BF16)
        rb = _pad_lanes(jnp.concatenate([moe_group_bias[l], moe_expert_bias[l]]))
        x_mid, hn, route = _outproj(xc, y, w_out[l].astype(BF16), norm_ffn[l][None, :], wrh, wrl, rb,
                                    tm=256)

        row_token, pos, tile_expert = _dispatch(route, T)
        ys = _experts(tile_expert, row_token, hn, moe_w_gate.reshape(depth * MOE_EXPERTS, D, MOE_D_FF),
                      moe_w_up.reshape(depth * MOE_EXPERTS, D, MOE_D_FF),
                      moe_w_down.reshape(depth * MOE_EXPERTS, MOE_D_FF, D), layer=l)
        final = l == depth - 1
        nw_next = final_norm if final else norm_mix[l + 1]
        outs = _combine(x_mid, ys[pos[:, 0]], ys[pos[:, 1]], route, nw_next[None, :], final=final, tm=tm)
        xc = outs[0]
        if not final:
            hn_mix = outs[1]
    return xc.reshape(B, S, D)
```

```python
import functools

import jax
import jax.numpy as jnp
import numpy as np
from jax import lax
from jax.experimental import pallas as pl
from jax.experimental.pallas import tpu as pltpu

F32 = jnp.float32
BF16 = jnp.bfloat16

D_MODEL = 2048
ROPE_THETA = 10000.0
NORM_EPS = 1e-6
MASK_VALUE = -1e30
FORCE_SCORE = 1e6

NSA_HEADS = 8
NSA_KV_HEADS = 2
NSA_REP = NSA_HEADS // NSA_KV_HEADS
NSA_HEAD_DIM = 128
NSA_CMP_STRIDE = 16
NSA_CMP_LEN = 32
NSA_CMP_HIDDEN = 256
NSA_SLC_BLOCK = 64
NSA_SLC_TOPK = 16
NSA_WINDOW = 512
NSA_Q_DIM = NSA_HEADS * NSA_HEAD_DIM

SSM_D_INNER = 1024
SSM_HEAD_DIM = 64
SSM_HEADS = 16
SSM_GROUPS = 2
SSM_D_STATE = 128
SSM_CONV = 4
SSM_CHUNK = 128
SSM_BC_DIM = 2 * SSM_GROUPS * SSM_D_STATE

SWA_HEADS = 16
SWA_KV_HEADS = 2
SWA_HEAD_DIM = 64
SWA_WINDOW = 128
SWA_Q_DIM = SWA_HEADS * SWA_HEAD_DIM

MOE_GROUPS = 4
MOE_EXPERTS_PER_GROUP = 8
MOE_EXPERTS = MOE_GROUPS * MOE_EXPERTS_PER_GROUP
MOE_TOPK = 2
MOE_D_FF = 512

LANES = 128
ROW_CHUNKS = D_MODEL // LANES // 2
U32 = jnp.uint32
ATT_Q_TILE = 128
SLC_K_TILE = 512
MOE_ROW_TILE = 512
VMEM_LIMIT = 56 * 1024 * 1024

N_MERGE = 0
N_Z = 6144
N_XS = 7168
N_BC = 8192
N_CMP = 8704
N_SLCV = 9216
N_WINV = 9472
N_SWAV = 9728
N_TOTAL = 9856
SMALL_GATE0 = 16


def _params(sem):
    return pltpu.CompilerParams(dimension_semantics=sem, vmem_limit_bytes=VMEM_LIMIT)


def _dot(a, b):
    return jnp.dot(a, b, preferred_element_type=F32)


def _dot_nt(a, b):
    return lax.dot_general(a, b, (((1,), (1,)), ((), ())), preferred_element_type=F32)


def _split_bf16(v, n):
    parts = []
    for _ in range(n):
        p = v.astype(BF16)
        parts.append(p)
        v = v - p.astype(F32)
    return parts


def _expand(v, e):
    hi, lo = _split_bf16(v, 2)
    return _dot(hi, e) + _dot(lo, e)


def _sigmoid(v):
    return 1.0 / (1.0 + jnp.exp(-v))


def _silu(v):
    return v * _sigmoid(v)


def _prenorm_kernel(x_ref, nw_ref, hn_ref):
    x = x_ref[...]
    ms = jnp.mean(x * x, axis=-1, keepdims=True)
    hn_ref[...] = (x * lax.rsqrt(ms + NORM_EPS) * nw_ref[...]).astype(hn_ref.dtype)


def _prenorm(x2d, nw, *, tm):
    T, D = x2d.shape
    return pl.pallas_call(
        _prenorm_kernel, grid=(T // tm,),
        in_specs=[pl.BlockSpec((tm, D), lambda i: (i, 0)), pl.BlockSpec((1, D), lambda i: (0, 0))],
        out_specs=pl.BlockSpec((tm, D), lambda i: (i, 0)),
        out_shape=jax.ShapeDtypeStruct((T, D), BF16),
        compiler_params=_params(("parallel",)), name="prenorm")(x2d, nw)


def _inproj_kernel(*refs, rope, has_scale, n_out):
    it = iter(refs)
    hn_ref, w_ref = next(it), next(it)
    cs_ref = next(it) if has_scale else None
    tabs = [next(it) for _ in range({None: 0, 'a': 2, 'c': 3}[rope])]
    outs = [next(it) for _ in range(n_out)]

    acc = _dot_nt(hn_ref[...], w_ref[...])
    if has_scale:
        acc = acc * cs_ref[...]
    if rope is None:
        outs[0][...] = acc.astype(outs[0].dtype)
        return
    if n_out == 2:
        outs[1][...] = acc.astype(outs[1].dtype)
    for c in range(acc.shape[1] // LANES):
        a = acc[:, c * LANES:(c + 1) * LANES]
        if rope == 'a':
            r = a * tabs[0][...] + pltpu.roll(a, 64, 1) * tabs[1][...]
        else:
            r = (a * tabs[0][...] + pltpu.roll(a, 96, 1) * tabs[1][...]
                 + pltpu.roll(a, 32, 1) * tabs[2][...])
        outs[0][:, c * LANES:(c + 1) * LANES] = r.astype(outs[0].dtype)


def _inproj(hn, w, *, seq, tm, tn, rope=None, scale=None, tabs=(), out_dtypes=(BF16,)):
    T, D = hn.shape
    N = w.shape[0]
    nrow = seq // tm
    in_specs = [pl.BlockSpec((tm, D), lambda i, j: (i, 0)),
                pl.BlockSpec((tn, D), lambda i, j: (j, 0))]
    args = [hn, w]
    if scale is not None:
        in_specs.append(pl.BlockSpec((1, tn), lambda i, j: (0, j)))
        args.append(scale)
    for t in tabs:
        in_specs.append(pl.BlockSpec((tm, LANES), lambda i, j: (i % nrow, 0)))
        args.append(t)
    out_shape = tuple(jax.ShapeDtypeStruct((T, N), dt) for dt in out_dtypes)
    out_specs = tuple(pl.BlockSpec((tm, tn), lambda i, j: (i, j)) for _ in out_dtypes)
    kern = functools.partial(_inproj_kernel, rope=rope, has_scale=scale is not None,
                             n_out=len(out_dtypes))
    return pl.pallas_call(
        kern, grid=(T // tm, N // tn), in_specs=in_specs, out_specs=out_specs, out_shape=out_shape,
        compiler_params=_params(("parallel", "arbitrary")), name="inproj_" + str(rope))(*args)


def _cmp_mlp_kernel(k_ref, v_ref, pos_ref, w1_ref, b1_ref, w2k_ref, w2vt_ref, kc_ref, vct_ref, f32_scr):
    nch = kc_ref.shape[2]
    for c, src in enumerate((k_ref, v_ref)):
        f32_scr[...] = src[...].astype(F32)
        first = jnp.zeros((nch, NSA_CMP_HIDDEN), F32)
        second = jnp.zeros((nch, NSA_CMP_HIDDEN), F32)
        for t in range(NSA_CMP_STRIDE):
            xt = f32_scr[pl.ds(t, nch, stride=NSA_CMP_STRIDE), :]
            first += _dot((xt + pos_ref[c, t:t + 1, :]).astype(BF16), w1_ref[c, t])
            t2 = NSA_CMP_STRIDE + t
            second += _dot((xt + pos_ref[c, t2:t2 + 1, :]).astype(BF16), w1_ref[c, t2])
        hid = _silu(first + pltpu.roll(second, nch - 1, 0) + b1_ref[c]).astype(BF16)
        if c == 0:
            kc_ref[0, 0] = _dot(hid, w2k_ref[...]).astype(kc_ref.dtype)
        else:
            vct_ref[0, 0] = _dot_nt(w2vt_ref[...], hid).astype(vct_ref.dtype)


def _cmp_mlp(n_out, pos, w1r, b1, w2k, w2vt, *, batch, seq):
    nch = seq // NSA_CMP_STRIDE
    cb0 = N_CMP // LANES
    full = lambda a: pl.BlockSpec(a.shape, lambda b, g: (0,) * a.ndim)
    return pl.pallas_call(
        _cmp_mlp_kernel, grid=(batch, NSA_KV_HEADS),
        in_specs=[pl.BlockSpec((seq, LANES), lambda b, g: (b, cb0 + g)),
                  pl.BlockSpec((seq, LANES), lambda b, g: (b, cb0 + NSA_KV_HEADS + g)),
                  full(pos), full(w1r), full(b1), full(w2k), full(w2vt)],
        out_specs=(pl.BlockSpec((1, 1, nch, LANES), lambda b, g: (b, g, 0, 0)),
                   pl.BlockSpec((1, 1, LANES, nch), lambda b, g: (b, g, 0, 0))),
        out_shape=(jax.ShapeDtypeStruct((batch, NSA_KV_HEADS, nch, LANES), BF16),
                   jax.ShapeDtypeStruct((batch, NSA_KV_HEADS, LANES, nch), BF16)),
        scratch_shapes=[pltpu.VMEM((seq, LANES), F32)],
        compiler_params=_params(("parallel", "arbitrary")), name="nsa_cmp_mlp")(
            n_out, n_out, pos, w1r, b1, w2k, w2vt)


def _nsa_attn_kernel(qp_ref, qr_ref, small_ref, kc_ref, vct_ref, ovt_ref, slck_ref, slcvt_ref, wink_ref,
                     winvt_ref, o_ref, q4_scr, sel_scr, m_scr, l_scr, acc_scr, part_scr, *, topk):
    tq = qp_ref.shape[0]
    nch = kc_ref.shape[2]
    n_sel = ovt_ref.shape[0]
    rep = NSA_REP
    tk = SLC_K_TILE
    wlen = NSA_WINDOW + tq
    qt = pl.program_id(1)
    start = qt * tq

    def tile4(v):
        return jnp.concatenate([v] * rep, axis=1)

    def qpos(rows):
        return start + lax.broadcasted_iota(jnp.int32, (rows, tq), 1)

    def sub(rows):
        return lax.broadcasted_iota(jnp.int32, (rows, tq), 0)

    for g in range(NSA_KV_HEADS):
        for r in range(rep):
            h = g * rep + r
            q4_scr[0, g, r * tq:(r + 1) * tq, :] = qp_ref[:, h * LANES:(h + 1) * LANES]
            q4_scr[1, g, r * tq:(r + 1) * tq, :] = qr_ref[:, h * LANES:(h + 1) * LANES]

    gates = _sigmoid(small_ref[...]).T

    def gate4(g, br):
        return jnp.concatenate([gates[SMALL_GATE0 + 3 * (g * rep + r) + br:SMALL_GATE0 + 3 * (g * rep + r) + br + 1, :]
                                for r in range(rep)], axis=1)

    vis = sub(nch) * NSA_CMP_STRIDE + (NSA_CMP_LEN - 1) <= qpos(nch)
    vis_bias = tile4(jnp.where(vis, 0.0, MASK_VALUE))
    vis_one = tile4(jnp.where(vis, 1.0, 0.0))
    jj = sub(n_sel)
    pos_t = qpos(n_sel)
    qblk = pos_t // NSA_SLC_BLOCK
    causal_blk = jj * NSA_SLC_BLOCK <= pos_t
    forced = (jj == 0) | (jj == qblk) | (jj == qblk - 1)
    for g in range(NSA_KV_HEADS):
        s = _dot_nt(kc_ref[0, g], q4_scr[0, g]) + vis_bias
        e = jnp.exp(s - jnp.max(s, axis=0, keepdims=True))
        p = e * (1.0 / jnp.sum(e, axis=0, keepdims=True)) * vis_one
        part_scr[g] = gate4(g, 0) * _dot(vct_ref[0, g], p.astype(BF16))
        psum = p[:, 0:tq]
        for r in range(1, rep):
            psum = psum + p[:, r * tq:(r + 1) * tq]
        imp = sum(_dot(ovt_ref[...], part) for part in _split_bf16(psum, 3))
        imp = jnp.where(causal_blk, imp, MASK_VALUE)
        imp = jnp.where(forced, FORCE_SCORE, imp)
        groups = [imp[r0:r0 + 8, :] for r0 in range(0, n_sel, 8)]
        ranks = [jnp.zeros((8, tq), F32) for _ in groups]
        for jp in range(n_sel):
            row = imp[jp:jp + 1, :]
            for gi, blk in enumerate(groups):
                r0 = gi * 8
                if r0 > jp:
                    beats = jnp.where(row >= blk, 1.0, 0.0)
                elif r0 + 8 <= jp + 1:
                    beats = jnp.where(row > blk, 1.0, 0.0)
                else:
                    beats = jnp.where(jj[r0:r0 + 8, :] > jp, jnp.where(row >= blk, 1.0, 0.0),
                                      jnp.where(row > blk, 1.0, 0.0))
                ranks[gi] = ranks[gi] + beats
        rank = jnp.concatenate(ranks, axis=0)
        sel_scr[g * n_sel:(g + 1) * n_sel, :] = jnp.where(rank < topk, 1.0, 0.0)

    wstart = pl.multiple_of(jnp.maximum(qt - NSA_WINDOW // tq, 0) * tq, tq)
    diff = qpos(wlen) - (wstart + sub(wlen))
    wbias = tile4(jnp.where(diff >= 0, jnp.where(diff < NSA_WINDOW, 0.0, MASK_VALUE), MASK_VALUE))
    for g in range(NSA_KV_HEADS):
        s = _dot_nt(wink_ref[pl.ds(wstart, wlen), g * LANES:(g + 1) * LANES], q4_scr[1, g]) + wbias
        e = jnp.exp(s - jnp.max(s, axis=0, keepdims=True))
        o = _dot(winvt_ref[0, g * LANES:(g + 1) * LANES, pl.ds(wstart, wlen)], e.astype(BF16))
        part_scr[g] = part_scr[g] + gate4(g, 2) * (o * (1.0 / jnp.sum(e, axis=0, keepdims=True)))

    m_scr[...] = jnp.full(m_scr.shape, MASK_VALUE, F32)
    l_scr[...] = jnp.zeros(l_scr.shape, F32)
    acc_scr[...] = jnp.zeros(acc_scr.shape, F32)
    blocks_per_tile = tk // NSA_SLC_BLOCK

    def body(kt, carry):
        base = pl.multiple_of(kt * tk, tk)
        causal_bias = jnp.where(base + sub(tk) <= qpos(tk), 0.0, MASK_VALUE)
        for g in range(NSA_KV_HEADS):
            s = _dot_nt(slck_ref[pl.ds(base, tk), g * LANES:(g + 1) * LANES], q4_scr[1, g])
            picked = jnp.concatenate(
                [jnp.broadcast_to(sel_scr[pl.ds(g * n_sel + kt * blocks_per_tile + i, 1), :],
                                  (NSA_SLC_BLOCK, tq)) for i in range(blocks_per_tile)], axis=0)
            s = s + tile4(jnp.where(picked > 0.5, causal_bias, MASK_VALUE))
            m_prev = m_scr[g]
            m_next = jnp.maximum(m_prev, jnp.max(s, axis=0, keepdims=True))
            alpha = jnp.exp(m_prev - m_next)
            p = jnp.exp(s - m_next)
            l_scr[g] = alpha * l_scr[g] + jnp.sum(p, axis=0, keepdims=True)
            acc_scr[g] = acc_scr[g] * alpha + _dot(slcvt_ref[0, g * LANES:(g + 1) * LANES, pl.ds(base, tk)],
                                                   p.astype(BF16))
            m_scr[g] = m_next
        return carry

    lax.fori_loop(0, (start + tq + tk - 1) // tk, body, 0)

    for g in range(NSA_KV_HEADS):
        o = part_scr[g] + gate4(g, 1) * (acc_scr[g] * (1.0 / l_scr[g]))
        for r in range(rep):
            h = g * rep + r
            o_ref[:, h * LANES:(h + 1) * LANES] = o[:, r * tq:(r + 1) * tq].T.astype(o_ref.dtype)


def _nsa_attn(qp, qr, small, kc, vct, ovt, vt, *, batch, seq):
    tq = ATT_Q_TILE
    assert seq >= NSA_WINDOW + tq and seq % SLC_K_TILE == 0
    nq = seq // tq
    nch = seq // NSA_CMP_STRIDE
    n_sel = seq // NSA_SLC_BLOCK
    T = batch * seq
    row = lambda b, t: (b * nq + t, 0)
    per_b4 = lambda b, t: (b, 0, 0, 0)
    kcol = NSA_Q_DIM // 256
    rows = tq * NSA_REP
    kern = functools.partial(_nsa_attn_kernel, topk=min(NSA_SLC_TOPK, n_sel))
    return pl.pallas_call(
        kern, grid=(batch, nq),
        in_specs=[pl.BlockSpec((tq, NSA_Q_DIM), row), pl.BlockSpec((tq, NSA_Q_DIM), row),
                  pl.BlockSpec((tq, LANES), row),
                  pl.BlockSpec((1, NSA_KV_HEADS, nch, LANES), per_b4),
                  pl.BlockSpec((1, NSA_KV_HEADS, LANES, nch), per_b4),
                  pl.BlockSpec((n_sel, nch), lambda b, t: (0, 0)),
                  pl.BlockSpec((seq, 256), lambda b, t: (b, kcol)),
                  pl.BlockSpec((1, NSA_KV_HEADS * LANES, seq), lambda b, t: (b, 0, 0)),
                  pl.BlockSpec((seq, 256), lambda b, t: (b, kcol + 1)),
                  pl.BlockSpec((1, NSA_KV_HEADS * LANES, seq), lambda b, t: (b, 1, 0))],
        out_specs=pl.BlockSpec((tq, NSA_Q_DIM), row),
        out_shape=jax.ShapeDtypeStruct((T, NSA_Q_DIM), BF16),
        scratch_shapes=[pltpu.VMEM((2, NSA_KV_HEADS, rows, LANES), BF16),
                        pltpu.VMEM((NSA_KV_HEADS * n_sel, tq), F32),
                        pltpu.VMEM((NSA_KV_HEADS, 1, rows), F32), pltpu.VMEM((NSA_KV_HEADS, 1, rows), F32),
                        pltpu.VMEM((NSA_KV_HEADS, LANES, rows), F32),
                        pltpu.VMEM((NSA_KV_HEADS, LANES, rows), F32)],
        compiler_params=_params(("parallel", "arbitrary")), name="nsa_attn")(
            qp, qr, small, kc, vct, ovt, qr, vt, qr, vt)


def _ssd_kernel(xs_ref, bc_ref, z_ref, small_ref, cw_ref, cb_ref, dtb_ref, alog_ref, dexp_ref,
                nw_ref, eh_ref, o_ref, xs_scr, bc_scr, h_scr):
    L = SSM_CHUNK
    P2 = SSM_D_INNER // SSM_GROUPS
    N = SSM_D_STATE
    c = pl.program_id(1)

    @pl.when(c == 0)
    def _():
        xs_scr[0:8, :] = jnp.zeros((8, SSM_D_INNER), F32)
        bc_scr[0:8, :] = jnp.zeros((8, SSM_BC_DIM), F32)
        h_scr[...] = jnp.zeros(h_scr.shape, F32)

    xs_scr[8:8 + L, :] = xs_ref[...].astype(F32)
    bc_scr[8:8 + L, :] = bc_ref[...].astype(F32)

    def conv(scr, col0, width):
        acc = jnp.zeros((L, width), F32) + cb_ref[:, col0:col0 + width]
        for k in range(SSM_CONV):
            acc += scr[8 - (SSM_CONV - 1) + k:8 - (SSM_CONV - 1) + k + L, :] * cw_ref[k:k + 1, col0:col0 + width]
        return _silu(acc)

    xs = conv(xs_scr, 0, SSM_D_INNER)
    bcm = conv(bc_scr, SSM_D_INNER, SSM_BC_DIM)
    xs_scr[0:8, :] = xs_scr[L:L + 8, :]
    bc_scr[0:8, :] = bc_scr[L:L + 8, :]

    lane = lax.broadcasted_iota(jnp.int32, (L, LANES), 1)
    pre = small_ref[...] + dtb_ref[...]
    dt = jnp.maximum(pre, 0.0) + jnp.log(1.0 + jnp.exp(-jnp.abs(pre)))
    dt = jnp.where(lane < SSM_HEADS, dt, 0.0)
    a = dt * (-jnp.exp(alog_ref[...]))
    tri = (lax.broadcasted_iota(jnp.int32, (L, L), 0)
           >= lax.broadcasted_iota(jnp.int32, (L, L), 1))
    tri_b = tri.astype(BF16)
    a_cs = sum(_dot(tri_b, part) for part in _split_bf16(a, 3))
    a_cs_t = a_cs.T
    a_end = a_cs[L - 1:L, :]
    eh = eh_ref[...]
    dt_x = _expand(dt, eh)
    ea_x = _expand(jnp.exp(a_cs), eh)
    de_x = _expand(jnp.exp(a_end - a_cs), eh)
    cd_x = _expand(jnp.broadcast_to(jnp.exp(a_end), (8, LANES)), eh)[0:1]

    X = xs * dt_x
    Xb = X.astype(BF16)
    Xe = (X * de_x).astype(BF16)
    lane_lo = lax.broadcasted_iota(jnp.int32, (L, LANES), 1) < SSM_HEAD_DIM
    y_parts = []
    for g in range(SSM_GROUPS):
        Bg = bcm[:, g * N:(g + 1) * N]
        Cg = bcm[:, (SSM_GROUPS + g) * N:(SSM_GROUPS + g + 1) * N]
        Cb = Cg.astype(BF16)
        cbm = _dot_nt(Cb, Bg.astype(BF16))
        hT = h_scr[g]
        y_off = _dot(Cb, hT.astype(BF16)) * ea_x[:, g * P2:(g + 1) * P2]
        y_dg = []
        for pp in range(P2 // LANES):
            h0 = g * (SSM_HEADS // SSM_GROUPS) + 2 * pp
            acc = None
            for e in range(2):
                h = h0 + e
                seg = a_cs[:, h:h + 1] - a_cs_t[h:h + 1, :]
                dec = jnp.exp(jnp.where(tri, seg, MASK_VALUE))
                m = (cbm * dec).astype(BF16)
                col = g * P2 + pp * LANES
                xh = jnp.where(lane_lo if e == 0 else jnp.logical_not(lane_lo), Xb[:, col:col + LANES],
                               jnp.zeros((), BF16))
                t = _dot(m, xh)
                acc = t if acc is None else acc + t
            y_dg.append(acc)
        y_parts.append(jnp.concatenate(y_dg, axis=1) + y_off)
        st = _dot(Bg.T.astype(BF16), Xe[:, g * P2:(g + 1) * P2])
        h_scr[g] = hT * cd_x[:, g * P2:(g + 1) * P2] + st
    y = jnp.concatenate(y_parts, axis=1) + xs * dexp_ref[...]
    y = y * _silu(z_ref[...].astype(F32))
    outs = []
    for g in range(SSM_GROUPS):
        yg = y[:, g * P2:(g + 1) * P2]
        outs.append(yg * lax.rsqrt(jnp.mean(yg * yg, axis=-1, keepdims=True) + NORM_EPS))
    o_ref[...] = (jnp.concatenate(outs, axis=1) * nw_ref[...]).astype(o_ref.dtype)


def _ssd(n_out, small, cw, cb, dtb, alog, dexp, nw, eh, *, batch, seq):
    L = SSM_CHUNK
    nc = seq // L
    T = batch * seq
    row = lambda b, c: b * nc + c
    full = lambda shape: pl.BlockSpec(shape, lambda b, c: (0,) * len(shape))
    return pl.pallas_call(
        _ssd_kernel, grid=(batch, nc),
        in_specs=[pl.BlockSpec((L, SSM_D_INNER), lambda b, c: (row(b, c), N_XS // SSM_D_INNER)),
                  pl.BlockSpec((L, SSM_BC_DIM), lambda b, c: (row(b, c), N_BC // SSM_BC_DIM)),
                  pl.BlockSpec((L, SSM_D_INNER), lambda b, c: (row(b, c), N_Z // SSM_D_INNER)),
                  pl.BlockSpec((L, LANES), lambda b, c: (row(b, c), 0)),
                  full(cw.shape), full(cb.shape), full(dtb.shape), full(alog.shape),
                  full(dexp.shape), full(nw.shape), full(eh.shape)],
        out_specs=pl.BlockSpec((L, SSM_D_INNER), lambda b, c: (row(b, c), 0)),
        out_shape=jax.ShapeDtypeStruct((T, SSM_D_INNER), BF16),
        scratch_shapes=[pltpu.VMEM((L + 8, SSM_D_INNER), F32), pltpu.VMEM((L + 8, SSM_BC_DIM), F32),
                        pltpu.VMEM((SSM_GROUPS, SSM_D_STATE, SSM_D_INNER // SSM_GROUPS), F32)],
        compiler_params=_params(("parallel", "arbitrary")), name="ssd")(
            n_out, n_out, n_out, small, cw, cb, dtb, alog, dexp, nw, eh)


def _swa_kernel(sink_ref, q_ref, kp_ref, kc_ref, vtp_ref, vtc_ref, o_ref):
    tq = q_ref.shape[0]
    qt = pl.program_id(1)
    rep = SWA_HEADS // SWA_KV_HEADS
    npair = rep // 2
    nk = 2 * tq
    c_i = lax.broadcasted_iota(jnp.int32, (nk, tq), 0)
    diff = tq + lax.broadcasted_iota(jnp.int32, (nk, tq), 1) - c_i
    first_key = jnp.where(qt > 0, 0, tq)
    bias = jnp.where(diff >= 0, jnp.where(diff < SWA_WINDOW, jnp.where(c_i >= first_key, 0.0, MASK_VALUE),
                                          MASK_VALUE), MASK_VALUE)
    bias = jnp.concatenate([bias] * npair, axis=1)
    kf = jnp.concatenate([kp_ref[...], kc_ref[...]], axis=0).astype(F32)
    ks = pltpu.roll(kf, SWA_HEAD_DIM, 1)
    lane_lo = lax.broadcasted_iota(jnp.int32, (nk, LANES), 1) < SWA_HEAD_DIM
    vt = jnp.concatenate([vtp_ref[0], vtc_ref[0]], axis=1).astype(F32)
    vts = pltpu.roll(vt, SWA_HEAD_DIM, 0)
    row_lo = lax.broadcasted_iota(jnp.int32, (LANES, nk), 0) < SWA_HEAD_DIM
    for g in range(SWA_KV_HEADS):
        k_own, k_swp = (kf, ks) if g == 0 else (ks, kf)
        v_own, v_swp = (vt, vts) if g == 0 else (vts, vt)
        k_e = (jnp.where(lane_lo, k_own, 0.0).astype(BF16), jnp.where(lane_lo, 0.0, k_swp).astype(BF16))
        v_e = (jnp.where(row_lo, v_own, 0.0).astype(BF16), jnp.where(row_lo, 0.0, v_swp).astype(BF16))
        q4 = jnp.concatenate([q_ref[:, (g * npair + pp) * LANES:(g * npair + pp + 1) * LANES]
                              for pp in range(npair)], axis=0)
        o2 = None
        for e in range(2):
            sink = jnp.concatenate([jnp.full((1, tq), sink_ref[g * rep + 2 * pp + e], F32)
                                    for pp in range(npair)], axis=1)
            s = _dot_nt(k_e[e], q4) + bias
            m = jnp.maximum(jnp.max(s, axis=0, keepdims=True), sink)
            ex = jnp.exp(s - m)
            inv = 1.0 / (jnp.sum(ex, axis=0, keepdims=True) + jnp.exp(sink - m))
            t = _dot(v_e[e], ex.astype(BF16)) * inv
            o2 = t if o2 is None else o2 + t
        for pp in range(npair):
            col = (g * npair + pp) * LANES
            o_ref[:, col:col + LANES] = o2[:, pp * tq:(pp + 1) * tq].T.astype(o_ref.dtype)


def _swa(sinks, c_out, vt, *, batch, seq):
    tq = SWA_WINDOW
    nq = seq // tq
    T = batch * seq
    kcol = SWA_Q_DIM // LANES
    vrow = VT_ROWS // LANES - 1
    cur = lambda b, t: b * nq + t
    prev = lambda b, t: b * nq + jnp.maximum(t - 1, 0)
    return pl.pallas_call(
        _swa_kernel, grid=(batch, nq),
        in_specs=[pl.BlockSpec(memory_space=pltpu.SMEM),
                  pl.BlockSpec((tq, SWA_Q_DIM), lambda b, t: (cur(b, t), 0)),
                  pl.BlockSpec((tq, LANES), lambda b, t: (prev(b, t), kcol)),
                  pl.BlockSpec((tq, LANES), lambda b, t: (cur(b, t), kcol)),
                  pl.BlockSpec((1, LANES, tq), lambda b, t: (b, vrow, jnp.maximum(t - 1, 0))),
                  pl.BlockSpec((1, LANES, tq), lambda b, t: (b, vrow, t))],
        out_specs=pl.BlockSpec((tq, SWA_Q_DIM), lambda b, t: (cur(b, t), 0)),
        out_shape=jax.ShapeDtypeStruct((T, SWA_Q_DIM), BF16),
        compiler_params=_params(("parallel", "arbitrary")), name="swa")(
            sinks, c_out, c_out, c_out, vt, vt)


def _merge_kernel(oa_ref, ob_ref, oc_ref, mg_ref, pa_ref, pb_ref, pc_ref, y_ref):
    D = D_MODEL
    y = _sigmoid(mg_ref[:, 0:D].astype(F32)) * _dot(oa_ref[...], pa_ref[...])
    y += _sigmoid(mg_ref[:, D:2 * D].astype(F32)) * _dot(ob_ref[...], pb_ref[...])
    y += _sigmoid(mg_ref[:, 2 * D:3 * D].astype(F32)) * _dot(oc_ref[...], pc_ref[...])
    y_ref[...] = y.astype(y_ref.dtype)


def _merge(oa, ob, oc, n_out, pa, pb, pc, *, tm):
    T = oa.shape[0]
    rowblk = lambda w: pl.BlockSpec((tm, w), lambda i: (i, 0))
    const = lambda a: pl.BlockSpec(a.shape, lambda i: (0,) * a.ndim, pipeline_mode=pl.Buffered(1))
    return pl.pallas_call(
        _merge_kernel, grid=(T // tm,),
        in_specs=[rowblk(NSA_Q_DIM), rowblk(SSM_D_INNER), rowblk(SWA_Q_DIM), rowblk(3 * D_MODEL),
                  const(pa), const(pb), const(pc)],
        out_specs=rowblk(D_MODEL),
        out_shape=jax.ShapeDtypeStruct((T, D_MODEL), BF16),
        compiler_params=_params(("parallel",)), name="merge")(oa, ob, oc, n_out, pa, pb, pc)


def _outproj_kernel(x_ref, y_ref, wo_ref, nw_ref, wrh_ref, wrl_ref, rb_ref, xo_ref, hn_ref, route_ref):
    x = x_ref[...] + _dot(y_ref[...], wo_ref[...])
    xo_ref[...] = x
    ms = jnp.mean(x * x, axis=-1, keepdims=True)
    hn = x * lax.rsqrt(ms + NORM_EPS) * nw_ref[...]
    hn_hi, hn_lo = _split_bf16(hn, 2)
    bits = pltpu.bitcast(hn_hi.astype(F32), U32)
    for c in range(ROW_CHUNKS):
        lo = lax.shift_right_logical(bits[:, c * LANES:(c + 1) * LANES], U32(16))
        hi = bits[:, (ROW_CHUNKS + c) * LANES:(ROW_CHUNKS + c + 1) * LANES] & U32(0xFFFF0000)
        hn_ref[pl.ds(c, hn.shape[0], stride=ROW_CHUNKS), :] = hi | lo
    logit = (_dot(hn_hi, wrh_ref[...]) + _dot(hn_lo, wrh_ref[...]) + _dot(hn_hi, wrl_ref[...])
             + rb_ref[...])
    tm = logit.shape[0]
    lane = lax.broadcasted_iota(jnp.int32, (tm, LANES), 1)
    big = jnp.int32(LANES)
    gl = jnp.where(lane < MOE_GROUPS, logit, -jnp.inf)
    gmax = jnp.max(gl, axis=-1, keepdims=True)
    gidx = jnp.min(jnp.where(gl == gmax, lane, big), axis=-1, keepdims=True)
    gw = 1.0 / jnp.sum(jnp.exp(gl - gmax), axis=-1, keepdims=True)
    lo = MOE_GROUPS + MOE_EXPERTS_PER_GROUP * gidx
    el = jnp.where((lane >= lo) & (lane < lo + MOE_EXPERTS_PER_GROUP), logit, -jnp.inf)
    m1 = jnp.max(el, axis=-1, keepdims=True)
    i1 = jnp.min(jnp.where(el == m1, lane, big), axis=-1, keepdims=True)
    el2 = jnp.where(lane == i1, -jnp.inf, el)
    m2 = jnp.max(el2, axis=-1, keepdims=True)
    i2 = jnp.min(jnp.where(el2 == m2, lane, big), axis=-1, keepdims=True)
    e2 = jnp.exp(m2 - m1)
    w1 = gw / (1.0 + e2)
    w2 = gw * e2 / (1.0 + e2)
    route = jnp.where(lane == 0, (i1 - MOE_GROUPS).astype(F32),
                      jnp.where(lane == 1, (i2 - MOE_GROUPS).astype(F32),
                                jnp.where(lane == 2, w1, jnp.where(lane == 3, w2, 0.0))))
    route_ref[...] = route


def _outproj(x2d, y, wo, nw, wrh, wrl, rb, *, tm):
    T, D = x2d.shape
    rowblk = lambda w: pl.BlockSpec((tm, w), lambda i: (i, 0))
    const = lambda a: pl.BlockSpec(a.shape, lambda i: (0,) * a.ndim, pipeline_mode=pl.Buffered(1))
    return pl.pallas_call(
        _outproj_kernel, grid=(T // tm,),
        in_specs=[rowblk(D), rowblk(D), const(wo), const(nw), const(wrh), const(wrl), const(rb)],
        out_specs=(rowblk(D), pl.BlockSpec((tm * ROW_CHUNKS, LANES), lambda i: (i, 0)), rowblk(LANES)),
        out_shape=(jax.ShapeDtypeStruct((T, D), F32), jax.ShapeDtypeStruct((T * ROW_CHUNKS, LANES), U32),
                   jax.ShapeDtypeStruct((T, LANES), F32)),
        compiler_params=_params(("parallel",)), name="outproj")(x2d, y, wo, nw, wrh, wrl, rb)


def _expert_kernel(te_ref, tok_ref, tok_next_ref, hn_hbm, wg_ref, wu_ref, wd_ref, y_ref,
                   xbuf, sem, wgu_scr, wd_scr):
    i = pl.program_id(0)
    tm = y_ref.shape[0]
    slot = i % 2

    def token_copy(tok, r, s):
        return pltpu.make_async_copy(
            hn_hbm.at[pl.ds(pl.multiple_of(tok * ROW_CHUNKS, ROW_CHUNKS), ROW_CHUNKS), :],
            xbuf.at[s, pl.ds(pl.multiple_of(r * ROW_CHUNKS, ROW_CHUNKS), ROW_CHUNKS), :], sem.at[s])

    def request(idx_ref, s):
        def body(j, carry):
            for p in range(2):
                r = 2 * j + p
                token_copy(idx_ref[0, 0, r], r, s).start(priority=p)
            return carry
        lax.fori_loop(0, tm // 2, body, 0, unroll=4)

    @pl.when(i == 0)
    def _():
        request(tok_ref, 0)

    @pl.when(i + 1 < pl.num_programs(0))
    def _():
        request(tok_next_ref, 1 - slot)

    @pl.when((i == 0) | (te_ref[i] != te_ref[jnp.maximum(i - 1, 0)]))
    def _():
        wgu_scr[:, :MOE_D_FF] = wg_ref[0].astype(BF16)
        wgu_scr[:, MOE_D_FF:] = wu_ref[0].astype(BF16)
        wd_scr[...] = wd_ref[0].astype(BF16)

    pltpu.make_async_copy(hn_hbm.at[pl.ds(0, tm * ROW_CHUNKS), :], xbuf.at[slot], sem.at[slot]).wait()
    words = [xbuf[slot, pl.ds(c, tm, stride=ROW_CHUNKS), :] for c in range(ROW_CHUNKS)]
    lows = [pltpu.bitcast(lax.shift_left(w, U32(16)), F32).astype(BF16) for w in words]
    highs = [pltpu.bitcast(w & U32(0xFFFF0000), F32).astype(BF16) for w in words]
    x = jnp.concatenate(lows + highs, axis=1)
    gu = _dot(x, wgu_scr[...])
    act = (_silu(gu[:, :MOE_D_FF]) * gu[:, MOE_D_FF:]).astype(BF16)
    y_ref[...] = _dot(act, wd_scr[...]).astype(y_ref.dtype)


def _experts(tile_expert, row_token, hn_rows, wg, wu, wd, *, layer):
    tm = MOE_ROW_TILE
    n_tiles = row_token.shape[0] // tm
    D = D_MODEL
    first = layer * MOE_EXPERTS
    tok3 = row_token.reshape(n_tiles, 1, tm)
    smem_tile = lambda idx: pl.BlockSpec((1, 1, tm), idx, memory_space=pltpu.SMEM)
    grid_spec = pltpu.PrefetchScalarGridSpec(
        num_scalar_prefetch=1, grid=(n_tiles,),
        in_specs=[smem_tile(lambda i, te: (i, 0, 0)),
                  smem_tile(lambda i, te: (jnp.minimum(i + 1, n_tiles - 1), 0, 0)),
                  pl.BlockSpec(memory_space=pl.ANY),
                  pl.BlockSpec((1, D, MOE_D_FF), lambda i, te: (first + te[i], 0, 0)),
                  pl.BlockSpec((1, D, MOE_D_FF), lambda i, te: (first + te[i], 0, 0)),
                  pl.BlockSpec((1, MOE_D_FF, D), lambda i, te: (first + te[i], 0, 0))],
        out_specs=pl.BlockSpec((tm, D), lambda i, te: (i, 0)),
        scratch_shapes=[pltpu.VMEM((2, tm * ROW_CHUNKS, LANES), U32), pltpu.SemaphoreType.DMA((2,)),
                        pltpu.VMEM((D, 2 * MOE_D_FF), BF16), pltpu.VMEM((MOE_D_FF, D), BF16)])
    params = pltpu.CompilerParams(dimension_semantics=("arbitrary",), vmem_limit_bytes=VMEM_LIMIT,
                                  disable_bounds_checks=True)
    return pl.pallas_call(
        _expert_kernel, grid_spec=grid_spec,
        out_shape=jax.ShapeDtypeStruct((n_tiles * tm, D), BF16),
        compiler_params=params, name="experts")(tile_expert, tok3, tok3, hn_rows, wg, wu, wd)


def _combine_kernel(x_ref, y0_ref, y1_ref, route_ref, nw_ref, *o_refs, final):
    r = route_ref[...]
    x = x_ref[...] + r[:, 2:3] * y0_ref[...].astype(F32) + r[:, 3:4] * y1_ref[...].astype(F32)
    ms = jnp.mean(x * x, axis=-1, keepdims=True)
    hn = x * lax.rsqrt(ms + NORM_EPS) * nw_ref[...]
    if final:
        o_refs[0][...] = hn
    else:
        o_refs[0][...] = x
        o_refs[1][...] = hn.astype(BF16)


def _combine(x2d, y0, y1, route, nw, *, final, tm):
    T, D = x2d.shape
    rowblk = lambda w: pl.BlockSpec((tm, w), lambda i: (i, 0))
    out_shape = [jax.ShapeDtypeStruct((T, D), F32)] + ([] if final else [jax.ShapeDtypeStruct((T, D), BF16)])
    return pl.pallas_call(
        functools.partial(_combine_kernel, final=final), grid=(T // tm,),
        in_specs=[rowblk(D), rowblk(D), rowblk(D), rowblk(LANES),
                  pl.BlockSpec((1, D), lambda i: (0, 0))],
        out_specs=tuple(rowblk(D) for _ in out_shape), out_shape=tuple(out_shape),
        compiler_params=_params(("parallel",)), name="combine")(x2d, y0, y1, route, nw)


def _rope_tables(seq):
    def tab(dim):
        inv = 1.0 / (ROPE_THETA ** (jnp.arange(0, dim, 2, dtype=F32) / dim))
        ang = jnp.arange(seq, dtype=F32)[:, None] * inv[None, :]
        return jnp.cos(ang), jnp.sin(ang)
    ca, sa = tab(NSA_HEAD_DIM)
    cc, sc = tab(SWA_HEAD_DIM)
    z = jnp.zeros_like(sc)
    tabs_a = (jnp.concatenate([ca, ca], 1), jnp.concatenate([-sa, sa], 1))
    tabs_c = (jnp.concatenate([cc] * 4, 1), jnp.concatenate([-sc, z, -sc, z], 1),
              jnp.concatenate([z, sc, z, sc], 1))
    return tabs_a, tabs_c


def _overlap_t(seq):
    nch = seq // NSA_CMP_STRIDE
    n_sel = seq // NSA_SLC_BLOCK
    cs = np.arange(nch) * NSA_CMP_STRIDE
    ce = cs + NSA_CMP_LEN - 1
    ss = np.arange(n_sel) * NSA_SLC_BLOCK
    ov = (cs[None, :] <= ss[:, None] + NSA_SLC_BLOCK - 1) & (ce[None, :] >= ss[:, None])
    ov[:, nch - 1] = False
    return jnp.asarray(ov.astype(np.float32), BF16)


def _head_expand():
    e = np.zeros((LANES, SSM_D_INNER), np.float32)
    for h in range(SSM_HEADS):
        e[h, h * SSM_HEAD_DIM:(h + 1) * SSM_HEAD_DIM] = 1.0
    return jnp.asarray(e, BF16)


def _split_w_in(w_in):
    o = np.cumsum([0, NSA_Q_DIM, 1536, 24, SSM_D_INNER, SSM_D_INNER + SSM_BC_DIM, SSM_HEADS,
                   SWA_Q_DIM, 256, 3 * D_MODEL])
    w_t = w_in.T
    seg = lambda a, b: w_t[a:b]
    nsa_q = seg(o[0], o[1])
    kv = o[1]
    cmp_kv, slc_k, slc_v = seg(kv, kv + 512), seg(kv + 512, kv + 768), seg(kv + 768, kv + 1024)
    win_k, win_v = seg(kv + 1024, kv + 1280), seg(kv + 1280, kv + 1536)
    nsa_g = seg(o[2], o[3])
    ssm_z = seg(o[3], o[4])
    ssm_xs, ssm_bc = seg(o[4], o[4] + SSM_D_INNER), seg(o[4] + SSM_D_INNER, o[5])
    ssm_dt = seg(o[5], o[6])
    swa_q = seg(o[6], o[7])
    swa_k, swa_v = seg(o[7], o[7] + 128), seg(o[7] + 128, o[8])
    merge_g = seg(o[8], o[9])
    w_a = jnp.concatenate([nsa_q, slc_k, win_k], 0).astype(BF16)
    w_c = jnp.concatenate([swa_q, swa_k], 0).astype(BF16)
    w_n = jnp.concatenate([merge_g, ssm_z, ssm_xs, ssm_bc, cmp_kv, slc_v, win_v, swa_v], 0).astype(BF16)
    pad = jnp.zeros((LANES - SSM_HEADS - 24, w_in.shape[0]), w_in.dtype)
    w_s = jnp.concatenate([ssm_dt, nsa_g, pad], 0).astype(BF16)
    return w_a, w_c, w_n, w_s


def _pad_lanes(v):
    return jnp.pad(v, (0, LANES - v.shape[0]))[None, :]


VT_ROWS = 2 * NSA_KV_HEADS * LANES + LANES
VT_SEQ_TILE = 512


def _vt_kernel(slc_ref, win_ref, swa_ref, o_ref):
    r = 0
    for ref in (slc_ref, win_ref, swa_ref):
        for c in range(ref.shape[1] // LANES):
            blk = ref[:, c * LANES:(c + 1) * LANES].astype(F32)
            o_ref[0, r:r + LANES, :] = blk.T.astype(o_ref.dtype)
            r += LANES


def _values_transposed(n_out, *, batch, seq):
    ts = VT_SEQ_TILE
    ns = seq // ts
    row = lambda b, s: b * ns + s
    return pl.pallas_call(
        _vt_kernel, grid=(batch, ns),
        in_specs=[pl.BlockSpec((ts, 256), lambda b, s: (row(b, s), N_SLCV // 256)),
                  pl.BlockSpec((ts, 256), lambda b, s: (row(b, s), N_WINV // 256)),
                  pl.BlockSpec((ts, LANES), lambda b, s: (row(b, s), N_SWAV // LANES))],
        out_specs=pl.BlockSpec((1, VT_ROWS, ts), lambda b, s: (b, 0, s)),
        out_shape=jax.ShapeDtypeStruct((batch, VT_ROWS, seq), BF16),
        compiler_params=_params(("parallel", "arbitrary")), name="values_t")(n_out, n_out, n_out)


def _rank_kernel(route_ref, rank_ref, cnt_ref, base_scr):
    tm = route_ref.shape[0]

    @pl.when(pl.program_id(0) == 0)
    def _():
        base_scr[...] = jnp.zeros(base_scr.shape, F32)

    r = route_ref[...]
    lane = lax.broadcasted_iota(jnp.int32, (tm, LANES), 1)
    lanef = lane.astype(F32)
    oh0 = jnp.where(r[:, 0:1] == lanef, 1.0, 0.0)
    oh1 = jnp.where(r[:, 1:2] == lanef, 1.0, 0.0)
    tri = (lax.broadcasted_iota(jnp.int32, (tm, tm), 0)
           >= lax.broadcasted_iota(jnp.int32, (tm, tm), 1)).astype(BF16)
    tot = base_scr[0:1, :] + _dot(tri, (oh0 + oh1).astype(BF16))
    rank0 = jnp.sum(oh0 * (tot - 1.0), axis=-1, keepdims=True)
    rank1 = jnp.sum(oh1 * (tot - 1.0), axis=-1, keepdims=True)
    rank_ref[...] = jnp.where(lane == 0, rank0, jnp.where(lane == 1, rank1, 0.0))
    base_scr[0:1, :] = tot[tm - 1:tm, :]
    cnt_ref[...] = jnp.broadcast_to(tot[tm - 1:tm, :], cnt_ref.shape)


def _rank(route, *, tm):
    T = route.shape[0]
    return pl.pallas_call(
        _rank_kernel, grid=(T // tm,),
        in_specs=[pl.BlockSpec((tm, LANES), lambda i: (i, 0))],
        out_specs=(pl.BlockSpec((tm, LANES), lambda i: (i, 0)), pl.BlockSpec((8, LANES), lambda i: (0, 0))),
        out_shape=(jax.ShapeDtypeStruct((T, LANES), F32), jax.ShapeDtypeStruct((8, LANES), F32)),
        scratch_shapes=[pltpu.VMEM((8, LANES), F32)],
        compiler_params=_params(("arbitrary",)), name="moe_rank")(route)


def _dispatch(route, n_tok):
    tm = MOE_ROW_TILE
    n_asg = n_tok * MOE_TOPK
    n_rows = n_asg + MOE_EXPERTS * tm
    rank, cnt = _rank(route, tm=512)
    counts = cnt[0, :MOE_EXPERTS].astype(jnp.int32)
    padded = ((counts + tm - 1) // tm) * tm
    pend = jnp.cumsum(padded)
    pstart = pend - padded
    start = jnp.cumsum(counts) - counts
    eid = route[:, 0:MOE_TOPK].astype(jnp.int32)
    experts = jnp.arange(MOE_EXPERTS, dtype=jnp.int32)
    pstart_tok = jnp.sum(jnp.where(eid[..., None] == experts, pstart, 0), axis=-1)
    pos = pstart_tok + rank[:, 0:MOE_TOPK].astype(jnp.int32)
    order = jnp.argsort(eid.reshape(-1), stable=True)
    tile_start = jnp.arange(n_rows // tm, dtype=jnp.int32) * tm
    tile_expert = jnp.minimum(jnp.sum((tile_start[:, None] >= pend[None, :]).astype(jnp.int32), axis=1),
                              MOE_EXPERTS - 1)
    per_row = lambda table: jnp.repeat(table[tile_expert], tm)
    k = jnp.arange(n_rows, dtype=jnp.int32) - per_row(pstart)
    valid = k < per_row(counts)
    src = jnp.clip(per_row(start) + jnp.where(valid, k, 0), 0, n_asg - 1)
    row_token = jnp.where(valid, order[src] // MOE_TOPK, 0)
    return row_token, pos, tile_expert


def kernel(x, norm_mix, norm_ffn, w_in, nsa_cmp_pos, nsa_cmp_w1, nsa_cmp_b1, nsa_cmp_w2, ssm_conv_w,
           ssm_conv_b, ssm_dt_bias, ssm_a_log, ssm_d, ssm_norm, swa_sinks, proj_nsa, proj_ssm, proj_swa,
           w_out, moe_group_router, moe_group_bias, moe_expert_router, moe_expert_bias, moe_w_gate,
           moe_w_up, moe_w_down, final_norm):
    B, S, D = x.shape
    T = B * S
    depth = w_in.shape[0]
    tm = 512
    tabs_a, tabs_c = _rope_tables(S)
    ovt = _overlap_t(S)
    eh = _head_expand()
    scale_a = jnp.concatenate([jnp.full((NSA_Q_DIM,), NSA_HEAD_DIM ** -0.5, F32),
                               jnp.ones((512,), F32)])[None, :]
    scale_c = jnp.concatenate([jnp.full((SWA_Q_DIM,), SWA_HEAD_DIM ** -0.5, F32),
                               jnp.ones((LANES,), F32)])[None, :]
    xc = x.reshape(T, D)
    hn_mix = _prenorm(xc, norm_mix[0][None, :], tm=tm)
    for l in range(depth):
        w_a, w_c, w_n, w_s = _split_w_in(w_in[l])
        qr, qp = _inproj(hn_mix, w_a, seq=S, tm=1024, tn=512, rope='a', scale=scale_a, tabs=tabs_a,
                         out_dtypes=(BF16, BF16))
        (c_out,) = _inproj(hn_mix, w_c, seq=S, tm=tm, tn=w_c.shape[0], rope='c', scale=scale_c,
                           tabs=tabs_c)
        (n_out,) = _inproj(hn_mix, w_n, seq=S, tm=1024, tn=896)
        (small,) = _inproj(hn_mix, w_s, seq=S, tm=1024, tn=LANES, out_dtypes=(F32,))

        w1r = nsa_cmp_w1[l].reshape(2, NSA_CMP_LEN, NSA_HEAD_DIM, NSA_CMP_HIDDEN).astype(BF16)
        kc, vct = _cmp_mlp(n_out, nsa_cmp_pos[l], w1r, nsa_cmp_b1[l][:, None, :],
                           nsa_cmp_w2[l, 0].astype(BF16), nsa_cmp_w2[l, 1].T.astype(BF16), batch=B, seq=S)
        vt = _values_transposed(n_out, batch=B, seq=S)
        o_a = _nsa_attn(qp, qr, small, kc, vct, ovt, vt, batch=B, seq=S)

        dexp = jnp.repeat(ssm_d[l], SSM_HEAD_DIM)[None, :]
        o_b = _ssd(n_out, small, ssm_conv_w[l], ssm_conv_b[l][None, :], _pad_lanes(ssm_dt_bias[l]),
                   _pad_lanes(ssm_a_log[l]), dexp, ssm_norm[l][None, :], eh, batch=B, seq=S)
        o_c = _swa(swa_sinks[l], c_out, vt, batch=B, seq=S)

        y = _merge(o_a, o_b, o_c, n_out, proj_nsa[l].astype(BF16), proj_ssm[l].astype(BF16),
                   proj_swa[l].astype(BF16), tm=tm)
        wr = jnp.pad(jnp.concatenate([moe_group_router[l], moe_expert_router[l]], 1),
                     ((0, 0), (0, LANES - MOE_GROUPS - MOE_EXPERTS)))
        wrh = wr.astype(BF16)
        wrl = (wr - wrh.astype(F32)).astype(BF16)
        rb = _pad_lanes(jnp.concatenate([moe_group_bias[l], moe_expert_bias[l]]))
        x_mid, hn, route = _outproj(xc, y, w_out[l].astype(BF16), norm_ffn[l][None, :], wrh, wrl, rb,
                                    tm=256)

        row_token, pos, tile_expert = _dispatch(route, T)
        ys = _experts(tile_expert, row_token, hn, moe_w_gate.reshape(depth * MOE_EXPERTS, D, MOE_D_FF),
                      moe_w_up.reshape(depth * MOE_EXPERTS, D, MOE_D_FF),
                      moe_w_down.reshape(depth * MOE_EXPERTS, MOE_D_FF, D), layer=l)
        final = l == depth - 1
        nw_next = final_norm if final else norm_mix[l + 1]
        outs = _combine(x_mid, ys[pos[:, 0]], ys[pos[:, 1]], route, nw_next[None, :], final=final, tm=tm)
        xc = outs[0]
        if not final:
            hn_mix = outs[1]
    return xc.reshape(B, S, D)
```

```python
import functools

import jax
import jax.numpy as jnp
import numpy as np
from jax import lax
from jax.experimental import pallas as pl
from jax.experimental.pallas import tpu as pltpu

F32 = jnp.float32
BF16 = jnp.bfloat16

D_MODEL = 2048
ROPE_THETA = 10000.0
NORM_EPS = 1e-6
MASK_VALUE = -1e30
FORCE_SCORE = 1e6

NSA_HEADS = 8
NSA_KV_HEADS = 2
NSA_REP = NSA_HEADS // NSA_KV_HEADS
NSA_HEAD_DIM = 128
NSA_CMP_STRIDE = 16
NSA_CMP_LEN = 32
NSA_CMP_HIDDEN = 256
NSA_SLC_BLOCK = 64
NSA_SLC_TOPK = 16
NSA_WINDOW = 512
NSA_Q_DIM = NSA_HEADS * NSA_HEAD_DIM

SSM_D_INNER = 1024
SSM_HEAD_DIM = 64
SSM_HEADS = 16
SSM_GROUPS = 2
SSM_D_STATE = 128
SSM_CONV = 4
SSM_CHUNK = 128
SSM_BC_DIM = 2 * SSM_GROUPS * SSM_D_STATE

SWA_HEADS = 16
SWA_KV_HEADS = 2
SWA_HEAD_DIM = 64
SWA_WINDOW = 128
SWA_Q_DIM = SWA_HEADS * SWA_HEAD_DIM

MOE_GROUPS = 4
MOE_EXPERTS_PER_GROUP = 8
MOE_EXPERTS = MOE_GROUPS * MOE_EXPERTS_PER_GROUP
MOE_TOPK = 2
MOE_D_FF = 512

LANES = 128
ROW_CHUNKS = D_MODEL // LANES // 2
U32 = jnp.uint32
ATT_Q_TILE = 128
SLC_K_TILE = 512
MOE_ROW_TILE = 512
VMEM_LIMIT = 56 * 1024 * 1024

N_MERGE = 0
N_Z = 6144
N_XS = 7168
N_BC = 8192
N_CMP = 8704
N_SLCV = 9216
N_WINV = 9472
N_SWAV = 9728
N_TOTAL = 9856
SMALL_GATE0 = 16


def _params(sem):
    return pltpu.CompilerParams(dimension_semantics=sem, vmem_limit_bytes=VMEM_LIMIT)


def _dot(a, b):
    return jnp.dot(a, b, preferred_element_type=F32)


def _dot_nt(a, b):
    return lax.dot_general(a, b, (((1,), (1,)), ((), ())), preferred_element_type=F32)


def _split_bf16(v, n):
    parts = []
    for _ in range(n):
        p = v.astype(BF16)
        parts.append(p)
        v = v - p.astype(F32)
    return parts


def _expand(v, e):
    hi, lo = _split_bf16(v, 2)
    return _dot(hi, e) + _dot(lo, e)


def _sigmoid(v):
    return 1.0 / (1.0 + jnp.exp(-v))


def _silu(v):
    return v * _sigmoid(v)


def _prenorm_kernel(x_ref, nw_ref, hn_ref):
    x = x_ref[...]
    ms = jnp.mean(x * x, axis=-1, keepdims=True)
    hn_ref[...] = (x * lax.rsqrt(ms + NORM_EPS) * nw_ref[...]).astype(hn_ref.dtype)


def _prenorm(x2d, nw, *, tm):
    T, D = x2d.shape
    return pl.pallas_call(
        _prenorm_kernel, grid=(T // tm,),
        in_specs=[pl.BlockSpec((tm, D), lambda i: (i, 0)), pl.BlockSpec((1, D), lambda i: (0, 0))],
        out_specs=pl.BlockSpec((tm, D), lambda i: (i, 0)),
        out_shape=jax.ShapeDtypeStruct((T, D), BF16),
        compiler_params=_params(("parallel",)), name="prenorm")(x2d, nw)


def _inproj_kernel(*refs, rope, has_scale, n_out):
    it = iter(refs)
    hn_ref, w_ref = next(it), next(it)
    cs_ref = next(it) if has_scale else None
    tabs = [next(it) for _ in range({None: 0, 'a': 2, 'c': 3}[rope])]
    outs = [next(it) for _ in range(n_out)]

    acc = _dot_nt(hn_ref[...], w_ref[...])
    if has_scale:
        acc = acc * cs_ref[...]
    if rope is None:
        outs[0][...] = acc.astype(outs[0].dtype)
        return
    if n_out == 2:
        outs[1][...] = acc.astype(outs[1].dtype)
    for c in range(acc.shape[1] // LANES):
        a = acc[:, c * LANES:(c + 1) * LANES]
        if rope == 'a':
            r = a * tabs[0][...] + pltpu.roll(a, 64, 1) * tabs[1][...]
        else:
            r = (a * tabs[0][...] + pltpu.roll(a, 96, 1) * tabs[1][...]
                 + pltpu.roll(a, 32, 1) * tabs[2][...])
        outs[0][:, c * LANES:(c + 1) * LANES] = r.astype(outs[0].dtype)


def _inproj(hn, w, *, seq, tm, tn, rope=None, scale=None, tabs=(), out_dtypes=(BF16,)):
    T, D = hn.shape
    N = w.shape[0]
    nrow = seq // tm
    in_specs = [pl.BlockSpec((tm, D), lambda i, j: (i, 0)),
                pl.BlockSpec((tn, D), lambda i, j: (j, 0))]
    args = [hn, w]
    if scale is not None:
        in_specs.append(pl.BlockSpec((1, tn), lambda i, j: (0, j)))
        args.append(scale)
    for t in tabs:
        in_specs.append(pl.BlockSpec((tm, LANES), lambda i, j: (i % nrow, 0)))
        args.append(t)
    out_shape = tuple(jax.ShapeDtypeStruct((T, N), dt) for dt in out_dtypes)
    out_specs = tuple(pl.BlockSpec((tm, tn), lambda i, j: (i, j)) for _ in out_dtypes)
    kern = functools.partial(_inproj_kernel, rope=rope, has_scale=scale is not None,
                             n_out=len(out_dtypes))
    return pl.pallas_call(
        kern, grid=(T // tm, N // tn), in_specs=in_specs, out_specs=out_specs, out_shape=out_shape,
        compiler_params=_params(("parallel", "arbitrary")), name="inproj_" + str(rope))(*args)


def _cmp_mlp_kernel(k_ref, v_ref, pos_ref, w1_ref, b1_ref, w2k_ref, w2vt_ref, kc_ref, vct_ref, f32_scr):
    nch = kc_ref.shape[2]
    for c, src in enumerate((k_ref, v_ref)):
        f32_scr[...] = src[...].astype(F32)
        first = jnp.zeros((nch, NSA_CMP_HIDDEN), F32)
        second = jnp.zeros((nch, NSA_CMP_HIDDEN), F32)
        for t in range(NSA_CMP_STRIDE):
            xt = f32_scr[pl.ds(t, nch, stride=NSA_CMP_STRIDE), :]
            first += _dot((xt + pos_ref[c, t:t + 1, :]).astype(BF16), w1_ref[c, t])
            t2 = NSA_CMP_STRIDE + t
            second += _dot((xt + pos_ref[c, t2:t2 + 1, :]).astype(BF16), w1_ref[c, t2])
        hid = _silu(first + pltpu.roll(second, nch - 1, 0) + b1_ref[c]).astype(BF16)
        if c == 0:
            kc_ref[0, 0] = _dot(hid, w2k_ref[...]).astype(kc_ref.dtype)
        else:
            vct_ref[0, 0] = _dot_nt(w2vt_ref[...], hid).astype(vct_ref.dtype)


def _cmp_mlp(n_out, pos, w1r, b1, w2k, w2vt, *, batch, seq):
    nch = seq // NSA_CMP_STRIDE
    cb0 = N_CMP // LANES
    full = lambda a: pl.BlockSpec(a.shape, lambda b, g: (0,) * a.ndim)
    return pl.pallas_call(
        _cmp_mlp_kernel, grid=(batch, NSA_KV_HEADS),
        in_specs=[pl.BlockSpec((seq, LANES), lambda b, g: (b, cb0 + g)),
                  pl.BlockSpec((seq, LANES), lambda b, g: (b, cb0 + NSA_KV_HEADS + g)),
                  full(pos), full(w1r), full(b1), full(w2k), full(w2vt)],
        out_specs=(pl.BlockSpec((1, 1, nch, LANES), lambda b, g: (b, g, 0, 0)),
                   pl.BlockSpec((1, 1, LANES, nch), lambda b, g: (b, g, 0, 0))),
        out_shape=(jax.ShapeDtypeStruct((batch, NSA_KV_HEADS, nch, LANES), BF16),
                   jax.ShapeDtypeStruct((batch, NSA_KV_HEADS, LANES, nch), BF16)),
        scratch_shapes=[pltpu.VMEM((seq, LANES), F32)],
        compiler_params=_params(("parallel", "arbitrary")), name="nsa_cmp_mlp")(
            n_out, n_out, pos, w1r, b1, w2k, w2vt)


def _nsa_attn_kernel(qp_ref, qr_ref, small_ref, kc_ref, vct_ref, ovt_ref, slck_ref, slcvt_ref, wink_ref,
                     winvt_ref, o_ref, q4_scr, sel_scr, m_scr, l_scr, acc_scr, part_scr, *, topk):
    tq = qp_ref.shape[0]
    nch = kc_ref.shape[2]
    n_sel = ovt_ref.shape[0]
    rep = NSA_REP
    tk = SLC_K_TILE
    wlen = NSA_WINDOW + tq
    qt = pl.program_id(1)
    start = qt * tq

    def tile4(v):
        return jnp.concatenate([v] * rep, axis=1)

    def qpos(rows):
        return start + lax.broadcasted_iota(jnp.int32, (rows, tq), 1)

    def sub(rows):
        return lax.broadcasted_iota(jnp.int32, (rows, tq), 0)

    for g in range(NSA_KV_HEADS):
        for r in range(rep):
            h = g * rep + r
            q4_scr[0, g, r * tq:(r + 1) * tq, :] = qp_ref[:, h * LANES:(h + 1) * LANES]
            q4_scr[1, g, r * tq:(r + 1) * tq, :] = qr_ref[:, h * LANES:(h + 1) * LANES]

    gates = _sigmoid(small_ref[...]).T

    def gate4(g, br):
        return jnp.concatenate([gates[SMALL_GATE0 + 3 * (g * rep + r) + br:SMALL_GATE0 + 3 * (g * rep + r) + br + 1, :]
                                for r in range(rep)], axis=1)

    vis = sub(nch) * NSA_CMP_STRIDE + (NSA_CMP_LEN - 1) <= qpos(nch)
    vis_bias = tile4(jnp.where(vis, 0.0, MASK_VALUE))
    vis_one = tile4(jnp.where(vis, 1.0, 0.0))
    jj = sub(n_sel)
    pos_t = qpos(n_sel)
    qblk = pos_t // NSA_SLC_BLOCK
    causal_blk = jj * NSA_SLC_BLOCK <= pos_t
    forced = (jj == 0) | (jj == qblk) | (jj == qblk - 1)
    for g in range(NSA_KV_HEADS):
        s = _dot_nt(kc_ref[0, g], q4_scr[0, g]) + vis_bias
        e = jnp.exp(s - jnp.max(s, axis=0, keepdims=True))
        p = e * (1.0 / jnp.sum(e, axis=0, keepdims=True)) * vis_one
        part_scr[g] = gate4(g, 0) * _dot(vct_ref[0, g], p.astype(BF16))
        psum = p[:, 0:tq]
        for r in range(1, rep):
            psum = psum + p[:, r * tq:(r + 1) * tq]
        imp = sum(_dot(ovt_ref[...], part) for part in _split_bf16(psum, 3))
        imp = jnp.where(causal_blk, imp, MASK_VALUE)
        imp = jnp.where(forced, FORCE_SCORE, imp)
        groups = [imp[r0:r0 + 8, :] for r0 in range(0, n_sel, 8)]
        ranks = [jnp.zeros((8, tq), F32) for _ in groups]
        for jp in range(n_sel):
            row = imp[jp:jp + 1, :]
            for gi, blk in enumerate(groups):
                r0 = gi * 8
                if r0 > jp:
                    beats = jnp.where(row >= blk, 1.0, 0.0)
                elif r0 + 8 <= jp + 1:
                    beats = jnp.where(row > blk, 1.0, 0.0)
                else:
                    beats = jnp.where(jj[r0:r0 + 8, :] > jp, jnp.where(row >= blk, 1.0, 0.0),
                                      jnp.where(row > blk, 1.0, 0.0))
                ranks[gi] = ranks[gi] + beats
        rank = jnp.concatenate(ranks, axis=0)
        sel_scr[g * n_sel:(g + 1) * n_sel, :] = jnp.where(rank < topk, 1.0, 0.0)

    wstart = pl.multiple_of(jnp.maximum(qt - NSA_WINDOW // tq, 0) * tq, tq)
    diff = qpos(wlen) - (wstart + sub(wlen))
    wbias = tile4(jnp.where(diff >= 0, jnp.where(diff < NSA_WINDOW, 0.0, MASK_VALUE), MASK_VALUE))
    for g in range(NSA_KV_HEADS):
        s = _dot_nt(wink_ref[pl.ds(wstart, wlen), g * LANES:(g + 1) * LANES], q4_scr[1, g]) + wbias
        e = jnp.exp(s - jnp.max(s, axis=0, keepdims=True))
        o = _dot(winvt_ref[0, g * LANES:(g + 1) * LANES, pl.ds(wstart, wlen)], e.astype(BF16))
        part_scr[g] = part_scr[g] + gate4(g, 2) * (o * (1.0 / jnp.sum(e, axis=0, keepdims=True)))

    m_scr[...] = jnp.full(m_scr.shape, MASK_VALUE, F32)
    l_scr[...] = jnp.zeros(l_scr.shape, F32)
    acc_scr[...] = jnp.zeros(acc_scr.shape, F32)
    blocks_per_tile = tk // NSA_SLC_BLOCK

    def body(kt, carry):
        base = pl.multiple_of(kt * tk, tk)
        causal_bias = jnp.where(base + sub(tk) <= qpos(tk), 0.0, MASK_VALUE)
        for g in range(NSA_KV_HEADS):
            s = _dot_nt(slck_ref[pl.ds(base, tk), g * LANES:(g + 1) * LANES], q4_scr[1, g])
            picked = jnp.concatenate(
                [jnp.broadcast_to(sel_scr[pl.ds(g * n_sel + kt * blocks_per_tile + i, 1), :],
                                  (NSA_SLC_BLOCK, tq)) for i in range(blocks_per_tile)], axis=0)
            s = s + tile4(jnp.where(picked > 0.5, causal_bias, MASK_VALUE))
            m_prev = m_scr[g]
            m_next = jnp.maximum(m_prev, jnp.max(s, axis=0, keepdims=True))
            alpha = jnp.exp(m_prev - m_next)
            p = jnp.exp(s - m_next)
            l_scr[g] = alpha * l_scr[g] + jnp.sum(p, axis=0, keepdims=True)
            acc_scr[g] = acc_scr[g] * alpha + _dot(slcvt_ref[0, g * LANES:(g + 1) * LANES, pl.ds(base, tk)],
                                                   p.astype(BF16))
            m_scr[g] = m_next
        return carry

    lax.fori_loop(0, (start + tq + tk - 1) // tk, body, 0)

    for g in range(NSA_KV_HEADS):
        o = part_scr[g] + gate4(g, 1) * (acc_scr[g] * (1.0 / l_scr[g]))
        for r in range(rep):
            h = g * rep + r
            o_ref[:, h * LANES:(h + 1) * LANES] = o[:, r * tq:(r + 1) * tq].T.astype(o_ref.dtype)


def _nsa_attn(qp, qr, small, kc, vct, ovt, vt, *, batch, seq):
    tq = ATT_Q_TILE
    assert seq >= NSA_WINDOW + tq and seq % SLC_K_TILE == 0
    nq = seq // tq
    nch = seq // NSA_CMP_STRIDE
    n_sel = seq // NSA_SLC_BLOCK
    T = batch * seq
    row = lambda b, t: (b * nq + t, 0)
    per_b4 = lambda b, t: (b, 0, 0, 0)
    kcol = NSA_Q_DIM // 256
    rows = tq * NSA_REP
    kern = functools.partial(_nsa_attn_kernel, topk=min(NSA_SLC_TOPK, n_sel))
    return pl.pallas_call(
        kern, grid=(batch, nq),
        in_specs=[pl.BlockSpec((tq, NSA_Q_DIM), row), pl.BlockSpec((tq, NSA_Q_DIM), row),
                  pl.BlockSpec((tq, LANES), row),
                  pl.BlockSpec((1, NSA_KV_HEADS, nch, LANES), per_b4),
                  pl.BlockSpec((1, NSA_KV_HEADS, LANES, nch), per_b4),
                  pl.BlockSpec((n_sel, nch), lambda b, t: (0, 0)),
                  pl.BlockSpec((seq, 256), lambda b, t: (b, kcol)),
                  pl.BlockSpec((1, NSA_KV_HEADS * LANES, seq), lambda b, t: (b, 0, 0)),
                  pl.BlockSpec((seq, 256), lambda b, t: (b, kcol + 1)),
                  pl.BlockSpec((1, NSA_KV_HEADS * LANES, seq), lambda b, t: (b, 1, 0))],
        out_specs=pl.BlockSpec((tq, NSA_Q_DIM), row),
        out_shape=jax.ShapeDtypeStruct((T, NSA_Q_DIM), BF16),
        scratch_shapes=[pltpu.VMEM((2, NSA_KV_HEADS, rows, LANES), BF16),
                        pltpu.VMEM((NSA_KV_HEADS * n_sel, tq), F32),
                        pltpu.VMEM((NSA_KV_HEADS, 1, rows), F32), pltpu.VMEM((NSA_KV_HEADS, 1, rows), F32),
                        pltpu.VMEM((NSA_KV_HEADS, LANES, rows), F32),
                        pltpu.VMEM((NSA_KV_HEADS, LANES, rows), F32)],
        compiler_params=_params(("parallel", "arbitrary")), name="nsa_attn")(
            qp, qr, small, kc, vct, ovt, qr, vt, qr, vt)


def _ssd_kernel(xs_ref, bc_ref, z_ref, small_ref, cw_ref, cb_ref, dtb_ref, alog_ref, dexp_ref,
                nw_ref, eh_ref, o_ref, xs_scr, bc_scr, h_scr):
    L = SSM_CHUNK
    P2 = SSM_D_INNER // SSM_GROUPS
    N = SSM_D_STATE
    c = pl.program_id(1)

    @pl.when(c == 0)
    def _():
        xs_scr[0:8, :] = jnp.zeros((8, SSM_D_INNER), F32)
        bc_scr[0:8, :] = jnp.zeros((8, SSM_BC_DIM), F32)
        h_scr[...] = jnp.zeros(h_scr.shape, F32)

    xs_scr[8:8 + L, :] = xs_ref[...].astype(F32)
    bc_scr[8:8 + L, :] = bc_ref[...].astype(F32)

    def conv(scr, col0, width):
        acc = jnp.zeros((L, width), F32) + cb_ref[:, col0:col0 + width]
        for k in range(SSM_CONV):
            acc += scr[8 - (SSM_CONV - 1) + k:8 - (SSM_CONV - 1) + k + L, :] * cw_ref[k:k + 1, col0:col0 + width]
        return _silu(acc)

    xs = conv(xs_scr, 0, SSM_D_INNER)
    bcm = conv(bc_scr, SSM_D_INNER, SSM_BC_DIM)
    xs_scr[0:8, :] = xs_scr[L:L + 8, :]
    bc_scr[0:8, :] = bc_scr[L:L + 8, :]

    lane = lax.broadcasted_iota(jnp.int32, (L, LANES), 1)
    pre = small_ref[...] + dtb_ref[...]
    dt = jnp.maximum(pre, 0.0) + jnp.log(1.0 + jnp.exp(-jnp.abs(pre)))
    dt = jnp.where(lane < SSM_HEADS, dt, 0.0)
    a = dt * (-jnp.exp(alog_ref[...]))
    tri = (lax.broadcasted_iota(jnp.int32, (L, L), 0)
           >= lax.broadcasted_iota(jnp.int32, (L, L), 1))
    tri_b = tri.astype(BF16)
    a_cs = sum(_dot(tri_b, part) for part in _split_bf16(a, 3))
    a_cs_t = a_cs.T
    a_end = a_cs[L - 1:L, :]
    eh = eh_ref[...]
    dt_x = _expand(dt, eh)
    ea_x = _expand(jnp.exp(a_cs), eh)
    de_x = _expand(jnp.exp(a_end - a_cs), eh)
    cd_x = _expand(jnp.broadcast_to(jnp.exp(a_end), (8, LANES)), eh)[0:1]

    X = xs * dt_x
    Xb = X.astype(BF16)
    Xe = (X * de_x).astype(BF16)
    lane_lo = lax.broadcasted_iota(jnp.int32, (L, LANES), 1) < SSM_HEAD_DIM
    y_parts = []
    for g in range(SSM_GROUPS):
        Bg = bcm[:, g * N:(g + 1) * N]
        Cg = bcm[:, (SSM_GROUPS + g) * N:(SSM_GROUPS + g + 1) * N]
        Cb = Cg.astype(BF16)
        cbm = _dot_nt(Cb, Bg.astype(BF16))
        hT = h_scr[g]
        y_off = _dot(Cb, hT.astype(BF16)) * ea_x[:, g * P2:(g + 1) * P2]
        y_dg = []
        for pp in range(P2 // LANES):
            h0 = g * (SSM_HEADS // SSM_GROUPS) + 2 * pp
            acc = None
            for e in range(2):
                h = h0 + e
                seg = a_cs[:, h:h + 1] - a_cs_t[h:h + 1, :]
                dec = jnp.exp(jnp.where(tri, seg, MASK_VALUE))
                m = (cbm * dec).astype(BF16)
                col = g * P2 + pp * LANES
                xh = jnp.where(lane_lo if e == 0 else jnp.logical_not(lane_lo), Xb[:, col:col + LANES],
                               jnp.zeros((), BF16))
                t = _dot(m, xh)
                acc = t if acc is None else acc + t
            y_dg.append(acc)
        y_parts.append(jnp.concatenate(y_dg, axis=1) + y_off)
        st = _dot(Bg.T.astype(BF16), Xe[:, g * P2:(g + 1) * P2])
        h_scr[g] = hT * cd_x[:, g * P2:(g + 1) * P2] + st
    y = jnp.concatenate(y_parts, axis=1) + xs * dexp_ref[...]
    y = y * _silu(z_ref[...].astype(F32))
    outs = []
    for g in range(SSM_GROUPS):
        yg = y[:, g * P2:(g + 1) * P2]
        outs.append(yg * lax.rsqrt(jnp.mean(yg * yg, axis=-1, keepdims=True) + NORM_EPS))
    o_ref[...] = (jnp.concatenate(outs, axis=1) * nw_ref[...]).astype(o_ref.dtype)


def _ssd(n_out, small, cw, cb, dtb, alog, dexp, nw, eh, *, batch, seq):
    L = SSM_CHUNK
    nc = seq // L
    T = batch * seq
    row = lambda b, c: b * nc + c
    full = lambda shape: pl.BlockSpec(shape, lambda b, c: (0,) * len(shape))
    return pl.pallas_call(
        _ssd_kernel, grid=(batch, nc),
        in_specs=[pl.BlockSpec((L, SSM_D_INNER), lambda b, c: (row(b, c), N_XS // SSM_D_INNER)),
                  pl.BlockSpec((L, SSM_BC_DIM), lambda b, c: (row(b, c), N_BC // SSM_BC_DIM)),
                  pl.BlockSpec((L, SSM_D_INNER), lambda b, c: (row(b, c), N_Z // SSM_D_INNER)),
                  pl.BlockSpec((L, LANES), lambda b, c: (row(b, c), 0)),
                  full(cw.shape), full(cb.shape), full(dtb.shape), full(alog.shape),
                  full(dexp.shape), full(nw.shape), full(eh.shape)],
        out_specs=pl.BlockSpec((L, SSM_D_INNER), lambda b, c: (row(b, c), 0)),
        out_shape=jax.ShapeDtypeStruct((T, SSM_D_INNER), BF16),
        scratch_shapes=[pltpu.VMEM((L + 8, SSM_D_INNER), F32), pltpu.VMEM((L + 8, SSM_BC_DIM), F32),
                        pltpu.VMEM((SSM_GROUPS, SSM_D_STATE, SSM_D_INNER // SSM_GROUPS), F32)],
        compiler_params=_params(("parallel", "arbitrary")), name="ssd")(
            n_out, n_out, n_out, small, cw, cb, dtb, alog, dexp, nw, eh)


def _swa_kernel(sink_ref, q_ref, kp_ref, kc_ref, vtp_ref, vtc_ref, o_ref):
    tq = q_ref.shape[0]
    qt = pl.program_id(1)
    rep = SWA_HEADS // SWA_KV_HEADS
    npair = rep // 2
    nk = 2 * tq
    c_i = lax.broadcasted_iota(jnp.int32, (nk, tq), 0)
    diff = tq + lax.broadcasted_iota(jnp.int32, (nk, tq), 1) - c_i
    first_key = jnp.where(qt > 0, 0, tq)
    bias = jnp.where(diff >= 0, jnp.where(diff < SWA_WINDOW, jnp.where(c_i >= first_key, 0.0, MASK_VALUE),
                                          MASK_VALUE), MASK_VALUE)
    bias = jnp.concatenate([bias] * npair, axis=1)
    kf = jnp.concatenate([kp_ref[...], kc_ref[...]], axis=0).astype(F32)
    ks = pltpu.roll(kf, SWA_HEAD_DIM, 1)
    lane_lo = lax.broadcasted_iota(jnp.int32, (nk, LANES), 1) < SWA_HEAD_DIM
    vt = jnp.concatenate([vtp_ref[0], vtc_ref[0]], axis=1).astype(F32)
    vts = pltpu.roll(vt, SWA_HEAD_DIM, 0)
    row_lo = lax.broadcasted_iota(jnp.int32, (LANES, nk), 0) < SWA_HEAD_DIM
    for g in range(SWA_KV_HEADS):
        k_own, k_swp = (kf, ks) if g == 0 else (ks, kf)
        v_own, v_swp = (vt, vts) if g == 0 else (vts, vt)
        k_e = (jnp.where(lane_lo, k_own, 0.0).astype(BF16), jnp.where(lane_lo, 0.0, k_swp).astype(BF16))
        v_e = (jnp.where(row_lo, v_own, 0.0).astype(BF16), jnp.where(row_lo, 0.0, v_swp).astype(BF16))
        q4 = jnp.concatenate([q_ref[:, (g * npair + pp) * LANES:(g * npair + pp + 1) * LANES]
                              for pp in range(npair)], axis=0)
        o2 = None
        for e in range(2):
            sink = jnp.concatenate([jnp.full((1, tq), sink_ref[g * rep + 2 * pp + e], F32)
                                    for pp in range(npair)], axis=1)
            s = _dot_nt(k_e[e], q4) + bias
            m = jnp.maximum(jnp.max(s, axis=0, keepdims=True), sink)
            ex = jnp.exp(s - m)
            inv = 1.0 / (jnp.sum(ex, axis=0, keepdims=True) + jnp.exp(sink - m))
            t = _dot(v_e[e], ex.astype(BF16)) * inv
            o2 = t if o2 is None else o2 + t
        for pp in range(npair):
            col = (g * npair + pp) * LANES
            o_ref[:, col:col + LANES] = o2[:, pp * tq:(pp + 1) * tq].T.astype(o_ref.dtype)


def _swa(sinks, c_out, vt, *, batch, seq):
    tq = SWA_WINDOW
    nq = seq // tq
    T = batch * seq
    kcol = SWA_Q_DIM // LANES
    vrow = VT_ROWS // LANES - 1
    cur = lambda b, t: b * nq + t
    prev = lambda b, t: b * nq + jnp.maximum(t - 1, 0)
    return pl.pallas_call(
        _swa_kernel, grid=(batch, nq),
        in_specs=[pl.BlockSpec(memory_space=pltpu.SMEM),
                  pl.BlockSpec((tq, SWA_Q_DIM), lambda b, t: (cur(b, t), 0)),
                  pl.BlockSpec((tq, LANES), lambda b, t: (prev(b, t), kcol)),
                  pl.BlockSpec((tq, LANES), lambda b, t: (cur(b, t), kcol)),
                  pl.BlockSpec((1, LANES, tq), lambda b, t: (b, vrow, jnp.maximum(t - 1, 0))),
                  pl.BlockSpec((1, LANES, tq), lambda b, t: (b, vrow, t))],
        out_specs=pl.BlockSpec((tq, SWA_Q_DIM), lambda b, t: (cur(b, t), 0)),
        out_shape=jax.ShapeDtypeStruct((T, SWA_Q_DIM), BF16),
        compiler_params=_params(("parallel", "arbitrary")), name="swa")(
            sinks, c_out, c_out, c_out, vt, vt)


def _merge_kernel(oa_ref, ob_ref, oc_ref, mg_ref, pa_ref, pb_ref, pc_ref, y_ref):
    D = D_MODEL
    y = _sigmoid(mg_ref[:, 0:D].astype(F32)) * _dot(oa_ref[...], pa_ref[...])
    y += _sigmoid(mg_ref[:, D:2 * D].astype(F32)) * _dot(ob_ref[...], pb_ref[...])
    y += _sigmoid(mg_ref[:, 2 * D:3 * D].astype(F32)) * _dot(oc_ref[...], pc_ref[...])
    y_ref[...] = y.astype(y_ref.dtype)


def _merge(oa, ob, oc, n_out, pa, pb, pc, *, tm):
    T = oa.shape[0]
    rowblk = lambda w: pl.BlockSpec((tm, w), lambda i: (i, 0))
    const = lambda a: pl.BlockSpec(a.shape, lambda i: (0,) * a.ndim, pipeline_mode=pl.Buffered(1))
    return pl.pallas_call(
        _merge_kernel, grid=(T // tm,),
        in_specs=[rowblk(NSA_Q_DIM), rowblk(SSM_D_INNER), rowblk(SWA_Q_DIM), rowblk(3 * D_MODEL),
                  const(pa), const(pb), const(pc)],
        out_specs=rowblk(D_MODEL),
        out_shape=jax.ShapeDtypeStruct((T, D_MODEL), BF16),
        compiler_params=_params(("parallel",)), name="merge")(oa, ob, oc, n_out, pa, pb, pc)


def _outproj_kernel(x_ref, y_ref, wo_ref, nw_ref, wrh_ref, wrl_ref, rb_ref, xo_ref, hn_ref, route_ref):
    x = x_ref[...] + _dot(y_ref[...], wo_ref[...])
    xo_ref[...] = x
    ms = jnp.mean(x * x, axis=-1, keepdims=True)
    hn = x * lax.rsqrt(ms + NORM_EPS) * nw_ref[...]
    hn_hi, hn_lo = _split_bf16(hn, 2)
    bits = pltpu.bitcast(hn_hi.astype(F32), U32)
    for c in range(ROW_CHUNKS):
        lo = lax.shift_right_logical(bits[:, c * LANES:(c + 1) * LANES], U32(16))
        hi = bits[:, (ROW_CHUNKS + c) * LANES:(ROW_CHUNKS + c + 1) * LANES] & U32(0xFFFF0000)
        hn_ref[pl.ds(c, hn.shape[0], stride=ROW_CHUNKS), :] = hi | lo
    logit = (_dot(hn_hi, wrh_ref[...]) + _dot(hn_lo, wrh_ref[...]) + _dot(hn_hi, wrl_ref[...])
             + rb_ref[...])
    tm = logit.shape[0]
    lane = lax.broadcasted_iota(jnp.int32, (tm, LANES), 1)
    big = jnp.int32(LANES)
    gl = jnp.where(lane < MOE_GROUPS, logit, -jnp.inf)
    gmax = jnp.max(gl, axis=-1, keepdims=True)
    gidx = jnp.min(jnp.where(gl == gmax, lane, big), axis=-1, keepdims=True)
    gw = 1.0 / jnp.sum(jnp.exp(gl - gmax), axis=-1, keepdims=True)
    lo = MOE_GROUPS + MOE_EXPERTS_PER_GROUP * gidx
    el = jnp.where((lane >= lo) & (lane < lo + MOE_EXPERTS_PER_GROUP), logit, -jnp.inf)
    m1 = jnp.max(el, axis=-1, keepdims=True)
    i1 = jnp.min(jnp.where(el == m1, lane, big), axis=-1, keepdims=True)
    el2 = jnp.where(lane == i1, -jnp.inf, el)
    m2 = jnp.max(el2, axis=-1, keepdims=True)
    i2 = jnp.min(jnp.where(el2 == m2, lane, big), axis=-1, keepdims=True)
    e2 = jnp.exp(m2 - m1)
    w1 = gw / (1.0 + e2)
    w2 = gw * e2 / (1.0 + e2)
    route = jnp.where(lane == 0, (i1 - MOE_GROUPS).astype(F32),
                      jnp.where(lane == 1, (i2 - MOE_GROUPS).astype(F32),
                                jnp.where(lane == 2, w1, jnp.where(lane == 3, w2, 0.0))))
    route_ref[...] = route


def _outproj(x2d, y, wo, nw, wrh, wrl, rb, *, tm):
    T, D = x2d.shape
    rowblk = lambda w: pl.BlockSpec((tm, w), lambda i: (i, 0))
    const = lambda a: pl.BlockSpec(a.shape, lambda i: (0,) * a.ndim, pipeline_mode=pl.Buffered(1))
    return pl.pallas_call(
        _outproj_kernel, grid=(T // tm,),
        in_specs=[rowblk(D), rowblk(D), const(wo), const(nw), const(wrh), const(wrl), const(rb)],
        out_specs=(rowblk(D), pl.BlockSpec((tm * ROW_CHUNKS, LANES), lambda i: (i, 0)), rowblk(LANES)),
        out_shape=(jax.ShapeDtypeStruct((T, D), F32), jax.ShapeDtypeStruct((T * ROW_CHUNKS, LANES), U32),
                   jax.ShapeDtypeStruct((T, LANES), F32)),
        compiler_params=_params(("parallel",)), name="outproj")(x2d, y, wo, nw, wrh, wrl, rb)


def _expert_kernel(te_ref, tok_ref, tok_next_ref, hn_hbm, wg_ref, wu_ref, wd_ref, y_ref,
                   xbuf, sem, wgu_scr, wd_scr):
    i = pl.program_id(0)
    tm = y_ref.shape[0]
    slot = i % 2

    def token_copy(tok, r, s):
        return pltpu.make_async_copy(
            hn_hbm.at[pl.ds(pl.multiple_of(tok * ROW_CHUNKS, ROW_CHUNKS), ROW_CHUNKS), :],
            xbuf.at[s, pl.ds(pl.multiple_of(r * ROW_CHUNKS, ROW_CHUNKS), ROW_CHUNKS), :], sem.at[s])

    def request(idx_ref, s):
        def body(j, carry):
            for p in range(2):
                r = 2 * j + p
                token_copy(idx_ref[0, 0, r], r, s).start(priority=p)
            return carry
        lax.fori_loop(0, tm // 2, body, 0, unroll=4)

    n_used = te_ref[pl.num_programs(0)]

    @pl.when(i == 0)
    def _():
        request(tok_ref, 0)

    @pl.when(i + 1 < n_used)
    def _():
        request(tok_next_ref, 1 - slot)

    @pl.when(i >= n_used)
    def _():
        y_ref[...] = jnp.zeros(y_ref.shape, y_ref.dtype)

    @pl.when(i < n_used)
    def _():
        @pl.when((i == 0) | (te_ref[i] != te_ref[jnp.maximum(i - 1, 0)]))
        def _():
            wgu_scr[:, :MOE_D_FF] = wg_ref[0].astype(BF16)
            wgu_scr[:, MOE_D_FF:] = wu_ref[0].astype(BF16)
            wd_scr[...] = wd_ref[0].astype(BF16)

        pltpu.make_async_copy(hn_hbm.at[pl.ds(0, tm * ROW_CHUNKS), :], xbuf.at[slot], sem.at[slot]).wait()
        words = [xbuf[slot, pl.ds(c, tm, stride=ROW_CHUNKS), :] for c in range(ROW_CHUNKS)]
        lows = [pltpu.bitcast(lax.shift_left(w, U32(16)), F32).astype(BF16) for w in words]
        highs = [pltpu.bitcast(w & U32(0xFFFF0000), F32).astype(BF16) for w in words]
        x = jnp.concatenate(lows + highs, axis=1)
        gu = _dot(x, wgu_scr[...])
        act = (_silu(gu[:, :MOE_D_FF]) * gu[:, MOE_D_FF:]).astype(BF16)
        y_ref[...] = _dot(act, wd_scr[...]).astype(y_ref.dtype)


def _experts(tile_expert, row_token, hn_rows, wg, wu, wd, *, layer):
    tm = MOE_ROW_TILE
    n_tiles = row_token.shape[0] // tm
    assert tile_expert.shape == (n_tiles + 1,)
    D = D_MODEL
    first = layer * MOE_EXPERTS
    tok3 = row_token.reshape(n_tiles, 1, tm)
    smem_tile = lambda idx: pl.BlockSpec((1, 1, tm), idx, memory_space=pltpu.SMEM)
    grid_spec = pltpu.PrefetchScalarGridSpec(
        num_scalar_prefetch=1, grid=(n_tiles,),
        in_specs=[smem_tile(lambda i, te: (i, 0, 0)),
                  smem_tile(lambda i, te: (jnp.minimum(i + 1, n_tiles - 1), 0, 0)),
                  pl.BlockSpec(memory_space=pl.ANY),
                  pl.BlockSpec((1, D, MOE_D_FF), lambda i, te: (first + te[i], 0, 0)),
                  pl.BlockSpec((1, D, MOE_D_FF), lambda i, te: (first + te[i], 0, 0)),
                  pl.BlockSpec((1, MOE_D_FF, D), lambda i, te: (first + te[i], 0, 0))],
        out_specs=pl.BlockSpec((tm, D), lambda i, te: (i, 0)),
        scratch_shapes=[pltpu.VMEM((2, tm * ROW_CHUNKS, LANES), U32), pltpu.SemaphoreType.DMA((2,)),
                        pltpu.VMEM((D, 2 * MOE_D_FF), BF16), pltpu.VMEM((MOE_D_FF, D), BF16)])
    params = pltpu.CompilerParams(dimension_semantics=("arbitrary",), vmem_limit_bytes=VMEM_LIMIT,
                                  disable_bounds_checks=True)
    return pl.pallas_call(
        _expert_kernel, grid_spec=grid_spec,
        out_shape=jax.ShapeDtypeStruct((n_tiles * tm, D), BF16),
        compiler_params=params, name="experts")(tile_expert, tok3, tok3, hn_rows, wg, wu, wd)


def _combine_kernel(x_ref, y0_ref, y1_ref, route_ref, nw_ref, *o_refs, final):
    r = route_ref[...]
    x = x_ref[...] + r[:, 2:3] * y0_ref[...].astype(F32) + r[:, 3:4] * y1_ref[...].astype(F32)
    ms = jnp.mean(x * x, axis=-1, keepdims=True)
    hn = x * lax.rsqrt(ms + NORM_EPS) * nw_ref[...]
    if final:
        o_refs[0][...] = hn
    else:
        o_refs[0][...] = x
        o_refs[1][...] = hn.astype(BF16)


def _combine(x2d, y0, y1, route, nw, *, final, tm):
    T, D = x2d.shape
    rowblk = lambda w: pl.BlockSpec((tm, w), lambda i: (i, 0))
    out_shape = [jax.ShapeDtypeStruct((T, D), F32)] + ([] if final else [jax.ShapeDtypeStruct((T, D), BF16)])
    return pl.pallas_call(
        functools.partial(_combine_kernel, final=final), grid=(T // tm,),
        in_specs=[rowblk(D), rowblk(D), rowblk(D), rowblk(LANES),
                  pl.BlockSpec((1, D), lambda i: (0, 0))],
        out_specs=tuple(rowblk(D) for _ in out_shape), out_shape=tuple(out_shape),
        compiler_params=_params(("parallel",)), name="combine")(x2d, y0, y1, route, nw)


def _rope_tables(seq):
    def tab(dim):
        inv = 1.0 / (ROPE_THETA ** (jnp.arange(0, dim, 2, dtype=F32) / dim))
        ang = jnp.arange(seq, dtype=F32)[:, None] * inv[None, :]
        return jnp.cos(ang), jnp.sin(ang)
    ca, sa = tab(NSA_HEAD_DIM)
    cc, sc = tab(SWA_HEAD_DIM)
    z = jnp.zeros_like(sc)
    tabs_a = (jnp.concatenate([ca, ca], 1), jnp.concatenate([-sa, sa], 1))
    tabs_c = (jnp.concatenate([cc] * 4, 1), jnp.concatenate([-sc, z, -sc, z], 1),
              jnp.concatenate([z, sc, z, sc], 1))
    return tabs_a, tabs_c


def _overlap_t(seq):
    nch = seq // NSA_CMP_STRIDE
    n_sel = seq // NSA_SLC_BLOCK
    cs = np.arange(nch) * NSA_CMP_STRIDE
    ce = cs + NSA_CMP_LEN - 1
    ss = np.arange(n_sel) * NSA_SLC_BLOCK
    ov = (cs[None, :] <= ss[:, None] + NSA_SLC_BLOCK - 1) & (ce[None, :] >= ss[:, None])
    ov[:, nch - 1] = False
    return jnp.asarray(ov.astype(np.float32), BF16)


def _head_expand():
    e = np.zeros((LANES, SSM_D_INNER), np.float32)
    for h in range(SSM_HEADS):
        e[h, h * SSM_HEAD_DIM:(h + 1) * SSM_HEAD_DIM] = 1.0
    return jnp.asarray(e, BF16)


def _split_w_in(w_in):
    o = np.cumsum([0, NSA_Q_DIM, 1536, 24, SSM_D_INNER, SSM_D_INNER + SSM_BC_DIM, SSM_HEADS,
                   SWA_Q_DIM, 256, 3 * D_MODEL])
    w_t = w_in.T
    seg = lambda a, b: w_t[a:b]
    nsa_q = seg(o[0], o[1])
    kv = o[1]
    cmp_kv, slc_k, slc_v = seg(kv, kv + 512), seg(kv + 512, kv + 768), seg(kv + 768, kv + 1024)
    win_k, win_v = seg(kv + 1024, kv + 1280), seg(kv + 1280, kv + 1536)
    nsa_g = seg(o[2], o[3])
    ssm_z = seg(o[3], o[4])
    ssm_xs, ssm_bc = seg(o[4], o[4] + SSM_D_INNER), seg(o[4] + SSM_D_INNER, o[5])
    ssm_dt = seg(o[5], o[6])
    swa_q = seg(o[6], o[7])
    swa_k, swa_v = seg(o[7], o[7] + 128), seg(o[7] + 128, o[8])
    merge_g = seg(o[8], o[9])
    w_a = jnp.concatenate([nsa_q, slc_k, win_k], 0).astype(BF16)
    w_c = jnp.concatenate([swa_q, swa_k], 0).astype(BF16)
    w_n = jnp.concatenate([merge_g, ssm_z, ssm_xs, ssm_bc, cmp_kv, slc_v, win_v, swa_v], 0).astype(BF16)
    pad = jnp.zeros((LANES - SSM_HEADS - 24, w_in.shape[0]), w_in.dtype)
    w_s = jnp.concatenate([ssm_dt, nsa_g, pad], 0).astype(BF16)
    return w_a, w_c, w_n, w_s


def _pad_lanes(v):
    return jnp.pad(v, (0, LANES - v.shape[0]))[None, :]


VT_ROWS = 2 * NSA_KV_HEADS * LANES + LANES
VT_SEQ_TILE = 512


def _vt_kernel(slc_ref, win_ref, swa_ref, o_ref):
    r = 0
    for ref in (slc_ref, win_ref, swa_ref):
        for c in range(ref.shape[1] // LANES):
            blk = ref[:, c * LANES:(c + 1) * LANES].astype(F32)
            o_ref[0, r:r + LANES, :] = blk.T.astype(o_ref.dtype)
            r += LANES


def _values_transposed(n_out, *, batch, seq):
    ts = VT_SEQ_TILE
    ns = seq // ts
    row = lambda b, s: b * ns + s
    return pl.pallas_call(
        _vt_kernel, grid=(batch, ns),
        in_specs=[pl.BlockSpec((ts, 256), lambda b, s: (row(b, s), N_SLCV // 256)),
                  pl.BlockSpec((ts, 256), lambda b, s: (row(b, s), N_WINV // 256)),
                  pl.BlockSpec((ts, LANES), lambda b, s: (row(b, s), N_SWAV // LANES))],
        out_specs=pl.BlockSpec((1, VT_ROWS, ts), lambda b, s: (b, 0, s)),
        out_shape=jax.ShapeDtypeStruct((batch, VT_ROWS, seq), BF16),
        compiler_params=_params(("parallel", "arbitrary")), name="values_t")(n_out, n_out, n_out)


def _rank_kernel(route_ref, rank_ref, cnt_ref, base_scr):
    tm = route_ref.shape[0]

    @pl.when(pl.program_id(0) == 0)
    def _():
        base_scr[...] = jnp.zeros(base_scr.shape, F32)

    r = route_ref[...]
    lane = lax.broadcasted_iota(jnp.int32, (tm, LANES), 1)
    lanef = lane.astype(F32)
    oh0 = jnp.where(r[:, 0:1] == lanef, 1.0, 0.0)
    oh1 = jnp.where(r[:, 1:2] == lanef, 1.0, 0.0)
    tri = (lax.broadcasted_iota(jnp.int32, (tm, tm), 0)
           >= lax.broadcasted_iota(jnp.int32, (tm, tm), 1)).astype(BF16)
    tot = base_scr[0:1, :] + _dot(tri, (oh0 + oh1).astype(BF16))
    rank0 = jnp.sum(oh0 * (tot - 1.0), axis=-1, keepdims=True)
    rank1 = jnp.sum(oh1 * (tot - 1.0), axis=-1, keepdims=True)
    rank_ref[...] = jnp.where(lane == 0, rank0, jnp.where(lane == 1, rank1, 0.0))
    base_scr[0:1, :] = tot[tm - 1:tm, :]
    cnt_ref[...] = jnp.broadcast_to(tot[tm - 1:tm, :], cnt_ref.shape)


def _rank(route, *, tm):
    T = route.shape[0]
    return pl.pallas_call(
        _rank_kernel, grid=(T // tm,),
        in_specs=[pl.BlockSpec((tm, LANES), lambda i: (i, 0))],
        out_specs=(pl.BlockSpec((tm, LANES), lambda i: (i, 0)), pl.BlockSpec((8, LANES), lambda i: (0, 0))),
        out_shape=(jax.ShapeDtypeStruct((T, LANES), F32), jax.ShapeDtypeStruct((8, LANES), F32)),
        scratch_shapes=[pltpu.VMEM((8, LANES), F32)],
        compiler_params=_params(("arbitrary",)), name="moe_rank")(route)


def _dispatch(route, n_tok):
    tm = MOE_ROW_TILE
    n_asg = n_tok * MOE_TOPK
    n_rows = n_asg + MOE_EXPERTS * tm
    rank, cnt = _rank(route, tm=512)
    counts = cnt[0, :MOE_EXPERTS].astype(jnp.int32)
    padded = ((counts + tm - 1) // tm) * tm
    pend = jnp.cumsum(padded)
    pstart = pend - padded
    start = jnp.cumsum(counts) - counts
    eid = route[:, 0:MOE_TOPK].astype(jnp.int32)
    experts = jnp.arange(MOE_EXPERTS, dtype=jnp.int32)
    pstart_tok = jnp.sum(jnp.where(eid[..., None] == experts, pstart, 0), axis=-1)
    pos = pstart_tok + rank[:, 0:MOE_TOPK].astype(jnp.int32)
    order = jnp.argsort(eid.reshape(-1), stable=True)
    tile_start = jnp.arange(n_rows // tm, dtype=jnp.int32) * tm
    tile_expert = jnp.minimum(jnp.sum((tile_start[:, None] >= pend[None, :]).astype(jnp.int32), axis=1),
                              MOE_EXPERTS - 1)
    per_row = lambda table: jnp.repeat(table[tile_expert], tm)
    k = jnp.arange(n_rows, dtype=jnp.int32) - per_row(pstart)
    valid = k < per_row(counts)
    src = jnp.clip(per_row(start) + jnp.where(valid, k, 0), 0, n_asg - 1)
    row_token = jnp.where(valid, order[src] // MOE_TOPK, 0)
    n_used = (pend[MOE_EXPERTS - 1] // tm).astype(jnp.int32)
    return row_token, pos, jnp.concatenate([tile_expert.astype(jnp.int32), n_used[None]])


def kernel(x, norm_mix, norm_ffn, w_in, nsa_cmp_pos, nsa_cmp_w1, nsa_cmp_b1, nsa_cmp_w2, ssm_conv_w,
           ssm_conv_b, ssm_dt_bias, ssm_a_log, ssm_d, ssm_norm, swa_sinks, proj_nsa, proj_ssm, proj_swa,
           w_out, moe_group_router, moe_group_bias, moe_expert_router, moe_expert_bias, moe_w_gate,
           moe_w_up, moe_w_down, final_norm):
    B, S, D = x.shape
    T = B * S
    depth = w_in.shape[0]
    tm = 512
    tabs_a, tabs_c = _rope_tables(S)
    ovt = _overlap_t(S)
    eh = _head_expand()
    scale_a = jnp.concatenate([jnp.full((NSA_Q_DIM,), NSA_HEAD_DIM ** -0.5, F32),
                               jnp.ones((512,), F32)])[None, :]
    scale_c = jnp.concatenate([jnp.full((SWA_Q_DIM,), SWA_HEAD_DIM ** -0.5, F32),
                               jnp.ones((LANES,), F32)])[None, :]
    xc = x.reshape(T, D)
    hn_mix = _prenorm(xc, norm_mix[0][None, :], tm=tm)
    for l in range(depth):
        w_a, w_c, w_n, w_s = _split_w_in(w_in[l])
        qr, qp = _inproj(hn_mix, w_a, seq=S, tm=1024, tn=512, rope='a', scale=scale_a, tabs=tabs_a,
                         out_dtypes=(BF16, BF16))
        (c_out,) = _inproj(hn_mix, w_c, seq=S, tm=tm, tn=w_c.shape[0], rope='c', scale=scale_c,
                           tabs=tabs_c)
        (n_out,) = _inproj(hn_mix, w_n, seq=S, tm=1024, tn=1408)
        (small,) = _inproj(hn_mix, w_s, seq=S, tm=1024, tn=LANES, out_dtypes=(F32,))

        w1r = nsa_cmp_w1[l].reshape(2, NSA_CMP_LEN, NSA_HEAD_DIM, NSA_CMP_HIDDEN).astype(BF16)
        kc, vct = _cmp_mlp(n_out, nsa_cmp_pos[l], w1r, nsa_cmp_b1[l][:, None, :],
                           nsa_cmp_w2[l, 0].astype(BF16), nsa_cmp_w2[l, 1].T.astype(BF16), batch=B, seq=S)
        vt = _values_transposed(n_out, batch=B, seq=S)
        o_a = _nsa_attn(qp, qr, small, kc, vct, ovt, vt, batch=B, seq=S)

        dexp = jnp.repeat(ssm_d[l], SSM_HEAD_DIM)[None, :]
        o_b = _ssd(n_out, small, ssm_conv_w[l], ssm_conv_b[l][None, :], _pad_lanes(ssm_dt_bias[l]),
                   _pad_lanes(ssm_a_log[l]), dexp, ssm_norm[l][None, :], eh, batch=B, seq=S)
        o_c = _swa(swa_sinks[l], c_out, vt, batch=B, seq=S)

        y = _merge(o_a, o_b, o_c, n_out, proj_nsa[l].astype(BF16), proj_ssm[l].astype(BF16),
                   proj_swa[l].astype(BF16), tm=tm)
        wr = jnp.pad(jnp.concatenate([moe_group_router[l], moe_expert_router[l]], 1),
                     ((0, 0), (0, LANES - MOE_GROUPS - MOE_EXPERTS)))
        wrh = wr.astype(BF16)
        wrl = (wr - wrh.astype(F32)).astype(BF16)
        rb = _pad_lanes(jnp.concatenate([moe_group_bias[l], moe_expert_bias[l]]))
        x_mid, hn, route = _outproj(xc, y, w_out[l].astype(BF16), norm_ffn[l][None, :], wrh, wrl, rb,
                                    tm=256)

        row_token, pos, tile_expert = _dispatch(route, T)
        ys = _experts(tile_expert, row_token, hn, moe_w_gate.reshape(depth * MOE_EXPERTS, D, MOE_D_FF),
                      moe_w_up.reshape(depth * MOE_EXPERTS, D, MOE_D_FF),
                      moe_w_down.reshape(depth * MOE_EXPERTS, MOE_D_FF, D), layer=l)
        final = l == depth - 1
        nw_next = final_norm if final else norm_mix[l + 1]
        outs = _combine(x_mid, ys[pos[:, 0]], ys[pos[:, 1]], route, nw_next[None, :], final=final, tm=tm)
        xc = outs[0]
        if not final:
            hn_mix = outs[1]
    return xc.reshape(B, S, D)
```

```python
import functools

import jax
import jax.numpy as jnp
import numpy as np
from jax import lax
from jax.experimental import pallas as pl
from jax.experimental.pallas import tpu as pltpu

F32 = jnp.float32
BF16 = jnp.bfloat16

D_MODEL = 2048
ROPE_THETA = 10000.0
NORM_EPS = 1e-6
MASK_VALUE = -1e30
LOG2_E = 1.4426950408889634
FORCE_SCORE = 1e6

NSA_HEADS = 8
NSA_KV_HEADS = 2
NSA_REP = NSA_HEADS // NSA_KV_HEADS
NSA_HEAD_DIM = 128
NSA_CMP_STRIDE = 16
NSA_CMP_LEN = 32
NSA_CMP_HIDDEN = 256
NSA_SLC_BLOCK = 64
NSA_SLC_TOPK = 16
NSA_WINDOW = 512
NSA_Q_DIM = NSA_HEADS * NSA_HEAD_DIM

SSM_D_INNER = 1024
SSM_HEAD_DIM = 64
SSM_HEADS = 16
SSM_GROUPS = 2
SSM_D_STATE = 128
SSM_CONV = 4
SSM_CHUNK = 128
SSM_BC_DIM = 2 * SSM_GROUPS * SSM_D_STATE

SWA_HEADS = 16
SWA_KV_HEADS = 2
SWA_HEAD_DIM = 64
SWA_WINDOW = 128
SWA_Q_DIM = SWA_HEADS * SWA_HEAD_DIM

MOE_GROUPS = 4
MOE_EXPERTS_PER_GROUP = 8
MOE_EXPERTS = MOE_GROUPS * MOE_EXPERTS_PER_GROUP
MOE_TOPK = 2
MOE_D_FF = 512

LANES = 128
ROW_CHUNKS = D_MODEL // LANES // 2
U32 = jnp.uint32
ATT_Q_TILE = 128
SLC_K_TILE = 512
MOE_ROW_TILE = 512
VMEM_LIMIT = 56 * 1024 * 1024

N_MERGE = 0
N_Z = 6144
N_XS = 7168
N_BC = 8192
N_CMP = 8704
N_SLCV = 9216
N_WINV = 9472
N_SWAV = 9728
N_TOTAL = 9856
SMALL_GATE0 = 16


def _params(sem):
    return pltpu.CompilerParams(dimension_semantics=sem, vmem_limit_bytes=VMEM_LIMIT)


def _dot(a, b):
    return jnp.dot(a, b, preferred_element_type=F32)


def _dot_nt(a, b):
    return lax.dot_general(a, b, (((1,), (1,)), ((), ())), preferred_element_type=F32)


def _split_bf16(v, n):
    parts = []
    for _ in range(n):
        p = v.astype(BF16)
        parts.append(p)
        v = v - p.astype(F32)
    return parts


def _expand(v, e):
    hi, lo = _split_bf16(v, 2)
    return _dot(hi, e) + _dot(lo, e)


def _sigmoid(v):
    return 1.0 / (1.0 + jnp.exp(-v))


def _silu(v):
    return v * _sigmoid(v)


def _prenorm_kernel(x_ref, nw_ref, hn_ref):
    x = x_ref[...]
    ms = jnp.mean(x * x, axis=-1, keepdims=True)
    hn_ref[...] = (x * lax.rsqrt(ms + NORM_EPS) * nw_ref[...]).astype(hn_ref.dtype)


def _prenorm(x2d, nw, *, tm):
    T, D = x2d.shape
    return pl.pallas_call(
        _prenorm_kernel, grid=(T // tm,),
        in_specs=[pl.BlockSpec((tm, D), lambda i: (i, 0)), pl.BlockSpec((1, D), lambda i: (0, 0))],
        out_specs=pl.BlockSpec((tm, D), lambda i: (i, 0)),
        out_shape=jax.ShapeDtypeStruct((T, D), BF16),
        compiler_params=_params(("parallel",)), name="prenorm")(x2d, nw)


def _inproj_kernel(*refs, rope, has_scale, n_out):
    it = iter(refs)
    hn_ref, w_ref = next(it), next(it)
    cs_ref = next(it) if has_scale else None
    tabs = [next(it) for _ in range({None: 0, 'a': 2, 'c': 3}[rope])]
    outs = [next(it) for _ in range(n_out)]

    acc = _dot_nt(hn_ref[...], w_ref[...])
    if has_scale:
        acc = acc * cs_ref[...]
    if rope is None:
        outs[0][...] = acc.astype(outs[0].dtype)
        return
    if n_out == 2:
        outs[1][...] = acc.astype(outs[1].dtype)
    for c in range(acc.shape[1] // LANES):
        a = acc[:, c * LANES:(c + 1) * LANES]
        if rope == 'a':
            r = a * tabs[0][...] + pltpu.roll(a, 64, 1) * tabs[1][...]
        else:
            r = (a * tabs[0][...] + pltpu.roll(a, 96, 1) * tabs[1][...]
                 + pltpu.roll(a, 32, 1) * tabs[2][...])
        outs[0][:, c * LANES:(c + 1) * LANES] = r.astype(outs[0].dtype)


def _inproj(hn, w, *, seq, tm, tn, rope=None, scale=None, tabs=(), out_dtypes=(BF16,)):
    T, D = hn.shape
    N = w.shape[0]
    nrow = seq // tm
    in_specs = [pl.BlockSpec((tm, D), lambda i, j: (i, 0)),
                pl.BlockSpec((tn, D), lambda i, j: (j, 0))]
    args = [hn, w]
    if scale is not None:
        in_specs.append(pl.BlockSpec((1, tn), lambda i, j: (0, j)))
        args.append(scale)
    for t in tabs:
        in_specs.append(pl.BlockSpec((tm, LANES), lambda i, j: (i % nrow, 0)))
        args.append(t)
    out_shape = tuple(jax.ShapeDtypeStruct((T, N), dt) for dt in out_dtypes)
    out_specs = tuple(pl.BlockSpec((tm, tn), lambda i, j: (i, j)) for _ in out_dtypes)
    kern = functools.partial(_inproj_kernel, rope=rope, has_scale=scale is not None,
                             n_out=len(out_dtypes))
    return pl.pallas_call(
        kern, grid=(T // tm, N // tn), in_specs=in_specs, out_specs=out_specs, out_shape=out_shape,
        compiler_params=_params(("parallel", "arbitrary")), name="inproj_" + str(rope))(*args)


def _cmp_mlp_kernel(k_ref, v_ref, pos_ref, w1_ref, b1_ref, w2k_ref, w2vt_ref, kc_ref, vct_ref, f32_scr):
    nch = kc_ref.shape[2]
    for c, src in enumerate((k_ref, v_ref)):
        f32_scr[...] = src[...].astype(F32)
        first = jnp.zeros((nch, NSA_CMP_HIDDEN), F32)
        second = jnp.zeros((nch, NSA_CMP_HIDDEN), F32)
        for t in range(NSA_CMP_STRIDE):
            xt = f32_scr[pl.ds(t, nch, stride=NSA_CMP_STRIDE), :]
            first += _dot((xt + pos_ref[c, t:t + 1, :]).astype(BF16), w1_ref[c, t])
            t2 = NSA_CMP_STRIDE + t
            second += _dot((xt + pos_ref[c, t2:t2 + 1, :]).astype(BF16), w1_ref[c, t2])
        hid = _silu(first + pltpu.roll(second, nch - 1, 0) + b1_ref[c]).astype(BF16)
        if c == 0:
            kc_ref[0, 0] = _dot(hid, w2k_ref[...]).astype(kc_ref.dtype)
        else:
            vct_ref[0, 0] = _dot_nt(w2vt_ref[...], hid).astype(vct_ref.dtype)


def _cmp_mlp(n_out, pos, w1r, b1, w2k, w2vt, *, batch, seq):
    nch = seq // NSA_CMP_STRIDE
    cb0 = N_CMP // LANES
    full = lambda a: pl.BlockSpec(a.shape, lambda b, g: (0,) * a.ndim)
    return pl.pallas_call(
        _cmp_mlp_kernel, grid=(batch, NSA_KV_HEADS),
        in_specs=[pl.BlockSpec((seq, LANES), lambda b, g: (b, cb0 + g)),
                  pl.BlockSpec((seq, LANES), lambda b, g: (b, cb0 + NSA_KV_HEADS + g)),
                  full(pos), full(w1r), full(b1), full(w2k), full(w2vt)],
        out_specs=(pl.BlockSpec((1, 1, nch, LANES), lambda b, g: (b, g, 0, 0)),
                   pl.BlockSpec((1, 1, LANES, nch), lambda b, g: (b, g, 0, 0))),
        out_shape=(jax.ShapeDtypeStruct((batch, NSA_KV_HEADS, nch, LANES), BF16),
                   jax.ShapeDtypeStruct((batch, NSA_KV_HEADS, LANES, nch), BF16)),
        scratch_shapes=[pltpu.VMEM((seq, LANES), F32)],
        compiler_params=_params(("parallel", "arbitrary")), name="nsa_cmp_mlp")(
            n_out, n_out, pos, w1r, b1, w2k, w2vt)


def _nsa_attn_kernel(qp_ref, qr_ref, small_ref, kc_ref, vct_ref, ovt_ref, slck_ref, slcvt_ref, wink_ref,
                     winvt_ref, o_ref, q4_scr, sel_scr, m_scr, l_scr, acc_scr, part_scr, *, topk):
    tq = qp_ref.shape[0]
    nch = kc_ref.shape[2]
    n_sel = ovt_ref.shape[0]
    rep = NSA_REP
    tk = SLC_K_TILE
    wlen = NSA_WINDOW + tq
    qt = pl.program_id(1)
    start = qt * tq

    def tile4(v):
        return jnp.concatenate([v] * rep, axis=1)

    def qpos(rows):
        return start + lax.broadcasted_iota(jnp.int32, (rows, tq), 1)

    def sub(rows):
        return lax.broadcasted_iota(jnp.int32, (rows, tq), 0)

    for g in range(NSA_KV_HEADS):
        for r in range(rep):
            h = g * rep + r
            q4_scr[0, g, r * tq:(r + 1) * tq, :] = qp_ref[:, h * LANES:(h + 1) * LANES]
            q4_scr[1, g, r * tq:(r + 1) * tq, :] = qr_ref[:, h * LANES:(h + 1) * LANES]

    gates = _sigmoid(small_ref[...]).T

    def gate4(g, br):
        return jnp.concatenate([gates[SMALL_GATE0 + 3 * (g * rep + r) + br:SMALL_GATE0 + 3 * (g * rep + r) + br + 1, :]
                                for r in range(rep)], axis=1)

    vis = sub(nch) * NSA_CMP_STRIDE + (NSA_CMP_LEN - 1) <= qpos(nch)
    vis_bias = tile4(jnp.where(vis, 0.0, MASK_VALUE))
    vis_one = tile4(jnp.where(vis, 1.0, 0.0))
    jj = sub(n_sel)
    pos_t = qpos(n_sel)
    qblk = pos_t // NSA_SLC_BLOCK
    causal_blk = jj * NSA_SLC_BLOCK <= pos_t
    forced = (jj == 0) | (jj == qblk) | (jj == qblk - 1)
    for g in range(NSA_KV_HEADS):
        s = _dot_nt(kc_ref[0, g], q4_scr[0, g]) + vis_bias
        e = jnp.exp2(s - jnp.max(s, axis=0, keepdims=True))
        p = e * (1.0 / jnp.sum(e, axis=0, keepdims=True)) * vis_one
        part_scr[g] = gate4(g, 0) * _dot(vct_ref[0, g], p.astype(BF16))
        psum = p[:, 0:tq]
        for r in range(1, rep):
            psum = psum + p[:, r * tq:(r + 1) * tq]
        imp = sum(_dot(ovt_ref[...], part) for part in _split_bf16(psum, 3))
        imp = jnp.where(causal_blk, imp, MASK_VALUE)
        imp = jnp.where(forced, FORCE_SCORE, imp)
        groups = [imp[r0:r0 + 8, :] for r0 in range(0, n_sel, 8)]
        ranks = [jnp.zeros((8, tq), F32) for _ in groups]
        for jp in range(n_sel):
            row = imp[jp:jp + 1, :]
            for gi, blk in enumerate(groups):
                r0 = gi * 8
                if r0 > jp:
                    beats = jnp.where(row >= blk, 1.0, 0.0)
                elif r0 + 8 <= jp + 1:
                    beats = jnp.where(row > blk, 1.0, 0.0)
                else:
                    beats = jnp.where(jj[r0:r0 + 8, :] > jp, jnp.where(row >= blk, 1.0, 0.0),
                                      jnp.where(row > blk, 1.0, 0.0))
                ranks[gi] = ranks[gi] + beats
        rank = jnp.concatenate(ranks, axis=0)
        sel_scr[g * n_sel:(g + 1) * n_sel, :] = jnp.where(rank < topk, 1.0, 0.0)

    wstart = pl.multiple_of(jnp.maximum(qt - NSA_WINDOW // tq, 0) * tq, tq)
    diff = qpos(wlen) - (wstart + sub(wlen))
    wbias = tile4(jnp.where(diff >= 0, jnp.where(diff < NSA_WINDOW, 0.0, MASK_VALUE), MASK_VALUE))
    for g in range(NSA_KV_HEADS):
        s = _dot_nt(wink_ref[pl.ds(wstart, wlen), g * LANES:(g + 1) * LANES], q4_scr[1, g]) + wbias
        e = jnp.exp2(s - jnp.max(s, axis=0, keepdims=True))
        o = _dot(winvt_ref[0, g * LANES:(g + 1) * LANES, pl.ds(wstart, wlen)], e.astype(BF16))
        part_scr[g] = part_scr[g] + gate4(g, 2) * (o * (1.0 / jnp.sum(e, axis=0, keepdims=True)))

    m_scr[...] = jnp.full(m_scr.shape, MASK_VALUE, F32)
    l_scr[...] = jnp.zeros(l_scr.shape, F32)
    acc_scr[...] = jnp.zeros(acc_scr.shape, F32)
    blocks_per_tile = tk // NSA_SLC_BLOCK

    def body(kt, carry):
        base = pl.multiple_of(kt * tk, tk)
        causal_bias = jnp.where(base + sub(tk) <= qpos(tk), 0.0, MASK_VALUE)
        for g in range(NSA_KV_HEADS):
            s = _dot_nt(slck_ref[pl.ds(base, tk), g * LANES:(g + 1) * LANES], q4_scr[1, g])
            picked = jnp.concatenate(
                [jnp.broadcast_to(sel_scr[pl.ds(g * n_sel + kt * blocks_per_tile + i, 1), :],
                                  (NSA_SLC_BLOCK, tq)) for i in range(blocks_per_tile)], axis=0)
            s = s + tile4(jnp.where(picked > 0.5, causal_bias, MASK_VALUE))
            m_prev = m_scr[g]
            m_next = jnp.maximum(m_prev, jnp.max(s, axis=0, keepdims=True))
            alpha = jnp.exp2(m_prev - m_next)
            p = jnp.exp2(s - m_next)
            l_scr[g] = alpha * l_scr[g] + jnp.sum(p, axis=0, keepdims=True)
            acc_scr[g] = acc_scr[g] * alpha + _dot(slcvt_ref[0, g * LANES:(g + 1) * LANES, pl.ds(base, tk)],
                                                   p.astype(BF16))
            m_scr[g] = m_next
        return carry

    lax.fori_loop(0, (start + tq + tk - 1) // tk, body, 0)

    for g in range(NSA_KV_HEADS):
        o = part_scr[g] + gate4(g, 1) * (acc_scr[g] * (1.0 / l_scr[g]))
        for r in range(rep):
            h = g * rep + r
            o_ref[:, h * LANES:(h + 1) * LANES] = o[:, r * tq:(r + 1) * tq].T.astype(o_ref.dtype)


def _nsa_attn(qp, qr, small, kc, vct, ovt, vt, *, batch, seq):
    tq = ATT_Q_TILE
    assert seq >= NSA_WINDOW + tq and seq % SLC_K_TILE == 0
    nq = seq // tq
    nch = seq // NSA_CMP_STRIDE
    n_sel = seq // NSA_SLC_BLOCK
    T = batch * seq
    row = lambda b, t: (b * nq + t, 0)
    per_b4 = lambda b, t: (b, 0, 0, 0)
    kcol = NSA_Q_DIM // 256
    rows = tq * NSA_REP
    kern = functools.partial(_nsa_attn_kernel, topk=min(NSA_SLC_TOPK, n_sel))
    return pl.pallas_call(
        kern, grid=(batch, nq),
        in_specs=[pl.BlockSpec((tq, NSA_Q_DIM), row), pl.BlockSpec((tq, NSA_Q_DIM), row),
                  pl.BlockSpec((tq, LANES), row),
                  pl.BlockSpec((1, NSA_KV_HEADS, nch, LANES), per_b4),
                  pl.BlockSpec((1, NSA_KV_HEADS, LANES, nch), per_b4),
                  pl.BlockSpec((n_sel, nch), lambda b, t: (0, 0)),
                  pl.BlockSpec((seq, 256), lambda b, t: (b, kcol)),
                  pl.BlockSpec((1, NSA_KV_HEADS * LANES, seq), lambda b, t: (b, 0, 0)),
                  pl.BlockSpec((seq, 256), lambda b, t: (b, kcol + 1)),
                  pl.BlockSpec((1, NSA_KV_HEADS * LANES, seq), lambda b, t: (b, 1, 0))],
        out_specs=pl.BlockSpec((tq, NSA_Q_DIM), row),
        out_shape=jax.ShapeDtypeStruct((T, NSA_Q_DIM), BF16),
        scratch_shapes=[pltpu.VMEM((2, NSA_KV_HEADS, rows, LANES), BF16),
                        pltpu.VMEM((NSA_KV_HEADS * n_sel, tq), F32),
                        pltpu.VMEM((NSA_KV_HEADS, 1, rows), F32), pltpu.VMEM((NSA_KV_HEADS, 1, rows), F32),
                        pltpu.VMEM((NSA_KV_HEADS, LANES, rows), F32),
                        pltpu.VMEM((NSA_KV_HEADS, LANES, rows), F32)],
        compiler_params=_params(("parallel", "arbitrary")), name="nsa_attn")(
            qp, qr, small, kc, vct, ovt, qr, vt, qr, vt)


def _ssd_kernel(xs_ref, bc_ref, z_ref, small_ref, cw_ref, cb_ref, dtb_ref, alog_ref, dexp_ref,
                nw_ref, eh_ref, o_ref, xs_scr, bc_scr, h_scr):
    L = SSM_CHUNK
    P2 = SSM_D_INNER // SSM_GROUPS
    N = SSM_D_STATE
    c = pl.program_id(1)

    @pl.when(c == 0)
    def _():
        xs_scr[0:8, :] = jnp.zeros((8, SSM_D_INNER), F32)
        bc_scr[0:8, :] = jnp.zeros((8, SSM_BC_DIM), F32)
        h_scr[...] = jnp.zeros(h_scr.shape, F32)

    xs_scr[8:8 + L, :] = xs_ref[...].astype(F32)
    bc_scr[8:8 + L, :] = bc_ref[...].astype(F32)

    def conv(scr, col0, width):
        acc = jnp.zeros((L, width), F32) + cb_ref[:, col0:col0 + width]
        for k in range(SSM_CONV):
            acc += scr[8 - (SSM_CONV - 1) + k:8 - (SSM_CONV - 1) + k + L, :] * cw_ref[k:k + 1, col0:col0 + width]
        return _silu(acc)

    xs = conv(xs_scr, 0, SSM_D_INNER)
    bcm = conv(bc_scr, SSM_D_INNER, SSM_BC_DIM)
    xs_scr[0:8, :] = xs_scr[L:L + 8, :]
    bc_scr[0:8, :] = bc_scr[L:L + 8, :]

    lane = lax.broadcasted_iota(jnp.int32, (L, LANES), 1)
    pre = small_ref[...] + dtb_ref[...]
    dt = jnp.maximum(pre, 0.0) + jnp.log(1.0 + jnp.exp(-jnp.abs(pre)))
    dt = jnp.where(lane < SSM_HEADS, dt, 0.0)
    a = dt * (-jnp.exp(alog_ref[...]))
    tri = (lax.broadcasted_iota(jnp.int32, (L, L), 0)
           >= lax.broadcasted_iota(jnp.int32, (L, L), 1))
    tri_b = tri.astype(BF16)
    a_cs = sum(_dot(tri_b, part) for part in _split_bf16(a, 3))
    a_cs_t = a_cs.T
    a_end = a_cs[L - 1:L, :]
    eh = eh_ref[...]
    dt_x = _expand(dt, eh)
    ea_x = _expand(jnp.exp(a_cs), eh)
    de_x = _expand(jnp.exp(a_end - a_cs), eh)
    cd_x = _expand(jnp.broadcast_to(jnp.exp(a_end), (8, LANES)), eh)[0:1]

    X = xs * dt_x
    Xb = X.astype(BF16)
    Xe = (X * de_x).astype(BF16)
    lane_lo = lax.broadcasted_iota(jnp.int32, (L, LANES), 1) < SSM_HEAD_DIM
    y_parts = []
    for g in range(SSM_GROUPS):
        Bg = bcm[:, g * N:(g + 1) * N]
        Cg = bcm[:, (SSM_GROUPS + g) * N:(SSM_GROUPS + g + 1) * N]
        Cb = Cg.astype(BF16)
        cbm = _dot_nt(Cb, Bg.astype(BF16))
        hT = h_scr[g]
        y_off = _dot(Cb, hT.astype(BF16)) * ea_x[:, g * P2:(g + 1) * P2]
        y_dg = []
        for pp in range(P2 // LANES):
            h0 = g * (SSM_HEADS // SSM_GROUPS) + 2 * pp
            acc = None
            for e in range(2):
                h = h0 + e
                seg = a_cs[:, h:h + 1] - a_cs_t[h:h + 1, :]
                dec = jnp.exp(jnp.where(tri, seg, MASK_VALUE))
                m = (cbm * dec).astype(BF16)
                col = g * P2 + pp * LANES
                xh = jnp.where(lane_lo if e == 0 else jnp.logical_not(lane_lo), Xb[:, col:col + LANES],
                               jnp.zeros((), BF16))
                t = _dot(m, xh)
                acc = t if acc is None else acc + t
            y_dg.append(acc)
        y_parts.append(jnp.concatenate(y_dg, axis=1) + y_off)
        st = _dot(Bg.T.astype(BF16), Xe[:, g * P2:(g + 1) * P2])
        h_scr[g] = hT * cd_x[:, g * P2:(g + 1) * P2] + st
    y = jnp.concatenate(y_parts, axis=1) + xs * dexp_ref[...]
    y = y * _silu(z_ref[...].astype(F32))
    outs = []
    for g in range(SSM_GROUPS):
        yg = y[:, g * P2:(g + 1) * P2]
        outs.append(yg * lax.rsqrt(jnp.mean(yg * yg, axis=-1, keepdims=True) + NORM_EPS))
    o_ref[...] = (jnp.concatenate(outs, axis=1) * nw_ref[...]).astype(o_ref.dtype)


def _ssd(n_out, small, cw, cb, dtb, alog, dexp, nw, eh, *, batch, seq):
    L = SSM_CHUNK
    nc = seq // L
    T = batch * seq
    row = lambda b, c: b * nc + c
    full = lambda shape: pl.BlockSpec(shape, lambda b, c: (0,) * len(shape))
    return pl.pallas_call(
        _ssd_kernel, grid=(batch, nc),
        in_specs=[pl.BlockSpec((L, SSM_D_INNER), lambda b, c: (row(b, c), N_XS // SSM_D_INNER)),
                  pl.BlockSpec((L, SSM_BC_DIM), lambda b, c: (row(b, c), N_BC // SSM_BC_DIM)),
                  pl.BlockSpec((L, SSM_D_INNER), lambda b, c: (row(b, c), N_Z // SSM_D_INNER)),
                  pl.BlockSpec((L, LANES), lambda b, c: (row(b, c), 0)),
                  full(cw.shape), full(cb.shape), full(dtb.shape), full(alog.shape),
                  full(dexp.shape), full(nw.shape), full(eh.shape)],
        out_specs=pl.BlockSpec((L, SSM_D_INNER), lambda b, c: (row(b, c), 0)),
        out_shape=jax.ShapeDtypeStruct((T, SSM_D_INNER), BF16),
        scratch_shapes=[pltpu.VMEM((L + 8, SSM_D_INNER), F32), pltpu.VMEM((L + 8, SSM_BC_DIM), F32),
                        pltpu.VMEM((SSM_GROUPS, SSM_D_STATE, SSM_D_INNER // SSM_GROUPS), F32)],
        compiler_params=_params(("parallel", "arbitrary")), name="ssd")(
            n_out, n_out, n_out, small, cw, cb, dtb, alog, dexp, nw, eh)


def _swa_kernel(sink_ref, q_ref, kp_ref, kc_ref, vtp_ref, vtc_ref, o_ref):
    tq = q_ref.shape[0]
    qt = pl.program_id(1)
    rep = SWA_HEADS // SWA_KV_HEADS
    npair = rep // 2
    nk = 2 * tq
    c_i = lax.broadcasted_iota(jnp.int32, (nk, tq), 0)
    diff = tq + lax.broadcasted_iota(jnp.int32, (nk, tq), 1) - c_i
    first_key = jnp.where(qt > 0, 0, tq)
    bias = jnp.where(diff >= 0, jnp.where(diff < SWA_WINDOW, jnp.where(c_i >= first_key, 0.0, MASK_VALUE),
                                          MASK_VALUE), MASK_VALUE)
    bias = jnp.concatenate([bias] * npair, axis=1)
    kf = jnp.concatenate([kp_ref[...], kc_ref[...]], axis=0).astype(F32)
    ks = pltpu.roll(kf, SWA_HEAD_DIM, 1)
    lane_lo = lax.broadcasted_iota(jnp.int32, (nk, LANES), 1) < SWA_HEAD_DIM
    vt = jnp.concatenate([vtp_ref[0], vtc_ref[0]], axis=1).astype(F32)
    vts = pltpu.roll(vt, SWA_HEAD_DIM, 0)
    row_lo = lax.broadcasted_iota(jnp.int32, (LANES, nk), 0) < SWA_HEAD_DIM
    for g in range(SWA_KV_HEADS):
        k_own, k_swp = (kf, ks) if g == 0 else (ks, kf)
        v_own, v_swp = (vt, vts) if g == 0 else (vts, vt)
        k_e = (jnp.where(lane_lo, k_own, 0.0).astype(BF16), jnp.where(lane_lo, 0.0, k_swp).astype(BF16))
        v_e = (jnp.where(row_lo, v_own, 0.0).astype(BF16), jnp.where(row_lo, 0.0, v_swp).astype(BF16))
        q4 = jnp.concatenate([q_ref[:, (g * npair + pp) * LANES:(g * npair + pp + 1) * LANES]
                              for pp in range(npair)], axis=0)
        o2 = None
        for e in range(2):
            sink = jnp.concatenate([jnp.full((1, tq), sink_ref[g * rep + 2 * pp + e] * LOG2_E, F32)
                                    for pp in range(npair)], axis=1)
            s = _dot_nt(k_e[e], q4) + bias
            m = jnp.maximum(jnp.max(s, axis=0, keepdims=True), sink)
            ex = jnp.exp2(s - m)
            inv = 1.0 / (jnp.sum(ex, axis=0, keepdims=True) + jnp.exp2(sink - m))
            t = _dot(v_e[e], ex.astype(BF16)) * inv
            o2 = t if o2 is None else o2 + t
        for pp in range(npair):
            col = (g * npair + pp) * LANES
            o_ref[:, col:col + LANES] = o2[:, pp * tq:(pp + 1) * tq].T.astype(o_ref.dtype)


def _swa(sinks, c_out, vt, *, batch, seq):
    tq = SWA_WINDOW
    nq = seq // tq
    T = batch * seq
    kcol = SWA_Q_DIM // LANES
    vrow = VT_ROWS // LANES - 1
    cur = lambda b, t: b * nq + t
    prev = lambda b, t: b * nq + jnp.maximum(t - 1, 0)
    return pl.pallas_call(
        _swa_kernel, grid=(batch, nq),
        in_specs=[pl.BlockSpec(memory_space=pltpu.SMEM),
                  pl.BlockSpec((tq, SWA_Q_DIM), lambda b, t: (cur(b, t), 0)),
                  pl.BlockSpec((tq, LANES), lambda b, t: (prev(b, t), kcol)),
                  pl.BlockSpec((tq, LANES), lambda b, t: (cur(b, t), kcol)),
                  pl.BlockSpec((1, LANES, tq), lambda b, t: (b, vrow, jnp.maximum(t - 1, 0))),
                  pl.BlockSpec((1, LANES, tq), lambda b, t: (b, vrow, t))],
        out_specs=pl.BlockSpec((tq, SWA_Q_DIM), lambda b, t: (cur(b, t), 0)),
        out_shape=jax.ShapeDtypeStruct((T, SWA_Q_DIM), BF16),
        compiler_params=_params(("parallel", "arbitrary")), name="swa")(
            sinks, c_out, c_out, c_out, vt, vt)


def _merge_kernel(oa_ref, ob_ref, oc_ref, mg_ref, pa_ref, pb_ref, pc_ref, y_ref):
    D = D_MODEL
    y = _sigmoid(mg_ref[:, 0:D].astype(F32)) * _dot(oa_ref[...], pa_ref[...])
    y += _sigmoid(mg_ref[:, D:2 * D].astype(F32)) * _dot(ob_ref[...], pb_ref[...])
    y += _sigmoid(mg_ref[:, 2 * D:3 * D].astype(F32)) * _dot(oc_ref[...], pc_ref[...])
    y_ref[...] = y.astype(y_ref.dtype)


def _merge(oa, ob, oc, n_out, pa, pb, pc, *, tm):
    T = oa.shape[0]
    rowblk = lambda w: pl.BlockSpec((tm, w), lambda i: (i, 0))
    const = lambda a: pl.BlockSpec(a.shape, lambda i: (0,) * a.ndim, pipeline_mode=pl.Buffered(1))
    return pl.pallas_call(
        _merge_kernel, grid=(T // tm,),
        in_specs=[rowblk(NSA_Q_DIM), rowblk(SSM_D_INNER), rowblk(SWA_Q_DIM), rowblk(3 * D_MODEL),
                  const(pa), const(pb), const(pc)],
        out_specs=rowblk(D_MODEL),
        out_shape=jax.ShapeDtypeStruct((T, D_MODEL), BF16),
        compiler_params=_params(("parallel",)), name="merge")(oa, ob, oc, n_out, pa, pb, pc)


def _outproj_kernel(x_ref, y_ref, wo_ref, nw_ref, wrh_ref, wrl_ref, rb_ref, xo_ref, hn_ref, route_ref):
    x = x_ref[...] + _dot(y_ref[...], wo_ref[...])
    xo_ref[...] = x
    ms = jnp.mean(x * x, axis=-1, keepdims=True)
    hn = x * lax.rsqrt(ms + NORM_EPS) * nw_ref[...]
    hn_hi, hn_lo = _split_bf16(hn, 2)
    bits = pltpu.bitcast(hn_hi.astype(F32), U32)
    for c in range(ROW_CHUNKS):
        lo = lax.shift_right_logical(bits[:, c * LANES:(c + 1) * LANES], U32(16))
        hi = bits[:, (ROW_CHUNKS + c) * LANES:(ROW_CHUNKS + c + 1) * LANES] & U32(0xFFFF0000)
        hn_ref[pl.ds(c, hn.shape[0], stride=ROW_CHUNKS), :] = hi | lo
    logit = (_dot(hn_hi, wrh_ref[...]) + _dot(hn_lo, wrh_ref[...]) + _dot(hn_hi, wrl_ref[...])
             + rb_ref[...])
    tm = logit.shape[0]
    lane = lax.broadcasted_iota(jnp.int32, (tm, LANES), 1)
    big = jnp.int32(LANES)
    gl = jnp.where(lane < MOE_GROUPS, logit, -jnp.inf)
    gmax = jnp.max(gl, axis=-1, keepdims=True)
    gidx = jnp.min(jnp.where(gl == gmax, lane, big), axis=-1, keepdims=True)
    gw = 1.0 / jnp.sum(jnp.exp(gl - gmax), axis=-1, keepdims=True)
    lo = MOE_GROUPS + MOE_EXPERTS_PER_GROUP * gidx
    el = jnp.where((lane >= lo) & (lane < lo + MOE_EXPERTS_PER_GROUP), logit, -jnp.inf)
    m1 = jnp.max(el, axis=-1, keepdims=True)
    i1 = jnp.min(jnp.where(el == m1, lane, big), axis=-1, keepdims=True)
    el2 = jnp.where(lane == i1, -jnp.inf, el)
    m2 = jnp.max(el2, axis=-1, keepdims=True)
    i2 = jnp.min(jnp.where(el2 == m2, lane, big), axis=-1, keepdims=True)
    e2 = jnp.exp(m2 - m1)
    w1 = gw / (1.0 + e2)
    w2 = gw * e2 / (1.0 + e2)
    route = jnp.where(lane == 0, (i1 - MOE_GROUPS).astype(F32),
                      jnp.where(lane == 1, (i2 - MOE_GROUPS).astype(F32),
                                jnp.where(lane == 2, w1, jnp.where(lane == 3, w2, 0.0))))
    route_ref[...] = route


def _outproj(x2d, y, wo, nw, wrh, wrl, rb, *, tm):
    T, D = x2d.shape
    rowblk = lambda w: pl.BlockSpec((tm, w), lambda i: (i, 0))
    const = lambda a: pl.BlockSpec(a.shape, lambda i: (0,) * a.ndim, pipeline_mode=pl.Buffered(1))
    return pl.pallas_call(
        _outproj_kernel, grid=(T // tm,),
        in_specs=[rowblk(D), rowblk(D), const(wo), const(nw), const(wrh), const(wrl), const(rb)],
        out_specs=(rowblk(D), pl.BlockSpec((tm * ROW_CHUNKS, LANES), lambda i: (i, 0)), rowblk(LANES)),
        out_shape=(jax.ShapeDtypeStruct((T, D), F32), jax.ShapeDtypeStruct((T * ROW_CHUNKS, LANES), U32),
                   jax.ShapeDtypeStruct((T, LANES), F32)),
        compiler_params=_params(("parallel",)), name="outproj")(x2d, y, wo, nw, wrh, wrl, rb)


def _expert_kernel(te_ref, tok_ref, tok_next_ref, hn_hbm, wg_ref, wu_ref, wd_ref, y_ref,
                   xbuf, sem, wgu_scr, wd_scr):
    i = pl.program_id(0)
    tm = y_ref.shape[0]
    slot = i % 2

    def token_copy(tok, r, s):
        return pltpu.make_async_copy(
            hn_hbm.at[pl.ds(pl.multiple_of(tok * ROW_CHUNKS, ROW_CHUNKS), ROW_CHUNKS), :],
            xbuf.at[s, pl.ds(pl.multiple_of(r * ROW_CHUNKS, ROW_CHUNKS), ROW_CHUNKS), :], sem.at[s])

    def request(idx_ref, s):
        def body(j, carry):
            for p in range(2):
                r = 2 * j + p
                token_copy(idx_ref[0, 0, r], r, s).start(priority=p)
            return carry
        lax.fori_loop(0, tm // 2, body, 0, unroll=4)

    n_used = te_ref[pl.num_programs(0)]

    @pl.when(i == 0)
    def _():
        request(tok_ref, 0)

    @pl.when(i + 1 < n_used)
    def _():
        request(tok_next_ref, 1 - slot)

    @pl.when(i >= n_used)
    def _():
        y_ref[...] = jnp.zeros(y_ref.shape, y_ref.dtype)

    @pl.when(i < n_used)
    def _():
        @pl.when((i == 0) | (te_ref[i] != te_ref[jnp.maximum(i - 1, 0)]))
        def _():
            wgu_scr[:, :MOE_D_FF] = wg_ref[0].astype(BF16)
            wgu_scr[:, MOE_D_FF:] = wu_ref[0].astype(BF16)
            wd_scr[...] = wd_ref[0].astype(BF16)

        pltpu.make_async_copy(hn_hbm.at[pl.ds(0, tm * ROW_CHUNKS), :], xbuf.at[slot], sem.at[slot]).wait()
        words = [xbuf[slot, pl.ds(c, tm, stride=ROW_CHUNKS), :] for c in range(ROW_CHUNKS)]
        lows = [pltpu.bitcast(lax.shift_left(w, U32(16)), F32).astype(BF16) for w in words]
        highs = [pltpu.bitcast(w & U32(0xFFFF0000), F32).astype(BF16) for w in words]
        x = jnp.concatenate(lows + highs, axis=1)
        gu = _dot(x, wgu_scr[...])
        act = (_silu(gu[:, :MOE_D_FF]) * gu[:, MOE_D_FF:]).astype(BF16)
        y_ref[...] = _dot(act, wd_scr[...]).astype(y_ref.dtype)


def _experts(tile_expert, row_token, hn_rows, wg, wu, wd, *, layer):
    tm = MOE_ROW_TILE
    n_tiles = row_token.shape[0] // tm
    assert tile_expert.shape == (n_tiles + 1,)
    D = D_MODEL
    first = layer * MOE_EXPERTS
    tok3 = row_token.reshape(n_tiles, 1, tm)
    smem_tile = lambda idx: pl.BlockSpec((1, 1, tm), idx, memory_space=pltpu.SMEM)
    grid_spec = pltpu.PrefetchScalarGridSpec(
        num_scalar_prefetch=1, grid=(n_tiles,),
        in_specs=[smem_tile(lambda i, te: (i, 0, 0)),
                  smem_tile(lambda i, te: (jnp.minimum(i + 1, n_tiles - 1), 0, 0)),
                  pl.BlockSpec(memory_space=pl.ANY),
                  pl.BlockSpec((1, D, MOE_D_FF), lambda i, te: (first + te[i], 0, 0)),
                  pl.BlockSpec((1, D, MOE_D_FF), lambda i, te: (first + te[i], 0, 0)),
                  pl.BlockSpec((1, MOE_D_FF, D), lambda i, te: (first + te[i], 0, 0))],
        out_specs=pl.BlockSpec((tm, D), lambda i, te: (i, 0)),
        scratch_shapes=[pltpu.VMEM((2, tm * ROW_CHUNKS, LANES), U32), pltpu.SemaphoreType.DMA((2,)),
                        pltpu.VMEM((D, 2 * MOE_D_FF), BF16), pltpu.VMEM((MOE_D_FF, D), BF16)])
    params = pltpu.CompilerParams(dimension_semantics=("arbitrary",), vmem_limit_bytes=VMEM_LIMIT,
                                  disable_bounds_checks=True)
    return pl.pallas_call(
        _expert_kernel, grid_spec=grid_spec,
        out_shape=jax.ShapeDtypeStruct((n_tiles * tm, D), BF16),
        compiler_params=params, name="experts")(tile_expert, tok3, tok3, hn_rows, wg, wu, wd)


def _combine_kernel(x_ref, y0_ref, y1_ref, route_ref, nw_ref, *o_refs, final):
    r = route_ref[...]
    x = x_ref[...] + r[:, 2:3] * y0_ref[...].astype(F32) + r[:, 3:4] * y1_ref[...].astype(F32)
    ms = jnp.mean(x * x, axis=-1, keepdims=True)
    hn = x * lax.rsqrt(ms + NORM_EPS) * nw_ref[...]
    if final:
        o_refs[0][...] = hn
    else:
        o_refs[0][...] = x
        o_refs[1][...] = hn.astype(BF16)


def _combine(x2d, y0, y1, route, nw, *, final, tm):
    T, D = x2d.shape
    rowblk = lambda w: pl.BlockSpec((tm, w), lambda i: (i, 0))
    out_shape = [jax.ShapeDtypeStruct((T, D), F32)] + ([] if final else [jax.ShapeDtypeStruct((T, D), BF16)])
    return pl.pallas_call(
        functools.partial(_combine_kernel, final=final), grid=(T // tm,),
        in_specs=[rowblk(D), rowblk(D), rowblk(D), rowblk(LANES),
                  pl.BlockSpec((1, D), lambda i: (0, 0))],
        out_specs=tuple(rowblk(D) for _ in out_shape), out_shape=tuple(out_shape),
        compiler_params=_params(("parallel",)), name="combine")(x2d, y0, y1, route, nw)


def _rope_tables(seq):
    def tab(dim):
        inv = 1.0 / (ROPE_THETA ** (jnp.arange(0, dim, 2, dtype=F32) / dim))
        ang = jnp.arange(seq, dtype=F32)[:, None] * inv[None, :]
        return jnp.cos(ang), jnp.sin(ang)
    ca, sa = tab(NSA_HEAD_DIM)
    cc, sc = tab(SWA_HEAD_DIM)
    z = jnp.zeros_like(sc)
    tabs_a = (jnp.concatenate([ca, ca], 1), jnp.concatenate([-sa, sa], 1))
    tabs_c = (jnp.concatenate([cc] * 4, 1), jnp.concatenate([-sc, z, -sc, z], 1),
              jnp.concatenate([z, sc, z, sc], 1))
    return tabs_a, tabs_c


def _overlap_t(seq):
    nch = seq // NSA_CMP_STRIDE
    n_sel = seq // NSA_SLC_BLOCK
    cs = np.arange(nch) * NSA_CMP_STRIDE
    ce = cs + NSA_CMP_LEN - 1
    ss = np.arange(n_sel) * NSA_SLC_BLOCK
    ov = (cs[None, :] <= ss[:, None] + NSA_SLC_BLOCK - 1) & (ce[None, :] >= ss[:, None])
    ov[:, nch - 1] = False
    return jnp.asarray(ov.astype(np.float32), BF16)


def _head_expand():
    e = np.zeros((LANES, SSM_D_INNER), np.float32)
    for h in range(SSM_HEADS):
        e[h, h * SSM_HEAD_DIM:(h + 1) * SSM_HEAD_DIM] = 1.0
    return jnp.asarray(e, BF16)


def _split_w_in(w_in):
    o = np.cumsum([0, NSA_Q_DIM, 1536, 24, SSM_D_INNER, SSM_D_INNER + SSM_BC_DIM, SSM_HEADS,
                   SWA_Q_DIM, 256, 3 * D_MODEL])
    w_t = w_in.T
    seg = lambda a, b: w_t[a:b]
    nsa_q = seg(o[0], o[1])
    kv = o[1]
    cmp_kv, slc_k, slc_v = seg(kv, kv + 512), seg(kv + 512, kv + 768), seg(kv + 768, kv + 1024)
    win_k, win_v = seg(kv + 1024, kv + 1280), seg(kv + 1280, kv + 1536)
    nsa_g = seg(o[2], o[3])
    ssm_z = seg(o[3], o[4])
    ssm_xs, ssm_bc = seg(o[4], o[4] + SSM_D_INNER), seg(o[4] + SSM_D_INNER, o[5])
    ssm_dt = seg(o[5], o[6])
    swa_q = seg(o[6], o[7])
    swa_k, swa_v = seg(o[7], o[7] + 128), seg(o[7] + 128, o[8])
    merge_g = seg(o[8], o[9])
    w_a = jnp.concatenate([nsa_q, slc_k, win_k], 0).astype(BF16)
    w_c = jnp.concatenate([swa_q, swa_k], 0).astype(BF16)
    w_n = jnp.concatenate([merge_g, ssm_z, ssm_xs, ssm_bc, cmp_kv, slc_v, win_v, swa_v], 0).astype(BF16)
    pad = jnp.zeros((LANES - SSM_HEADS - 24, w_in.shape[0]), w_in.dtype)
    w_s = jnp.concatenate([ssm_dt, nsa_g, pad], 0).astype(BF16)
    return w_a, w_c, w_n, w_s


def _pad_lanes(v):
    return jnp.pad(v, (0, LANES - v.shape[0]))[None, :]


VT_ROWS = 2 * NSA_KV_HEADS * LANES + LANES
VT_SEQ_TILE = 512


def _vt_kernel(slc_ref, win_ref, swa_ref, o_ref):
    r = 0
    for ref in (slc_ref, win_ref, swa_ref):
        for c in range(ref.shape[1] // LANES):
            blk = ref[:, c * LANES:(c + 1) * LANES].astype(F32)
            o_ref[0, r:r + LANES, :] = blk.T.astype(o_ref.dtype)
            r += LANES


def _values_transposed(n_out, *, batch, seq):
    ts = VT_SEQ_TILE
    ns = seq // ts
    row = lambda b, s: b * ns + s
    return pl.pallas_call(
        _vt_kernel, grid=(batch, ns),
        in_specs=[pl.BlockSpec((ts, 256), lambda b, s: (row(b, s), N_SLCV // 256)),
                  pl.BlockSpec((ts, 256), lambda b, s: (row(b, s), N_WINV // 256)),
                  pl.BlockSpec((ts, LANES), lambda b, s: (row(b, s), N_SWAV // LANES))],
        out_specs=pl.BlockSpec((1, VT_ROWS, ts), lambda b, s: (b, 0, s)),
        out_shape=jax.ShapeDtypeStruct((batch, VT_ROWS, seq), BF16),
        compiler_params=_params(("parallel", "arbitrary")), name="values_t")(n_out, n_out, n_out)


def _rank_kernel(route_ref, rank_ref, cnt_ref, base_scr):
    tm = route_ref.shape[0]

    @pl.when(pl.program_id(0) == 0)
    def _():
        base_scr[...] = jnp.zeros(base_scr.shape, F32)

    r = route_ref[...]
    lane = lax.broadcasted_iota(jnp.int32, (tm, LANES), 1)
    lanef = lane.astype(F32)
    oh0 = jnp.where(r[:, 0:1] == lanef, 1.0, 0.0)
    oh1 = jnp.where(r[:, 1:2] == lanef, 1.0, 0.0)
    tri = (lax.broadcasted_iota(jnp.int32, (tm, tm), 0)
           >= lax.broadcasted_iota(jnp.int32, (tm, tm), 1)).astype(BF16)
    tot = base_scr[0:1, :] + _dot(tri, (oh0 + oh1).astype(BF16))
    rank0 = jnp.sum(oh0 * (tot - 1.0), axis=-1, keepdims=True)
    rank1 = jnp.sum(oh1 * (tot - 1.0), axis=-1, keepdims=True)
    rank_ref[...] = jnp.where(lane == 0, rank0, jnp.where(lane == 1, rank1, 0.0))
    base_scr[0:1, :] = tot[tm - 1:tm, :]
    cnt_ref[...] = jnp.broadcast_to(tot[tm - 1:tm, :], cnt_ref.shape)


def _rank(route, *, tm):
    T = route.shape[0]
    return pl.pallas_call(
        _rank_kernel, grid=(T // tm,),
        in_specs=[pl.BlockSpec((tm, LANES), lambda i: (i, 0))],
        out_specs=(pl.BlockSpec((tm, LANES), lambda i: (i, 0)), pl.BlockSpec((8, LANES), lambda i: (0, 0))),
        out_shape=(jax.ShapeDtypeStruct((T, LANES), F32), jax.ShapeDtypeStruct((8, LANES), F32)),
        scratch_shapes=[pltpu.VMEM((8, LANES), F32)],
        compiler_params=_params(("arbitrary",)), name="moe_rank")(route)


def _dispatch(route, n_tok):
    tm = MOE_ROW_TILE
    n_asg = n_tok * MOE_TOPK
    n_rows = n_asg + MOE_EXPERTS * tm
    rank, cnt = _rank(route, tm=512)
    counts = cnt[0, :MOE_EXPERTS].astype(jnp.int32)
    padded = ((counts + tm - 1) // tm) * tm
    pend = jnp.cumsum(padded)
    pstart = pend - padded
    start = jnp.cumsum(counts) - counts
    eid = route[:, 0:MOE_TOPK].astype(jnp.int32)
    experts = jnp.arange(MOE_EXPERTS, dtype=jnp.int32)
    pstart_tok = jnp.sum(jnp.where(eid[..., None] == experts, pstart, 0), axis=-1)
    pos = pstart_tok + rank[:, 0:MOE_TOPK].astype(jnp.int32)
    order = jnp.argsort(eid.reshape(-1), stable=True)
    tile_start = jnp.arange(n_rows // tm, dtype=jnp.int32) * tm
    tile_expert = jnp.minimum(jnp.sum((tile_start[:, None] >= pend[None, :]).astype(jnp.int32), axis=1),
                              MOE_EXPERTS - 1)
    per_row = lambda table: jnp.repeat(table[tile_expert], tm)
    k = jnp.arange(n_rows, dtype=jnp.int32) - per_row(pstart)
    valid = k < per_row(counts)
    src = jnp.clip(per_row(start) + jnp.where(valid, k, 0), 0, n_asg - 1)
    row_token = jnp.where(valid, order[src] // MOE_TOPK, 0)
    n_used = (pend[MOE_EXPERTS - 1] // tm).astype(jnp.int32)
    return row_token, pos, jnp.concatenate([tile_expert.astype(jnp.int32), n_used[None]])


def kernel(x, norm_mix, norm_ffn, w_in, nsa_cmp_pos, nsa_cmp_w1, nsa_cmp_b1, nsa_cmp_w2, ssm_conv_w,
           ssm_conv_b, ssm_dt_bias, ssm_a_log, ssm_d, ssm_norm, swa_sinks, proj_nsa, proj_ssm, proj_swa,
           w_out, moe_group_router, moe_group_bias, moe_expert_router, moe_expert_bias, moe_w_gate,
           moe_w_up, moe_w_down, final_norm):
    B, S, D = x.shape
    T = B * S
    depth = w_in.shape[0]
    tm = 512
    tabs_a, tabs_c = _rope_tables(S)
    ovt = _overlap_t(S)
    eh = _head_expand()
    scale_a = jnp.concatenate([jnp.full((NSA_Q_DIM,), NSA_HEAD_DIM ** -0.5 * LOG2_E, F32),
                               jnp.ones((512,), F32)])[None, :]
    scale_c = jnp.concatenate([jnp.full((SWA_Q_DIM,), SWA_HEAD_DIM ** -0.5 * LOG2_E, F32),
                               jnp.ones((LANES,), F32)])[None, :]
    xc = x.reshape(T, D)
    hn_mix = _prenorm(xc, norm_mix[0][None, :], tm=tm)
    for l in range(depth):
        w_a, w_c, w_n, w_s = _split_w_in(w_in[l])
        qr, qp = _inproj(hn_mix, w_a, seq=S, tm=1024, tn=512, rope='a', scale=scale_a, tabs=tabs_a,
                         out_dtypes=(BF16, BF16))
        (c_out,) = _inproj(hn_mix, w_c, seq=S, tm=tm, tn=w_c.shape[0], rope='c', scale=scale_c,
                           tabs=tabs_c)
        (n_out,) = _inproj(hn_mix, w_n, seq=S, tm=1024, tn=1408)
        (small,) = _inproj(hn_mix, w_s, seq=S, tm=1024, tn=LANES, out_dtypes=(F32,))

        w1r = nsa_cmp_w1[l].reshape(2, NSA_CMP_LEN, NSA_HEAD_DIM, NSA_CMP_HIDDEN).astype(BF16)
        kc, vct = _cmp_mlp(n_out, nsa_cmp_pos[l], w1r, nsa_cmp_b1[l][:, None, :],
                           nsa_cmp_w2[l, 0].astype(BF16), nsa_cmp_w2[l, 1].T.astype(BF16), batch=B, seq=S)
        vt = _values_transposed(n_out, batch=B, seq=S)
        o_a = _nsa_attn(qp, qr, small, kc, vct, ovt, vt, batch=B, seq=S)

        dexp = jnp.repeat(ssm_d[l], SSM_HEAD_DIM)[None, :]
        o_b = _ssd(n_out, small, ssm_conv_w[l], ssm_conv_b[l][None, :], _pad_lanes(ssm_dt_bias[l]),
                   _pad_lanes(ssm_a_log[l]), dexp, ssm_norm[l][None, :], eh, batch=B, seq=S)
        o_c = _swa(swa_sinks[l], c_out, vt, batch=B, seq=S)

        y = _merge(o_a, o_b, o_c, n_out, proj_nsa[l].astype(BF16), proj_ssm[l].astype(BF16),
                   proj_swa[l].astype(BF16), tm=tm)
        wr = jnp.pad(jnp.concatenate([moe_group_router[l], moe_expert_router[l]], 1),
                     ((0, 0), (0, LANES - MOE_GROUPS - MOE_EXPERTS)))
        wrh = wr.astype(BF16)
        wrl = (wr - wrh.astype(F32)).astype(BF16)
        rb = _pad_lanes(jnp.concatenate([moe_group_bias[l], moe_expert_bias[l]]))
        x_mid, hn, route = _outproj(xc, y, w_out[l].astype(BF16), norm_ffn[l][None, :], wrh, wrl, rb,
                                    tm=256)

        row_token, pos, tile_expert = _dispatch(route, T)
        ys = _experts(tile_expert, row_token, hn, moe_w_gate.reshape(depth * MOE_EXPERTS, D, MOE_D_FF),
                      moe_w_up.reshape(depth * MOE_EXPERTS, D, MOE_D_FF),
                      moe_w_down.reshape(depth * MOE_EXPERTS, MOE_D_FF, D), layer=l)
        final = l == depth - 1
        nw_next = final_norm if final else norm_mix[l + 1]
        outs = _combine(x_mid, ys[pos[:, 0]], ys[pos[:, 1]], route, nw_next[None, :], final=final, tm=tm)
        xc = outs[0]
        if not final:
            hn_mix = outs[1]
    return xc.reshape(B, S, D)
```

```python
import functools

import jax
import jax.numpy as jnp
import numpy as np
from jax import lax
from jax.experimental import pallas as pl
from jax.experimental.pallas import tpu as pltpu

F32 = jnp.float32
BF16 = jnp.bfloat16

D_MODEL = 2048
ROPE_THETA = 10000.0
NORM_EPS = 1e-6
MASK_VALUE = -1e30
LOG2_E = 1.4426950408889634
FORCE_SCORE = 1e6

NSA_HEADS = 8
NSA_KV_HEADS = 2
NSA_REP = NSA_HEADS // NSA_KV_HEADS
NSA_HEAD_DIM = 128
NSA_CMP_STRIDE = 16
NSA_CMP_LEN = 32
NSA_CMP_HIDDEN = 256
NSA_SLC_BLOCK = 64
NSA_SLC_TOPK = 16
NSA_WINDOW = 512
NSA_Q_DIM = NSA_HEADS * NSA_HEAD_DIM

SSM_D_INNER = 1024
SSM_HEAD_DIM = 64
SSM_HEADS = 16
SSM_GROUPS = 2
SSM_D_STATE = 128
SSM_CONV = 4
SSM_CHUNK = 128
SSM_BC_DIM = 2 * SSM_GROUPS * SSM_D_STATE

SWA_HEADS = 16
SWA_KV_HEADS = 2
SWA_HEAD_DIM = 64
SWA_WINDOW = 128
SWA_Q_DIM = SWA_HEADS * SWA_HEAD_DIM

MOE_GROUPS = 4
MOE_EXPERTS_PER_GROUP = 8
MOE_EXPERTS = MOE_GROUPS * MOE_EXPERTS_PER_GROUP
MOE_TOPK = 2
MOE_D_FF = 512

LANES = 128
ROW_CHUNKS = D_MODEL // LANES // 2
U32 = jnp.uint32
ATT_Q_TILE = 128
SLC_K_TILE = 512
MOE_ROW_TILE = 512
VMEM_LIMIT = 56 * 1024 * 1024

N_MERGE = 0
N_Z = 6144
N_XS = 7168
N_BC = 8192
N_CMP = 8704
N_SLCV = 9216
N_WINV = 9472
N_SWAV = 9728
N_TOTAL = 9856
SMALL_GATE0 = 16


def _params(sem):
    return pltpu.CompilerParams(dimension_semantics=sem, vmem_limit_bytes=VMEM_LIMIT)


def _dot(a, b):
    return jnp.dot(a, b, preferred_element_type=F32)


def _dot_nt(a, b):
    return lax.dot_general(a, b, (((1,), (1,)), ((), ())), preferred_element_type=F32)


def _split_bf16(v, n):
    parts = []
    for _ in range(n):
        p = v.astype(BF16)
        parts.append(p)
        v = v - p.astype(F32)
    return parts


def _expand(v, e):
    hi, lo = _split_bf16(v, 2)
    return _dot(hi, e) + _dot(lo, e)


def _sigmoid(v):
    return 1.0 / (1.0 + jnp.exp(-v))


def _silu(v):
    return v * _sigmoid(v)


def _prenorm_kernel(x_ref, nw_ref, hn_ref):
    x = x_ref[...]
    ms = jnp.mean(x * x, axis=-1, keepdims=True)
    hn_ref[...] = (x * lax.rsqrt(ms + NORM_EPS) * nw_ref[...]).astype(hn_ref.dtype)


def _prenorm(x2d, nw, *, tm):
    T, D = x2d.shape
    return pl.pallas_call(
        _prenorm_kernel, grid=(T // tm,),
        in_specs=[pl.BlockSpec((tm, D), lambda i: (i, 0)), pl.BlockSpec((1, D), lambda i: (0, 0))],
        out_specs=pl.BlockSpec((tm, D), lambda i: (i, 0)),
        out_shape=jax.ShapeDtypeStruct((T, D), BF16),
        compiler_params=_params(("parallel",)), name="prenorm")(x2d, nw)


def _inproj_kernel(*refs, rope, has_scale, n_out):
    it = iter(refs)
    hn_ref, w_ref = next(it), next(it)
    cs_ref = next(it) if has_scale else None
    tabs = [next(it) for _ in range({None: 0, 'a': 2, 'c': 3}[rope])]
    outs = [next(it) for _ in range(n_out)]

    acc = _dot_nt(hn_ref[...], w_ref[...])
    if has_scale:
        acc = acc * cs_ref[...]
    if rope is None:
        outs[0][...] = acc.astype(outs[0].dtype)
        return
    if n_out == 2:
        outs[1][...] = acc.astype(outs[1].dtype)
    for c in range(acc.shape[1] // LANES):
        a = acc[:, c * LANES:(c + 1) * LANES]
        if rope == 'a':
            r = a * tabs[0][...] + pltpu.roll(a, 64, 1) * tabs[1][...]
        else:
            r = (a * tabs[0][...] + pltpu.roll(a, 96, 1) * tabs[1][...]
                 + pltpu.roll(a, 32, 1) * tabs[2][...])
        outs[0][:, c * LANES:(c + 1) * LANES] = r.astype(outs[0].dtype)


def _inproj(hn, w, *, seq, tm, tn, rope=None, scale=None, tabs=(), out_dtypes=(BF16,)):
    T, D = hn.shape
    N = w.shape[0]
    nrow = seq // tm
    in_specs = [pl.BlockSpec((tm, D), lambda i, j: (i, 0)),
                pl.BlockSpec((tn, D), lambda i, j: (j, 0))]
    args = [hn, w]
    if scale is not None:
        in_specs.append(pl.BlockSpec((1, tn), lambda i, j: (0, j)))
        args.append(scale)
    for t in tabs:
        in_specs.append(pl.BlockSpec((tm, LANES), lambda i, j: (i % nrow, 0)))
        args.append(t)
    out_shape = tuple(jax.ShapeDtypeStruct((T, N), dt) for dt in out_dtypes)
    out_specs = tuple(pl.BlockSpec((tm, tn), lambda i, j: (i, j)) for _ in out_dtypes)
    kern = functools.partial(_inproj_kernel, rope=rope, has_scale=scale is not None,
                             n_out=len(out_dtypes))
    return pl.pallas_call(
        kern, grid=(T // tm, N // tn), in_specs=in_specs, out_specs=out_specs, out_shape=out_shape,
        compiler_params=_params(("parallel", "arbitrary")), name="inproj_" + str(rope))(*args)


def _cmp_mlp_kernel(k_ref, v_ref, pos_ref, w1_ref, b1_ref, w2k_ref, w2vt_ref, kc_ref, vct_ref, f32_scr):
    nch = kc_ref.shape[2]
    for c, src in enumerate((k_ref, v_ref)):
        f32_scr[...] = src[...].astype(F32)
        first = jnp.zeros((nch, NSA_CMP_HIDDEN), F32)
        second = jnp.zeros((nch, NSA_CMP_HIDDEN), F32)
        for t in range(NSA_CMP_STRIDE):
            xt = f32_scr[pl.ds(t, nch, stride=NSA_CMP_STRIDE), :]
            first += _dot((xt + pos_ref[c, t:t + 1, :]).astype(BF16), w1_ref[c, t])
            t2 = NSA_CMP_STRIDE + t
            second += _dot((xt + pos_ref[c, t2:t2 + 1, :]).astype(BF16), w1_ref[c, t2])
        hid = _silu(first + pltpu.roll(second, nch - 1, 0) + b1_ref[c]).astype(BF16)
        if c == 0:
            kc_ref[0, 0] = _dot(hid, w2k_ref[...]).astype(kc_ref.dtype)
        else:
            vct_ref[0, 0] = _dot_nt(w2vt_ref[...], hid).astype(vct_ref.dtype)


def _cmp_mlp(n_out, pos, w1r, b1, w2k, w2vt, *, batch, seq):
    nch = seq // NSA_CMP_STRIDE
    cb0 = N_CMP // LANES
    full = lambda a: pl.BlockSpec(a.shape, lambda b, g: (0,) * a.ndim)
    return pl.pallas_call(
        _cmp_mlp_kernel, grid=(batch, NSA_KV_HEADS),
        in_specs=[pl.BlockSpec((seq, LANES), lambda b, g: (b, cb0 + g)),
                  pl.BlockSpec((seq, LANES), lambda b, g: (b, cb0 + NSA_KV_HEADS + g)),
                  full(pos), full(w1r), full(b1), full(w2k), full(w2vt)],
        out_specs=(pl.BlockSpec((1, 1, nch, LANES), lambda b, g: (b, g, 0, 0)),
                   pl.BlockSpec((1, 1, LANES, nch), lambda b, g: (b, g, 0, 0))),
        out_shape=(jax.ShapeDtypeStruct((batch, NSA_KV_HEADS, nch, LANES), BF16),
                   jax.ShapeDtypeStruct((batch, NSA_KV_HEADS, LANES, nch), BF16)),
        scratch_shapes=[pltpu.VMEM((seq, LANES), F32)],
        compiler_params=_params(("parallel", "arbitrary")), name="nsa_cmp_mlp")(
            n_out, n_out, pos, w1r, b1, w2k, w2vt)


def _nsa_attn_kernel(qp_ref, qr_ref, small_ref, kc_ref, vct_ref, ovt_ref, slck_ref, slcvt_ref, wink_ref,
                     winvt_ref, o_ref, q4_scr, sel_scr, m_scr, l_scr, acc_scr, part_scr, *, topk):
    tq = qp_ref.shape[0]
    nch = kc_ref.shape[2]
    n_sel = ovt_ref.shape[0]
    rep = NSA_REP
    tk = SLC_K_TILE
    wlen = NSA_WINDOW + tq
    qt = pl.program_id(1)
    start = qt * tq

    def tile4(v):
        return jnp.concatenate([v] * rep, axis=1)

    def qpos(rows):
        return start + lax.broadcasted_iota(jnp.int32, (rows, tq), 1)

    def sub(rows):
        return lax.broadcasted_iota(jnp.int32, (rows, tq), 0)

    for g in range(NSA_KV_HEADS):
        for r in range(rep):
            h = g * rep + r
            q4_scr[0, g, r * tq:(r + 1) * tq, :] = qp_ref[:, h * LANES:(h + 1) * LANES]
            q4_scr[1, g, r * tq:(r + 1) * tq, :] = qr_ref[:, h * LANES:(h + 1) * LANES]

    gates = _sigmoid(small_ref[...]).T

    def gate4(g, br):
        return jnp.concatenate([gates[SMALL_GATE0 + 3 * (g * rep + r) + br:SMALL_GATE0 + 3 * (g * rep + r) + br + 1, :]
                                for r in range(rep)], axis=1)

    vis = sub(nch) * NSA_CMP_STRIDE + (NSA_CMP_LEN - 1) <= qpos(nch)
    vis_bias = tile4(jnp.where(vis, 0.0, MASK_VALUE))
    vis_one = tile4(jnp.where(vis, 1.0, 0.0))
    jj = sub(n_sel)
    pos_t = qpos(n_sel)
    qblk = pos_t // NSA_SLC_BLOCK
    causal_blk = jj * NSA_SLC_BLOCK <= pos_t
    forced = (jj == 0) | (jj == qblk) | (jj == qblk - 1)
    for g in range(NSA_KV_HEADS):
        s = _dot_nt(kc_ref[0, g], q4_scr[0, g]) + vis_bias
        e = jnp.exp2(s - jnp.max(s, axis=0, keepdims=True))
        p = e * (1.0 / jnp.sum(e, axis=0, keepdims=True)) * vis_one
        part_scr[g] = gate4(g, 0) * _dot(vct_ref[0, g], p.astype(BF16))
        psum = p[:, 0:tq]
        for r in range(1, rep):
            psum = psum + p[:, r * tq:(r + 1) * tq]
        imp = sum(_dot(ovt_ref[...], part) for part in _split_bf16(psum, 3))
        imp = jnp.where(causal_blk, imp, MASK_VALUE)
        imp = jnp.where(forced, FORCE_SCORE, imp)
        groups = [imp[r0:r0 + 8, :] for r0 in range(0, n_sel, 8)]
        ranks = [jnp.zeros((8, tq), F32) for _ in groups]
        for jp in range(n_sel):
            row = imp[jp:jp + 1, :]
            for gi, blk in enumerate(groups):
                r0 = gi * 8
                if r0 > jp:
                    beats = jnp.where(row >= blk, 1.0, 0.0)
                elif r0 + 8 <= jp + 1:
                    beats = jnp.where(row > blk, 1.0, 0.0)
                else:
                    beats = jnp.where(jj[r0:r0 + 8, :] > jp, jnp.where(row >= blk, 1.0, 0.0),
                                      jnp.where(row > blk, 1.0, 0.0))
                ranks[gi] = ranks[gi] + beats
        rank = jnp.concatenate(ranks, axis=0)
        sel_scr[g * n_sel:(g + 1) * n_sel, :] = jnp.where(rank < topk, 1.0, 0.0)

    wstart = pl.multiple_of(jnp.maximum(qt - NSA_WINDOW // tq, 0) * tq, tq)
    diff = qpos(wlen) - (wstart + sub(wlen))
    wbias = tile4(jnp.where(diff >= 0, jnp.where(diff < NSA_WINDOW, 0.0, MASK_VALUE), MASK_VALUE))
    for g in range(NSA_KV_HEADS):
        s = _dot_nt(wink_ref[pl.ds(wstart, wlen), g * LANES:(g + 1) * LANES], q4_scr[1, g]) + wbias
        e = jnp.exp2(s - jnp.max(s, axis=0, keepdims=True))
        o = _dot(winvt_ref[0, g * LANES:(g + 1) * LANES, pl.ds(wstart, wlen)], e.astype(BF16))
        part_scr[g] = part_scr[g] + gate4(g, 2) * (o * (1.0 / jnp.sum(e, axis=0, keepdims=True)))

    m_scr[...] = jnp.full(m_scr.shape, MASK_VALUE, F32)
    l_scr[...] = jnp.zeros(l_scr.shape, F32)
    acc_scr[...] = jnp.zeros(acc_scr.shape, F32)
    blocks_per_tile = tk // NSA_SLC_BLOCK

    def body(kt, carry):
        base = pl.multiple_of(kt * tk, tk)
        causal_bias = jnp.where(base + sub(tk) <= qpos(tk), 0.0, MASK_VALUE)
        for g in range(NSA_KV_HEADS):
            s = _dot_nt(slck_ref[pl.ds(base, tk), g * LANES:(g + 1) * LANES], q4_scr[1, g])
            picked = jnp.concatenate(
                [jnp.broadcast_to(sel_scr[pl.ds(g * n_sel + kt * blocks_per_tile + i, 1), :],
                                  (NSA_SLC_BLOCK, tq)) for i in range(blocks_per_tile)], axis=0)
            s = s + tile4(jnp.where(picked > 0.5, causal_bias, MASK_VALUE))
            m_prev = m_scr[g]
            m_next = jnp.maximum(m_prev, jnp.max(s, axis=0, keepdims=True))
            alpha = jnp.exp2(m_prev - m_next)
            p = jnp.exp2(s - m_next)
            l_scr[g] = alpha * l_scr[g] + jnp.sum(p, axis=0, keepdims=True)
            acc_scr[g] = acc_scr[g] * alpha + _dot(slcvt_ref[0, g * LANES:(g + 1) * LANES, pl.ds(base, tk)],
                                                   p.astype(BF16))
            m_scr[g] = m_next
        return carry

    lax.fori_loop(0, (start + tq + tk - 1) // tk, body, 0)

    for g in range(NSA_KV_HEADS):
        o = part_scr[g] + gate4(g, 1) * (acc_scr[g] * (1.0 / l_scr[g]))
        for r in range(rep):
            h = g * rep + r
            o_ref[:, h * LANES:(h + 1) * LANES] = o[:, r * tq:(r + 1) * tq].T.astype(o_ref.dtype)


def _nsa_attn(qp, qr, small, kc, vct, ovt, vt, *, batch, seq):
    tq = ATT_Q_TILE
    assert seq >= NSA_WINDOW + tq and seq % SLC_K_TILE == 0
    nq = seq // tq
    nch = seq // NSA_CMP_STRIDE
    n_sel = seq // NSA_SLC_BLOCK
    T = batch * seq
    row = lambda b, t: (b * nq + t, 0)
    per_b4 = lambda b, t: (b, 0, 0, 0)
    kcol = NSA_Q_DIM // 256
    rows = tq * NSA_REP
    kern = functools.partial(_nsa_attn_kernel, topk=min(NSA_SLC_TOPK, n_sel))
    return pl.pallas_call(
        kern, grid=(batch, nq),
        in_specs=[pl.BlockSpec((tq, NSA_Q_DIM), row), pl.BlockSpec((tq, NSA_Q_DIM), row),
                  pl.BlockSpec((tq, LANES), row),
                  pl.BlockSpec((1, NSA_KV_HEADS, nch, LANES), per_b4),
                  pl.BlockSpec((1, NSA_KV_HEADS, LANES, nch), per_b4),
                  pl.BlockSpec((n_sel, nch), lambda b, t: (0, 0)),
                  pl.BlockSpec((seq, 256), lambda b, t: (b, kcol)),
                  pl.BlockSpec((1, NSA_KV_HEADS * LANES, seq), lambda b, t: (b, 0, 0)),
                  pl.BlockSpec((seq, 256), lambda b, t: (b, kcol + 1)),
                  pl.BlockSpec((1, NSA_KV_HEADS * LANES, seq), lambda b, t: (b, 1, 0))],
        out_specs=pl.BlockSpec((tq, NSA_Q_DIM), row),
        out_shape=jax.ShapeDtypeStruct((T, NSA_Q_DIM), BF16),
        scratch_shapes=[pltpu.VMEM((2, NSA_KV_HEADS, rows, LANES), BF16),
                        pltpu.VMEM((NSA_KV_HEADS * n_sel, tq), F32),
                        pltpu.VMEM((NSA_KV_HEADS, 1, rows), F32), pltpu.VMEM((NSA_KV_HEADS, 1, rows), F32),
                        pltpu.VMEM((NSA_KV_HEADS, LANES, rows), F32),
                        pltpu.VMEM((NSA_KV_HEADS, LANES, rows), F32)],
        compiler_params=_params(("parallel", "arbitrary")), name="nsa_attn")(
            qp, qr, small, kc, vct, ovt, qr, vt, qr, vt)


def _ssd_kernel(xs_ref, bc_ref, z_ref, small_ref, cw_ref, cb_ref, dtb_ref, alog_ref, dexp_ref,
                nw_ref, eh_ref, o_ref, xs_scr, bc_scr, h_scr):
    L = SSM_CHUNK
    P2 = SSM_D_INNER // SSM_GROUPS
    N = SSM_D_STATE
    c = pl.program_id(1)

    @pl.when(c == 0)
    def _():
        xs_scr[0:L, :] = jnp.zeros((L, SSM_D_INNER), BF16)
        bc_scr[0:L, :] = jnp.zeros((L, SSM_BC_DIM), BF16)
        h_scr[...] = jnp.zeros(h_scr.shape, F32)

    xs_scr[L:2 * L, :] = xs_ref[...]
    bc_scr[L:2 * L, :] = bc_ref[...]
    sel_row = lax.broadcasted_iota(jnp.int32, (L, 2 * L), 0)
    sel_col = lax.broadcasted_iota(jnp.int32, (L, 2 * L), 1)
    shifts = [(sel_col == sel_row + (L - (SSM_CONV - 1 - k))).astype(BF16) for k in range(SSM_CONV - 1)]

    def conv(scr, cur_ref, col0, width):
        full = scr[...]
        last = SSM_CONV - 1
        acc = cb_ref[:, col0:col0 + width] + cur_ref[...].astype(F32) * cw_ref[last:last + 1, col0:col0 + width]
        for k in range(last):
            acc += _dot(shifts[k], full) * cw_ref[k:k + 1, col0:col0 + width]
        return _silu(acc)

    xs = conv(xs_scr, xs_ref, 0, SSM_D_INNER)
    bcm = conv(bc_scr, bc_ref, SSM_D_INNER, SSM_BC_DIM)
    xs_scr[0:L, :] = xs_ref[...]
    bc_scr[0:L, :] = bc_ref[...]

    lane = lax.broadcasted_iota(jnp.int32, (L, LANES), 1)
    pre = small_ref[...] + dtb_ref[...]
    dt = jnp.maximum(pre, 0.0) + jnp.log(1.0 + jnp.exp(-jnp.abs(pre)))
    dt = jnp.where(lane < SSM_HEADS, dt, 0.0)
    a = dt * (-jnp.exp(alog_ref[...]))
    tri = (lax.broadcasted_iota(jnp.int32, (L, L), 0)
           >= lax.broadcasted_iota(jnp.int32, (L, L), 1))
    tri_b = tri.astype(BF16)
    a_cs = sum(_dot(tri_b, part) for part in _split_bf16(a, 3))
    a_cs_t = a_cs.T
    a_end = a_cs[L - 1:L, :]
    eh = eh_ref[...]
    dt_x = _expand(dt, eh)
    ea_x = _expand(jnp.exp(a_cs), eh)
    de_x = _expand(jnp.exp(a_end - a_cs), eh)
    cd_x = _expand(jnp.broadcast_to(jnp.exp(a_end), (8, LANES)), eh)[0:1]

    X = xs * dt_x
    Xb = X.astype(BF16)
    Xe = (X * de_x).astype(BF16)
    lane_lo = lax.broadcasted_iota(jnp.int32, (L, LANES), 1) < SSM_HEAD_DIM
    y_parts = []
    for g in range(SSM_GROUPS):
        Bg = bcm[:, g * N:(g + 1) * N]
        Cg = bcm[:, (SSM_GROUPS + g) * N:(SSM_GROUPS + g + 1) * N]
        Cb = Cg.astype(BF16)
        cbm = _dot_nt(Cb, Bg.astype(BF16))
        hT = h_scr[g]
        y_off = _dot(Cb, hT.astype(BF16)) * ea_x[:, g * P2:(g + 1) * P2]
        y_dg = []
        for pp in range(P2 // LANES):
            h0 = g * (SSM_HEADS // SSM_GROUPS) + 2 * pp
            acc = None
            for e in range(2):
                h = h0 + e
                seg = a_cs[:, h:h + 1] - a_cs_t[h:h + 1, :]
                dec = jnp.exp(jnp.where(tri, seg, MASK_VALUE))
                m = (cbm * dec).astype(BF16)
                col = g * P2 + pp * LANES
                xh = jnp.where(lane_lo if e == 0 else jnp.logical_not(lane_lo), Xb[:, col:col + LANES],
                               jnp.zeros((), BF16))
                t = _dot(m, xh)
                acc = t if acc is None else acc + t
            y_dg.append(acc)
        y_parts.append(jnp.concatenate(y_dg, axis=1) + y_off)
        st = _dot(Bg.T.astype(BF16), Xe[:, g * P2:(g + 1) * P2])
        h_scr[g] = hT * cd_x[:, g * P2:(g + 1) * P2] + st
    y = jnp.concatenate(y_parts, axis=1) + xs * dexp_ref[...]
    y = y * _silu(z_ref[...].astype(F32))
    outs = []
    for g in range(SSM_GROUPS):
        yg = y[:, g * P2:(g + 1) * P2]
        outs.append(yg * lax.rsqrt(jnp.mean(yg * yg, axis=-1, keepdims=True) + NORM_EPS))
    o_ref[...] = (jnp.concatenate(outs, axis=1) * nw_ref[...]).astype(o_ref.dtype)


def _ssd(n_out, small, cw, cb, dtb, alog, dexp, nw, eh, *, batch, seq):
    L = SSM_CHUNK
    nc = seq // L
    T = batch * seq
    row = lambda b, c: b * nc + c
    full = lambda shape: pl.BlockSpec(shape, lambda b, c: (0,) * len(shape))
    return pl.pallas_call(
        _ssd_kernel, grid=(batch, nc),
        in_specs=[pl.BlockSpec((L, SSM_D_INNER), lambda b, c: (row(b, c), N_XS // SSM_D_INNER)),
                  pl.BlockSpec((L, SSM_BC_DIM), lambda b, c: (row(b, c), N_BC // SSM_BC_DIM)),
                  pl.BlockSpec((L, SSM_D_INNER), lambda b, c: (row(b, c), N_Z // SSM_D_INNER)),
                  pl.BlockSpec((L, LANES), lambda b, c: (row(b, c), 0)),
                  full(cw.shape), full(cb.shape), full(dtb.shape), full(alog.shape),
                  full(dexp.shape), full(nw.shape), full(eh.shape)],
        out_specs=pl.BlockSpec((L, SSM_D_INNER), lambda b, c: (row(b, c), 0)),
        out_shape=jax.ShapeDtypeStruct((T, SSM_D_INNER), BF16),
        scratch_shapes=[pltpu.VMEM((2 * L, SSM_D_INNER), BF16), pltpu.VMEM((2 * L, SSM_BC_DIM), BF16),
                        pltpu.VMEM((SSM_GROUPS, SSM_D_STATE, SSM_D_INNER // SSM_GROUPS), F32)],
        compiler_params=_params(("parallel", "arbitrary")), name="ssd")(
            n_out, n_out, n_out, small, cw, cb, dtb, alog, dexp, nw, eh)


def _swa_kernel(sink_ref, q_ref, kp_ref, kc_ref, vtp_ref, vtc_ref, o_ref):
    tq = q_ref.shape[0]
    qt = pl.program_id(1)
    rep = SWA_HEADS // SWA_KV_HEADS
    npair = rep // 2
    nk = 2 * tq
    c_i = lax.broadcasted_iota(jnp.int32, (nk, tq), 0)
    diff = tq + lax.broadcasted_iota(jnp.int32, (nk, tq), 1) - c_i
    first_key = jnp.where(qt > 0, 0, tq)
    bias = jnp.where(diff >= 0, jnp.where(diff < SWA_WINDOW, jnp.where(c_i >= first_key, 0.0, MASK_VALUE),
                                          MASK_VALUE), MASK_VALUE)
    bias = jnp.concatenate([bias] * npair, axis=1)
    kf = jnp.concatenate([kp_ref[...], kc_ref[...]], axis=0).astype(F32)
    ks = pltpu.roll(kf, SWA_HEAD_DIM, 1)
    lane_lo = lax.broadcasted_iota(jnp.int32, (nk, LANES), 1) < SWA_HEAD_DIM
    vt = jnp.concatenate([vtp_ref[0], vtc_ref[0]], axis=1).astype(F32)
    vts = pltpu.roll(vt, SWA_HEAD_DIM, 0)
    row_lo = lax.broadcasted_iota(jnp.int32, (LANES, nk), 0) < SWA_HEAD_DIM
    for g in range(SWA_KV_HEADS):
        k_own, k_swp = (kf, ks) if g == 0 else (ks, kf)
        v_own, v_swp = (vt, vts) if g == 0 else (vts, vt)
        k_e = (jnp.where(lane_lo, k_own, 0.0).astype(BF16), jnp.where(lane_lo, 0.0, k_swp).astype(BF16))
        v_e = (jnp.where(row_lo, v_own, 0.0).astype(BF16), jnp.where(row_lo, 0.0, v_swp).astype(BF16))
        q4 = jnp.concatenate([q_ref[:, (g * npair + pp) * LANES:(g * npair + pp + 1) * LANES]
                              for pp in range(npair)], axis=0)
        o2 = None
        for e in range(2):
            sink = jnp.concatenate([jnp.full((1, tq), sink_ref[g * rep + 2 * pp + e] * LOG2_E, F32)
                                    for pp in range(npair)], axis=1)
            s = _dot_nt(k_e[e], q4) + bias
            m = jnp.maximum(jnp.max(s, axis=0, keepdims=True), sink)
            ex = jnp.exp2(s - m)
            inv = 1.0 / (jnp.sum(ex, axis=0, keepdims=True) + jnp.exp2(sink - m))
            t = _dot(v_e[e], ex.astype(BF16)) * inv
            o2 = t if o2 is None else o2 + t
        for pp in range(npair):
            col = (g * npair + pp) * LANES
            o_ref[:, col:col + LANES] = o2[:, pp * tq:(pp + 1) * tq].T.astype(o_ref.dtype)


def _swa(sinks, c_out, vt, *, batch, seq):
    tq = SWA_WINDOW
    nq = seq // tq
    T = batch * seq
    kcol = SWA_Q_DIM // LANES
    vrow = VT_ROWS // LANES - 1
    cur = lambda b, t: b * nq + t
    prev = lambda b, t: b * nq + jnp.maximum(t - 1, 0)
    return pl.pallas_call(
        _swa_kernel, grid=(batch, nq),
        in_specs=[pl.BlockSpec(memory_space=pltpu.SMEM),
                  pl.BlockSpec((tq, SWA_Q_DIM), lambda b, t: (cur(b, t), 0)),
                  pl.BlockSpec((tq, LANES), lambda b, t: (prev(b, t), kcol)),
                  pl.BlockSpec((tq, LANES), lambda b, t: (cur(b, t), kcol)),
                  pl.BlockSpec((1, LANES, tq), lambda b, t: (b, vrow, jnp.maximum(t - 1, 0))),
                  pl.BlockSpec((1, LANES, tq), lambda b, t: (b, vrow, t))],
        out_specs=pl.BlockSpec((tq, SWA_Q_DIM), lambda b, t: (cur(b, t), 0)),
        out_shape=jax.ShapeDtypeStruct((T, SWA_Q_DIM), BF16),
        compiler_params=_params(("parallel", "arbitrary")), name="swa")(
            sinks, c_out, c_out, c_out, vt, vt)


def _merge_kernel(oa_ref, ob_ref, oc_ref, mg_ref, pa_ref, pb_ref, pc_ref, y_ref):
    D = D_MODEL
    y = _sigmoid(mg_ref[:, 0:D].astype(F32)) * _dot(oa_ref[...], pa_ref[...])
    y += _sigmoid(mg_ref[:, D:2 * D].astype(F32)) * _dot(ob_ref[...], pb_ref[...])
    y += _sigmoid(mg_ref[:, 2 * D:3 * D].astype(F32)) * _dot(oc_ref[...], pc_ref[...])
    y_ref[...] = y.astype(y_ref.dtype)


def _merge(oa, ob, oc, n_out, pa, pb, pc, *, tm):
    T = oa.shape[0]
    rowblk = lambda w: pl.BlockSpec((tm, w), lambda i: (i, 0))
    const = lambda a: pl.BlockSpec(a.shape, lambda i: (0,) * a.ndim, pipeline_mode=pl.Buffered(1))
    return pl.pallas_call(
        _merge_kernel, grid=(T // tm,),
        in_specs=[rowblk(NSA_Q_DIM), rowblk(SSM_D_INNER), rowblk(SWA_Q_DIM), rowblk(3 * D_MODEL),
                  const(pa), const(pb), const(pc)],
        out_specs=rowblk(D_MODEL),
        out_shape=jax.ShapeDtypeStruct((T, D_MODEL), BF16),
        compiler_params=_params(("parallel",)), name="merge")(oa, ob, oc, n_out, pa, pb, pc)


def _outproj_kernel(x_ref, y_ref, wo_ref, nw_ref, wrh_ref, wrl_ref, rb_ref, xo_ref, hn_ref, route_ref):
    x = x_ref[...] + _dot(y_ref[...], wo_ref[...])
    xo_ref[...] = x
    ms = jnp.mean(x * x, axis=-1, keepdims=True)
    hn = x * lax.rsqrt(ms + NORM_EPS) * nw_ref[...]
    hn_hi, hn_lo = _split_bf16(hn, 2)
    bits = pltpu.bitcast(hn_hi.astype(F32), U32)
    for c in range(ROW_CHUNKS):
        lo = lax.shift_right_logical(bits[:, c * LANES:(c + 1) * LANES], U32(16))
        hi = bits[:, (ROW_CHUNKS + c) * LANES:(ROW_CHUNKS + c + 1) * LANES] & U32(0xFFFF0000)
        hn_ref[pl.ds(c, hn.shape[0], stride=ROW_CHUNKS), :] = hi | lo
    logit = (_dot(hn_hi, wrh_ref[...]) + _dot(hn_lo, wrh_ref[...]) + _dot(hn_hi, wrl_ref[...])
             + rb_ref[...])
    tm = logit.shape[0]
    lane = lax.broadcasted_iota(jnp.int32, (tm, LANES), 1)
    big = jnp.int32(LANES)
    gl = jnp.where(lane < MOE_GROUPS, logit, -jnp.inf)
    gmax = jnp.max(gl, axis=-1, keepdims=True)
    gidx = jnp.min(jnp.where(gl == gmax, lane, big), axis=-1, keepdims=True)
    gw = 1.0 / jnp.sum(jnp.exp(gl - gmax), axis=-1, keepdims=True)
    lo = MOE_GROUPS + MOE_EXPERTS_PER_GROUP * gidx
    el = jnp.where((lane >= lo) & (lane < lo + MOE_EXPERTS_PER_GROUP), logit, -jnp.inf)
    m1 = jnp.max(el, axis=-1, keepdims=True)
    i1 = jnp.min(jnp.where(el == m1, lane, big), axis=-1, keepdims=True)
    el2 = jnp.where(lane == i1, -jnp.inf, el)
    m2 = jnp.max(el2, axis=-1, keepdims=True)
    i2 = jnp.min(jnp.where(el2 == m2, lane, big), axis=-1, keepdims=True)
    e2 = jnp.exp(m2 - m1)
    w1 = gw / (1.0 + e2)
    w2 = gw * e2 / (1.0 + e2)
    route = jnp.where(lane == 0, (i1 - MOE_GROUPS).astype(F32),
                      jnp.where(lane == 1, (i2 - MOE_GROUPS).astype(F32),
                                jnp.where(lane == 2, w1, jnp.where(lane == 3, w2, 0.0))))
    route_ref[...] = route


def _outproj(x2d, y, wo, nw, wrh, wrl, rb, *, tm):
    T, D = x2d.shape
    rowblk = lambda w: pl.BlockSpec((tm, w), lambda i: (i, 0))
    const = lambda a: pl.BlockSpec(a.shape, lambda i: (0,) * a.ndim, pipeline_mode=pl.Buffered(1))
    return pl.pallas_call(
        _outproj_kernel, grid=(T // tm,),
        in_specs=[rowblk(D), rowblk(D), const(wo), const(nw), const(wrh), const(wrl), const(rb)],
        out_specs=(rowblk(D), pl.BlockSpec((tm * ROW_CHUNKS, LANES), lambda i: (i, 0)), rowblk(LANES)),
        out_shape=(jax.ShapeDtypeStruct((T, D), F32), jax.ShapeDtypeStruct((T * ROW_CHUNKS, LANES), U32),
                   jax.ShapeDtypeStruct((T, LANES), F32)),
        compiler_params=_params(("parallel",)), name="outproj")(x2d, y, wo, nw, wrh, wrl, rb)


def _expert_kernel(te_ref, tok_ref, tok_next_ref, hn_hbm, wg_ref, wu_ref, wd_ref, y_ref,
                   xbuf, sem, wgu_scr, wd_scr):
    i = pl.program_id(0)
    tm = y_ref.shape[0]
    slot = i % 2

    def token_copy(tok, r, s):
        return pltpu.make_async_copy(
            hn_hbm.at[pl.ds(pl.multiple_of(tok * ROW_CHUNKS, ROW_CHUNKS), ROW_CHUNKS), :],
            xbuf.at[s, pl.ds(pl.multiple_of(r * ROW_CHUNKS, ROW_CHUNKS), ROW_CHUNKS), :], sem.at[s])

    def request(idx_ref, s):
        def body(j, carry):
            for p in range(2):
                r = 2 * j + p
                token_copy(idx_ref[0, 0, r], r, s).start(priority=p)
            return carry
        lax.fori_loop(0, tm // 2, body, 0, unroll=4)

    n_used = te_ref[pl.num_programs(0)]

    @pl.when(i == 0)
    def _():
        request(tok_ref, 0)

    @pl.when(i + 1 < n_used)
    def _():
        request(tok_next_ref, 1 - slot)

    @pl.when(i >= n_used)
    def _():
        y_ref[...] = jnp.zeros(y_ref.shape, y_ref.dtype)

    @pl.when(i < n_used)
    def _():
        @pl.when((i == 0) | (te_ref[i] != te_ref[jnp.maximum(i - 1, 0)]))
        def _():
            wgu_scr[:, :MOE_D_FF] = wg_ref[0].astype(BF16)
            wgu_scr[:, MOE_D_FF:] = wu_ref[0].astype(BF16)
            wd_scr[...] = wd_ref[0].astype(BF16)

        pltpu.make_async_copy(hn_hbm.at[pl.ds(0, tm * ROW_CHUNKS), :], xbuf.at[slot], sem.at[slot]).wait()
        words = [xbuf[slot, pl.ds(c, tm, stride=ROW_CHUNKS), :] for c in range(ROW_CHUNKS)]
        lows = [pltpu.bitcast(lax.shift_left(w, U32(16)), F32).astype(BF16) for w in words]
        highs = [pltpu.bitcast(w & U32(0xFFFF0000), F32).astype(BF16) for w in words]
        x = jnp.concatenate(lows + highs, axis=1)
        gu = _dot(x, wgu_scr[...])
        act = (_silu(gu[:, :MOE_D_FF]) * gu[:, MOE_D_FF:]).astype(BF16)
        y_ref[...] = _dot(act, wd_scr[...]).astype(y_ref.dtype)


def _experts(tile_expert, row_token, hn_rows, wg, wu, wd, *, layer):
    tm = MOE_ROW_TILE
    n_tiles = row_token.shape[0] // tm
    assert tile_expert.shape == (n_tiles + 1,)
    D = D_MODEL
    first = layer * MOE_EXPERTS
    tok3 = row_token.reshape(n_tiles, 1, tm)
    smem_tile = lambda idx: pl.BlockSpec((1, 1, tm), idx, memory_space=pltpu.SMEM)
    grid_spec = pltpu.PrefetchScalarGridSpec(
        num_scalar_prefetch=1, grid=(n_tiles,),
        in_specs=[smem_tile(lambda i, te: (i, 0, 0)),
                  smem_tile(lambda i, te: (jnp.minimum(i + 1, n_tiles - 1), 0, 0)),
                  pl.BlockSpec(memory_space=pl.ANY),
                  pl.BlockSpec((1, D, MOE_D_FF), lambda i, te: (first + te[i], 0, 0)),
                  pl.BlockSpec((1, D, MOE_D_FF), lambda i, te: (first + te[i], 0, 0)),
                  pl.BlockSpec((1, MOE_D_FF, D), lambda i, te: (first + te[i], 0, 0))],
        out_specs=pl.BlockSpec((tm, D), lambda i, te: (i, 0)),
        scratch_shapes=[pltpu.VMEM((2, tm * ROW_CHUNKS, LANES), U32), pltpu.SemaphoreType.DMA((2,)),
                        pltpu.VMEM((D, 2 * MOE_D_FF), BF16), pltpu.VMEM((MOE_D_FF, D), BF16)])
    params = pltpu.CompilerParams(dimension_semantics=("arbitrary",), vmem_limit_bytes=VMEM_LIMIT,
                                  disable_bounds_checks=True)
    return pl.pallas_call(
        _expert_kernel, grid_spec=grid_spec,
        out_shape=jax.ShapeDtypeStruct((n_tiles * tm, D), BF16),
        compiler_params=params, name="experts")(tile_expert, tok3, tok3, hn_rows, wg, wu, wd)


def _combine_kernel(x_ref, y0_ref, y1_ref, route_ref, nw_ref, *o_refs, final):
    r = route_ref[...]
    x = x_ref[...] + r[:, 2:3] * y0_ref[...].astype(F32) + r[:, 3:4] * y1_ref[...].astype(F32)
    ms = jnp.mean(x * x, axis=-1, keepdims=True)
    hn = x * lax.rsqrt(ms + NORM_EPS) * nw_ref[...]
    if final:
        o_refs[0][...] = hn
    else:
        o_refs[0][...] = x
        o_refs[1][...] = hn.astype(BF16)


def _combine(x2d, y0, y1, route, nw, *, final, tm):
    T, D = x2d.shape
    rowblk = lambda w: pl.BlockSpec((tm, w), lambda i: (i, 0))
    out_shape = [jax.ShapeDtypeStruct((T, D), F32)] + ([] if final else [jax.ShapeDtypeStruct((T, D), BF16)])
    return pl.pallas_call(
        functools.partial(_combine_kernel, final=final), grid=(T // tm,),
        in_specs=[rowblk(D), rowblk(D), rowblk(D), rowblk(LANES),
                  pl.BlockSpec((1, D), lambda i: (0, 0))],
        out_specs=tuple(rowblk(D) for _ in out_shape), out_shape=tuple(out_shape),
        compiler_params=_params(("parallel",)), name="combine")(x2d, y0, y1, route, nw)


def _rope_tables(seq):
    def tab(dim):
        inv = 1.0 / (ROPE_THETA ** (jnp.arange(0, dim, 2, dtype=F32) / dim))
        ang = jnp.arange(seq, dtype=F32)[:, None] * inv[None, :]
        return jnp.cos(ang), jnp.sin(ang)
    ca, sa = tab(NSA_HEAD_DIM)
    cc, sc = tab(SWA_HEAD_DIM)
    z = jnp.zeros_like(sc)
    tabs_a = (jnp.concatenate([ca, ca], 1), jnp.concatenate([-sa, sa], 1))
    tabs_c = (jnp.concatenate([cc] * 4, 1), jnp.concatenate([-sc, z, -sc, z], 1),
              jnp.concatenate([z, sc, z, sc], 1))
    return tabs_a, tabs_c


def _overlap_t(seq):
    nch = seq // NSA_CMP_STRIDE
    n_sel = seq // NSA_SLC_BLOCK
    cs = np.arange(nch) * NSA_CMP_STRIDE
    ce = cs + NSA_CMP_LEN - 1
    ss = np.arange(n_sel) * NSA_SLC_BLOCK
    ov = (cs[None, :] <= ss[:, None] + NSA_SLC_BLOCK - 1) & (ce[None, :] >= ss[:, None])
    ov[:, nch - 1] = False
    return jnp.asarray(ov.astype(np.float32), BF16)


def _head_expand():
    e = np.zeros((LANES, SSM_D_INNER), np.float32)
    for h in range(SSM_HEADS):
        e[h, h * SSM_HEAD_DIM:(h + 1) * SSM_HEAD_DIM] = 1.0
    return jnp.asarray(e, BF16)


def _split_w_in(w_in):
    o = np.cumsum([0, NSA_Q_DIM, 1536, 24, SSM_D_INNER, SSM_D_INNER + SSM_BC_DIM, SSM_HEADS,
                   SWA_Q_DIM, 256, 3 * D_MODEL])
    w_t = w_in.T
    seg = lambda a, b: w_t[a:b]
    nsa_q = seg(o[0], o[1])
    kv = o[1]
    cmp_kv, slc_k, slc_v = seg(kv, kv + 512), seg(kv + 512, kv + 768), seg(kv + 768, kv + 1024)
    win_k, win_v = seg(kv + 1024, kv + 1280), seg(kv + 1280, kv + 1536)
    nsa_g = seg(o[2], o[3])
    ssm_z = seg(o[3], o[4])
    ssm_xs, ssm_bc = seg(o[4], o[4] + SSM_D_INNER), seg(o[4] + SSM_D_INNER, o[5])
    ssm_dt = seg(o[5], o[6])
    swa_q = seg(o[6], o[7])
    swa_k, swa_v = seg(o[7], o[7] + 128), seg(o[7] + 128, o[8])
    merge_g = seg(o[8], o[9])
    w_a = jnp.concatenate([nsa_q, slc_k, win_k], 0).astype(BF16)
    w_c = jnp.concatenate([swa_q, swa_k], 0).astype(BF16)
    w_n = jnp.concatenate([merge_g, ssm_z, ssm_xs, ssm_bc, cmp_kv, slc_v, win_v, swa_v], 0).astype(BF16)
    pad = jnp.zeros((LANES - SSM_HEADS - 24, w_in.shape[0]), w_in.dtype)
    w_s = jnp.concatenate([ssm_dt, nsa_g, pad], 0).astype(BF16)
    return w_a, w_c, w_n, w_s


def _pad_lanes(v):
    return jnp.pad(v, (0, LANES - v.shape[0]))[None, :]


VT_ROWS = 2 * NSA_KV_HEADS * LANES + LANES
VT_SEQ_TILE = 512


def _vt_kernel(slc_ref, win_ref, swa_ref, o_ref):
    r = 0
    for ref in (slc_ref, win_ref, swa_ref):
        for c in range(ref.shape[1] // LANES):
            blk = ref[:, c * LANES:(c + 1) * LANES].astype(F32)
            o_ref[0, r:r + LANES, :] = blk.T.astype(o_ref.dtype)
            r += LANES


def _values_transposed(n_out, *, batch, seq):
    ts = VT_SEQ_TILE
    ns = seq // ts
    row = lambda b, s: b * ns + s
    return pl.pallas_call(
        _vt_kernel, grid=(batch, ns),
        in_specs=[pl.BlockSpec((ts, 256), lambda b, s: (row(b, s), N_SLCV // 256)),
                  pl.BlockSpec((ts, 256), lambda b, s: (row(b, s), N_WINV // 256)),
                  pl.BlockSpec((ts, LANES), lambda b, s: (row(b, s), N_SWAV // LANES))],
        out_specs=pl.BlockSpec((1, VT_ROWS, ts), lambda b, s: (b, 0, s)),
        out_shape=jax.ShapeDtypeStruct((batch, VT_ROWS, seq), BF16),
        compiler_params=_params(("parallel", "arbitrary")), name="values_t")(n_out, n_out, n_out)


def _rank_kernel(route_ref, rank_ref, cnt_ref, base_scr):
    tm = route_ref.shape[0]

    @pl.when(pl.program_id(0) == 0)
    def _():
        base_scr[...] = jnp.zeros(base_scr.shape, F32)

    r = route_ref[...]
    lane = lax.broadcasted_iota(jnp.int32, (tm, LANES), 1)
    lanef = lane.astype(F32)
    oh0 = jnp.where(r[:, 0:1] == lanef, 1.0, 0.0)
    oh1 = jnp.where(r[:, 1:2] == lanef, 1.0, 0.0)
    tri = (lax.broadcasted_iota(jnp.int32, (tm, tm), 0)
           >= lax.broadcasted_iota(jnp.int32, (tm, tm), 1)).astype(BF16)
    tot = base_scr[0:1, :] + _dot(tri, (oh0 + oh1).astype(BF16))
    rank0 = jnp.sum(oh0 * (tot - 1.0), axis=-1, keepdims=True)
    rank1 = jnp.sum(oh1 * (tot - 1.0), axis=-1, keepdims=True)
    rank_ref[...] = jnp.where(lane == 0, rank0, jnp.where(lane == 1, rank1, 0.0))
    base_scr[0:1, :] = tot[tm - 1:tm, :]
    cnt_ref[...] = jnp.broadcast_to(tot[tm - 1:tm, :], cnt_ref.shape)


def _rank(route, *, tm):
    T = route.shape[0]
    return pl.pallas_call(
        _rank_kernel, grid=(T // tm,),
        in_specs=[pl.BlockSpec((tm, LANES), lambda i: (i, 0))],
        out_specs=(pl.BlockSpec((tm, LANES), lambda i: (i, 0)), pl.BlockSpec((8, LANES), lambda i: (0, 0))),
        out_shape=(jax.ShapeDtypeStruct((T, LANES), F32), jax.ShapeDtypeStruct((8, LANES), F32)),
        scratch_shapes=[pltpu.VMEM((8, LANES), F32)],
        compiler_params=_params(("arbitrary",)), name="moe_rank")(route)


def _dispatch(route, n_tok):
    tm = MOE_ROW_TILE
    n_asg = n_tok * MOE_TOPK
    n_rows = n_asg + MOE_EXPERTS * tm
    rank, cnt = _rank(route, tm=512)
    counts = cnt[0, :MOE_EXPERTS].astype(jnp.int32)
    padded = ((counts + tm - 1) // tm) * tm
    pend = jnp.cumsum(padded)
    pstart = pend - padded
    start = jnp.cumsum(counts) - counts
    eid = route[:, 0:MOE_TOPK].astype(jnp.int32)
    experts = jnp.arange(MOE_EXPERTS, dtype=jnp.int32)
    pstart_tok = jnp.sum(jnp.where(eid[..., None] == experts, pstart, 0), axis=-1)
    pos = pstart_tok + rank[:, 0:MOE_TOPK].astype(jnp.int32)
    order = jnp.argsort(eid.reshape(-1), stable=True)
    tile_start = jnp.arange(n_rows // tm, dtype=jnp.int32) * tm
    tile_expert = jnp.minimum(jnp.sum((tile_start[:, None] >= pend[None, :]).astype(jnp.int32), axis=1),
                              MOE_EXPERTS - 1)
    per_row = lambda table: jnp.repeat(table[tile_expert], tm)
    k = jnp.arange(n_rows, dtype=jnp.int32) - per_row(pstart)
    valid = k < per_row(counts)
    src = jnp.clip(per_row(start) + jnp.where(valid, k, 0), 0, n_asg - 1)
    row_token = jnp.where(valid, order[src] // MOE_TOPK, 0)
    n_used = (pend[MOE_EXPERTS - 1] // tm).astype(jnp.int32)
    return row_token, pos, jnp.concatenate([tile_expert.astype(jnp.int32), n_used[None]])


def kernel(x, norm_mix, norm_ffn, w_in, nsa_cmp_pos, nsa_cmp_w1, nsa_cmp_b1, nsa_cmp_w2, ssm_conv_w,
           ssm_conv_b, ssm_dt_bias, ssm_a_log, ssm_d, ssm_norm, swa_sinks, proj_nsa, proj_ssm, proj_swa,
           w_out, moe_group_router, moe_group_bias, moe_expert_router, moe_expert_bias, moe_w_gate,
           moe_w_up, moe_w_down, final_norm):
    B, S, D = x.shape
    T = B * S
    depth = w_in.shape[0]
    tm = 512
    tabs_a, tabs_c = _rope_tables(S)
    ovt = _overlap_t(S)
    eh = _head_expand()
    scale_a = jnp.concatenate([jnp.full((NSA_Q_DIM,), NSA_HEAD_DIM ** -0.5 * LOG2_E, F32),
                               jnp.ones((512,), F32)])[None, :]
    scale_c = jnp.concatenate([jnp.full((SWA_Q_DIM,), SWA_HEAD_DIM ** -0.5 * LOG2_E, F32),
                               jnp.ones((LANES,), F32)])[None, :]
    xc = x.reshape(T, D)
    hn_mix = _prenorm(xc, norm_mix[0][None, :], tm=tm)
    for l in range(depth):
        w_a, w_c, w_n, w_s = _split_w_in(w_in[l])
        qr, qp = _inproj(hn_mix, w_a, seq=S, tm=1024, tn=512, rope='a', scale=scale_a, tabs=tabs_a,
                         out_dtypes=(BF16, BF16))
        (c_out,) = _inproj(hn_mix, w_c, seq=S, tm=1024, tn=w_c.shape[0], rope='c', scale=scale_c,
                           tabs=tabs_c)
        (n_out,) = _inproj(hn_mix, w_n, seq=S, tm=1024, tn=1408)
        (small,) = _inproj(hn_mix, w_s, seq=S, tm=1024, tn=LANES, out_dtypes=(F32,))

        w1r = nsa_cmp_w1[l].reshape(2, NSA_CMP_LEN, NSA_HEAD_DIM, NSA_CMP_HIDDEN).astype(BF16)
        kc, vct = _cmp_mlp(n_out, nsa_cmp_pos[l], w1r, nsa_cmp_b1[l][:, None, :],
                           nsa_cmp_w2[l, 0].astype(BF16), nsa_cmp_w2[l, 1].T.astype(BF16), batch=B, seq=S)
        vt = _values_transposed(n_out, batch=B, seq=S)
        o_a = _nsa_attn(qp, qr, small, kc, vct, ovt, vt, batch=B, seq=S)

        dexp = jnp.repeat(ssm_d[l], SSM_HEAD_DIM)[None, :]
        o_b = _ssd(n_out, small, ssm_conv_w[l], ssm_conv_b[l][None, :], _pad_lanes(ssm_dt_bias[l]),
                   _pad_lanes(ssm_a_log[l]), dexp, ssm_norm[l][None, :], eh, batch=B, seq=S)
        o_c = _swa(swa_sinks[l], c_out, vt, batch=B, seq=S)

        y = _merge(o_a, o_b, o_c, n_out, proj_nsa[l].astype(BF16), proj_ssm[l].astype(BF16),
                   proj_swa[l].astype(BF16), tm=tm)
        wr = jnp.pad(jnp.concatenate([moe_group_router[l], moe_expert_router[l]], 1),
                     ((0, 0), (0, LANES - MOE_GROUPS - MOE_EXPERTS)))
        wrh = wr.astype(BF16)
        wrl = (wr - wrh.astype(F32)).astype(BF16)
        rb = _pad_lanes(jnp.concatenate([moe_group_bias[l], moe_expert_bias[l]]))
        x_mid, hn, route = _outproj(xc, y, w_out[l].astype(BF16), norm_ffn[l][None, :], wrh, wrl, rb,
                                    tm=256)

        row_token, pos, tile_expert = _dispatch(route, T)
        ys = _experts(tile_expert, row_token, hn, moe_w_gate.reshape(depth * MOE_EXPERTS, D, MOE_D_FF),
                      moe_w_up.reshape(depth * MOE_EXPERTS, D, MOE_D_FF),
                      moe_w_down.reshape(depth * MOE_EXPERTS, MOE_D_FF, D), layer=l)
        final = l == depth - 1
        nw_next = final_norm if final else norm_mix[l + 1]
        outs = _combine(x_mid, ys[pos[:, 0]], ys[pos[:, 1]], route, nw_next[None, :], final=final, tm=tm)
        xc = outs[0]
        if not final:
            hn_mix = outs[1]
    return xc.reshape(B, S, D)
```

```python
import functools

import jax
import jax.numpy as jnp
import numpy as np
from jax import lax
from jax.experimental import pallas as pl
from jax.experimental.pallas import tpu as pltpu

F32 = jnp.float32
BF16 = jnp.bfloat16

D_MODEL = 2048
ROPE_THETA = 10000.0
NORM_EPS = 1e-6
MASK_VALUE = -1e30
LOG2_E = 1.4426950408889634
FORCE_SCORE = 1e6

NSA_HEADS = 8
NSA_KV_HEADS = 2
NSA_REP = NSA_HEADS // NSA_KV_HEADS
NSA_HEAD_DIM = 128
NSA_CMP_STRIDE = 16
NSA_CMP_LEN = 32
NSA_CMP_HIDDEN = 256
NSA_SLC_BLOCK = 64
NSA_SLC_TOPK = 16
NSA_WINDOW = 512
NSA_Q_DIM = NSA_HEADS * NSA_HEAD_DIM

SSM_D_INNER = 1024
SSM_HEAD_DIM = 64
SSM_HEADS = 16
SSM_GROUPS = 2
SSM_D_STATE = 128
SSM_CONV = 4
SSM_CHUNK = 128
SSM_BC_DIM = 2 * SSM_GROUPS * SSM_D_STATE

SWA_HEADS = 16
SWA_KV_HEADS = 2
SWA_HEAD_DIM = 64
SWA_WINDOW = 128
SWA_Q_DIM = SWA_HEADS * SWA_HEAD_DIM

MOE_GROUPS = 4
MOE_EXPERTS_PER_GROUP = 8
MOE_EXPERTS = MOE_GROUPS * MOE_EXPERTS_PER_GROUP
MOE_TOPK = 2
MOE_D_FF = 512

LANES = 128
ROW_CHUNKS = D_MODEL // LANES // 2
U32 = jnp.uint32
ATT_Q_TILE = 128
SLC_K_TILE = 512
MOE_ROW_TILE = 512
VMEM_LIMIT = 56 * 1024 * 1024

N_MERGE = 0
N_Z = 6144
N_XS = 7168
N_BC = 8192
N_CMP = 8704
N_SLCV = 9216
N_WINV = 9472
N_SWAV = 9728
N_TOTAL = 9856
SMALL_GATE0 = 16


def _params(sem):
    return pltpu.CompilerParams(dimension_semantics=sem, vmem_limit_bytes=VMEM_LIMIT)


def _dot(a, b):
    return jnp.dot(a, b, preferred_element_type=F32)


def _dot_nt(a, b):
    return lax.dot_general(a, b, (((1,), (1,)), ((), ())), preferred_element_type=F32)


def _split_bf16(v, n):
    parts = []
    for _ in range(n):
        p = v.astype(BF16)
        parts.append(p)
        v = v - p.astype(F32)
    return parts


def _expand(v, e):
    hi, lo = _split_bf16(v, 2)
    return _dot(hi, e) + _dot(lo, e)


def _sigmoid(v):
    return 1.0 / (1.0 + jnp.exp(-v))


def _silu(v):
    return v * _sigmoid(v)


def _prenorm_kernel(x_ref, nw_ref, hn_ref):
    x = x_ref[...]
    ms = jnp.mean(x * x, axis=-1, keepdims=True)
    hn_ref[...] = (x * lax.rsqrt(ms + NORM_EPS) * nw_ref[...]).astype(hn_ref.dtype)


def _prenorm(x2d, nw, *, tm):
    T, D = x2d.shape
    return pl.pallas_call(
        _prenorm_kernel, grid=(T // tm,),
        in_specs=[pl.BlockSpec((tm, D), lambda i: (i, 0)), pl.BlockSpec((1, D), lambda i: (0, 0))],
        out_specs=pl.BlockSpec((tm, D), lambda i: (i, 0)),
        out_shape=jax.ShapeDtypeStruct((T, D), BF16),
        compiler_params=_params(("parallel",)), name="prenorm")(x2d, nw)


def _inproj_kernel(*refs, rope, has_scale, n_out):
    it = iter(refs)
    hn_ref, w_ref = next(it), next(it)
    cs_ref = next(it) if has_scale else None
    tabs = [next(it) for _ in range({None: 0, 'a': 2, 'c': 3}[rope])]
    outs = [next(it) for _ in range(n_out)]

    acc = _dot_nt(hn_ref[...], w_ref[...])
    if has_scale:
        acc = acc * cs_ref[...]
    if rope is None:
        outs[0][...] = acc.astype(outs[0].dtype)
        return
    if n_out == 2:
        outs[1][...] = acc.astype(outs[1].dtype)
    for c in range(acc.shape[1] // LANES):
        a = acc[:, c * LANES:(c + 1) * LANES]
        if rope == 'a':
            r = a * tabs[0][...] + pltpu.roll(a, 64, 1) * tabs[1][...]
        else:
            r = (a * tabs[0][...] + pltpu.roll(a, 96, 1) * tabs[1][...]
                 + pltpu.roll(a, 32, 1) * tabs[2][...])
        outs[0][:, c * LANES:(c + 1) * LANES] = r.astype(outs[0].dtype)


def _inproj(hn, w, *, seq, tm, tn, rope=None, scale=None, tabs=(), out_dtypes=(BF16,)):
    T, D = hn.shape
    N = w.shape[0]
    nrow = seq // tm
    in_specs = [pl.BlockSpec((tm, D), lambda j, i: (i, 0)),
                pl.BlockSpec((tn, D), lambda j, i: (j, 0))]
    args = [hn, w]
    if scale is not None:
        in_specs.append(pl.BlockSpec((1, tn), lambda j, i: (0, j)))
        args.append(scale)
    for t in tabs:
        in_specs.append(pl.BlockSpec((tm, LANES), lambda j, i: (i % nrow, 0)))
        args.append(t)
    out_shape = tuple(jax.ShapeDtypeStruct((T, N), dt) for dt in out_dtypes)
    out_specs = tuple(pl.BlockSpec((tm, tn), lambda j, i: (i, j)) for _ in out_dtypes)
    kern = functools.partial(_inproj_kernel, rope=rope, has_scale=scale is not None,
                             n_out=len(out_dtypes))
    return pl.pallas_call(
        kern, grid=(N // tn, T // tm), in_specs=in_specs, out_specs=out_specs, out_shape=out_shape,
        compiler_params=_params(("parallel", "arbitrary")), name="inproj_" + str(rope))(*args)


def _cmp_mlp_kernel(k_ref, v_ref, pos_ref, w1_ref, b1_ref, w2k_ref, w2vt_ref, kc_ref, vct_ref, f32_scr):
    nch = kc_ref.shape[2]
    for c, src in enumerate((k_ref, v_ref)):
        f32_scr[...] = src[...].astype(F32)
        first = jnp.zeros((nch, NSA_CMP_HIDDEN), F32)
        second = jnp.zeros((nch, NSA_CMP_HIDDEN), F32)
        for t in range(NSA_CMP_STRIDE):
            xt = f32_scr[pl.ds(t, nch, stride=NSA_CMP_STRIDE), :]
            first += _dot((xt + pos_ref[c, t:t + 1, :]).astype(BF16), w1_ref[c, t])
            t2 = NSA_CMP_STRIDE + t
            second += _dot((xt + pos_ref[c, t2:t2 + 1, :]).astype(BF16), w1_ref[c, t2])
        hid = _silu(first + pltpu.roll(second, nch - 1, 0) + b1_ref[c]).astype(BF16)
        if c == 0:
            kc_ref[0, 0] = _dot(hid, w2k_ref[...]).astype(kc_ref.dtype)
        else:
            vct_ref[0, 0] = _dot_nt(w2vt_ref[...], hid).astype(vct_ref.dtype)


def _cmp_mlp(n_out, pos, w1r, b1, w2k, w2vt, *, batch, seq):
    nch = seq // NSA_CMP_STRIDE
    cb0 = N_CMP // LANES
    full = lambda a: pl.BlockSpec(a.shape, lambda b, g: (0,) * a.ndim)
    return pl.pallas_call(
        _cmp_mlp_kernel, grid=(batch, NSA_KV_HEADS),
        in_specs=[pl.BlockSpec((seq, LANES), lambda b, g: (b, cb0 + g)),
                  pl.BlockSpec((seq, LANES), lambda b, g: (b, cb0 + NSA_KV_HEADS + g)),
                  full(pos), full(w1r), full(b1), full(w2k), full(w2vt)],
        out_specs=(pl.BlockSpec((1, 1, nch, LANES), lambda b, g: (b, g, 0, 0)),
                   pl.BlockSpec((1, 1, LANES, nch), lambda b, g: (b, g, 0, 0))),
        out_shape=(jax.ShapeDtypeStruct((batch, NSA_KV_HEADS, nch, LANES), BF16),
                   jax.ShapeDtypeStruct((batch, NSA_KV_HEADS, LANES, nch), BF16)),
        scratch_shapes=[pltpu.VMEM((seq, LANES), F32)],
        compiler_params=_params(("parallel", "arbitrary")), name="nsa_cmp_mlp")(
            n_out, n_out, pos, w1r, b1, w2k, w2vt)


def _nsa_attn_kernel(qp_ref, qr_ref, small_ref, kc_ref, vct_ref, ovt_ref, slck_ref, slcvt_ref, wink_ref,
                     winvt_ref, o_ref, q4_scr, sel_scr, m_scr, l_scr, acc_scr, part_scr, *, topk):
    tq = qp_ref.shape[0]
    nch = kc_ref.shape[2]
    n_sel = ovt_ref.shape[0]
    rep = NSA_REP
    tk = SLC_K_TILE
    wlen = NSA_WINDOW + tq
    qt = pl.program_id(1)
    start = qt * tq

    def tile4(v):
        return jnp.concatenate([v] * rep, axis=1)

    def qpos(rows):
        return start + lax.broadcasted_iota(jnp.int32, (rows, tq), 1)

    def sub(rows):
        return lax.broadcasted_iota(jnp.int32, (rows, tq), 0)

    for g in range(NSA_KV_HEADS):
        for r in range(rep):
            h = g * rep + r
            q4_scr[0, g, r * tq:(r + 1) * tq, :] = qp_ref[:, h * LANES:(h + 1) * LANES]
            q4_scr[1, g, r * tq:(r + 1) * tq, :] = qr_ref[:, h * LANES:(h + 1) * LANES]

    gates = _sigmoid(small_ref[...]).T

    def gate4(g, br):
        return jnp.concatenate([gates[SMALL_GATE0 + 3 * (g * rep + r) + br:SMALL_GATE0 + 3 * (g * rep + r) + br + 1, :]
                                for r in range(rep)], axis=1)

    vis = sub(nch) * NSA_CMP_STRIDE + (NSA_CMP_LEN - 1) <= qpos(nch)
    vis_bias = tile4(jnp.where(vis, 0.0, MASK_VALUE))
    vis_one = tile4(jnp.where(vis, 1.0, 0.0))
    jj = sub(n_sel)
    pos_t = qpos(n_sel)
    qblk = pos_t // NSA_SLC_BLOCK
    causal_blk = jj * NSA_SLC_BLOCK <= pos_t
    forced = (jj == 0) | (jj == qblk) | (jj == qblk - 1)
    for g in range(NSA_KV_HEADS):
        s = _dot_nt(kc_ref[0, g], q4_scr[0, g]) + vis_bias
        e = jnp.exp2(s - jnp.max(s, axis=0, keepdims=True))
        p = e * (1.0 / jnp.sum(e, axis=0, keepdims=True)) * vis_one
        part_scr[g] = gate4(g, 0) * _dot(vct_ref[0, g], p.astype(BF16))
        psum = p[:, 0:tq]
        for r in range(1, rep):
            psum = psum + p[:, r * tq:(r + 1) * tq]
        imp = sum(_dot(ovt_ref[...], part) for part in _split_bf16(psum, 3))
        imp = jnp.where(causal_blk, imp, MASK_VALUE)
        imp = jnp.where(forced, FORCE_SCORE, imp)
        groups = [imp[r0:r0 + 8, :] for r0 in range(0, n_sel, 8)]
        ranks = [jnp.zeros((8, tq), F32) for _ in groups]
        for jp in range(n_sel):
            row = imp[jp:jp + 1, :]
            for gi, blk in enumerate(groups):
                r0 = gi * 8
                if r0 > jp:
                    beats = jnp.where(row >= blk, 1.0, 0.0)
                elif r0 + 8 <= jp + 1:
                    beats = jnp.where(row > blk, 1.0, 0.0)
                else:
                    beats = jnp.where(jj[r0:r0 + 8, :] > jp, jnp.where(row >= blk, 1.0, 0.0),
                                      jnp.where(row > blk, 1.0, 0.0))
                ranks[gi] = ranks[gi] + beats
        rank = jnp.concatenate(ranks, axis=0)
        sel_scr[g * n_sel:(g + 1) * n_sel, :] = jnp.where(rank < topk, 1.0, 0.0)

    wstart = pl.multiple_of(jnp.maximum(qt - NSA_WINDOW // tq, 0) * tq, tq)
    diff = qpos(wlen) - (wstart + sub(wlen))
    wbias = tile4(jnp.where(diff >= 0, jnp.where(diff < NSA_WINDOW, 0.0, MASK_VALUE), MASK_VALUE))
    for g in range(NSA_KV_HEADS):
        s = _dot_nt(wink_ref[pl.ds(wstart, wlen), g * LANES:(g + 1) * LANES], q4_scr[1, g]) + wbias
        e = jnp.exp2(s - jnp.max(s, axis=0, keepdims=True))
        o = _dot(winvt_ref[0, g * LANES:(g + 1) * LANES, pl.ds(wstart, wlen)], e.astype(BF16))
        part_scr[g] = part_scr[g] + gate4(g, 2) * (o * (1.0 / jnp.sum(e, axis=0, keepdims=True)))

    m_scr[...] = jnp.full(m_scr.shape, MASK_VALUE, F32)
    l_scr[...] = jnp.zeros(l_scr.shape, F32)
    acc_scr[...] = jnp.zeros(acc_scr.shape, F32)
    blocks_per_tile = tk // NSA_SLC_BLOCK

    def body(kt, carry):
        base = pl.multiple_of(kt * tk, tk)
        causal_bias = jnp.where(base + sub(tk) <= qpos(tk), 0.0, MASK_VALUE)
        for g in range(NSA_KV_HEADS):
            s = _dot_nt(slck_ref[pl.ds(base, tk), g * LANES:(g + 1) * LANES], q4_scr[1, g])
            picked = jnp.concatenate(
                [jnp.broadcast_to(sel_scr[pl.ds(g * n_sel + kt * blocks_per_tile + i, 1), :],
                                  (NSA_SLC_BLOCK, tq)) for i in range(blocks_per_tile)], axis=0)
            s = s + tile4(jnp.where(picked > 0.5, causal_bias, MASK_VALUE))
            m_prev = m_scr[g]
            m_next = jnp.maximum(m_prev, jnp.max(s, axis=0, keepdims=True))
            alpha = jnp.exp2(m_prev - m_next)
            p = jnp.exp2(s - m_next)
            l_scr[g] = alpha * l_scr[g] + jnp.sum(p, axis=0, keepdims=True)
            acc_scr[g] = acc_scr[g] * alpha + _dot(slcvt_ref[0, g * LANES:(g + 1) * LANES, pl.ds(base, tk)],
                                                   p.astype(BF16))
            m_scr[g] = m_next
        return carry

    lax.fori_loop(0, (start + tq + tk - 1) // tk, body, 0)

    for g in range(NSA_KV_HEADS):
        o = part_scr[g] + gate4(g, 1) * (acc_scr[g] * (1.0 / l_scr[g]))
        for r in range(rep):
            h = g * rep + r
            o_ref[:, h * LANES:(h + 1) * LANES] = o[:, r * tq:(r + 1) * tq].T.astype(o_ref.dtype)


def _nsa_attn(qp, qr, small, kc, vct, ovt, vt, *, batch, seq):
    tq = ATT_Q_TILE
    assert seq >= NSA_WINDOW + tq and seq % SLC_K_TILE == 0
    nq = seq // tq
    nch = seq // NSA_CMP_STRIDE
    n_sel = seq // NSA_SLC_BLOCK
    T = batch * seq
    row = lambda b, t: (b * nq + t, 0)
    per_b4 = lambda b, t: (b, 0, 0, 0)
    kcol = NSA_Q_DIM // 256
    rows = tq * NSA_REP
    kern = functools.partial(_nsa_attn_kernel, topk=min(NSA_SLC_TOPK, n_sel))
    return pl.pallas_call(
        kern, grid=(batch, nq),
        in_specs=[pl.BlockSpec((tq, NSA_Q_DIM), row), pl.BlockSpec((tq, NSA_Q_DIM), row),
                  pl.BlockSpec((tq, LANES), row),
                  pl.BlockSpec((1, NSA_KV_HEADS, nch, LANES), per_b4),
                  pl.BlockSpec((1, NSA_KV_HEADS, LANES, nch), per_b4),
                  pl.BlockSpec((n_sel, nch), lambda b, t: (0, 0)),
                  pl.BlockSpec((seq, 256), lambda b, t: (b, kcol)),
                  pl.BlockSpec((1, NSA_KV_HEADS * LANES, seq), lambda b, t: (b, 0, 0)),
                  pl.BlockSpec((seq, 256), lambda b, t: (b, kcol + 1)),
                  pl.BlockSpec((1, NSA_KV_HEADS * LANES, seq), lambda b, t: (b, 1, 0))],
        out_specs=pl.BlockSpec((tq, NSA_Q_DIM), row),
        out_shape=jax.ShapeDtypeStruct((T, NSA_Q_DIM), BF16),
        scratch_shapes=[pltpu.VMEM((2, NSA_KV_HEADS, rows, LANES), BF16),
                        pltpu.VMEM((NSA_KV_HEADS * n_sel, tq), F32),
                        pltpu.VMEM((NSA_KV_HEADS, 1, rows), F32), pltpu.VMEM((NSA_KV_HEADS, 1, rows), F32),
                        pltpu.VMEM((NSA_KV_HEADS, LANES, rows), F32),
                        pltpu.VMEM((NSA_KV_HEADS, LANES, rows), F32)],
        compiler_params=_params(("parallel", "arbitrary")), name="nsa_attn")(
            qp, qr, small, kc, vct, ovt, qr, vt, qr, vt)


def _ssd_kernel(xs_ref, bc_ref, z_ref, small_ref, cw_ref, cb_ref, dtb_ref, alog_ref, dexp_ref,
                nw_ref, eh_ref, o_ref, xs_scr, bc_scr, h_scr):
    L = SSM_CHUNK
    P2 = SSM_D_INNER // SSM_GROUPS
    N = SSM_D_STATE
    c = pl.program_id(1)

    @pl.when(c == 0)
    def _():
        xs_scr[0:8, :] = jnp.zeros((8, SSM_D_INNER), F32)
        bc_scr[0:8, :] = jnp.zeros((8, SSM_BC_DIM), F32)
        h_scr[...] = jnp.zeros(h_scr.shape, F32)

    xs_scr[8:8 + L, :] = xs_ref[...].astype(F32)
    bc_scr[8:8 + L, :] = bc_ref[...].astype(F32)

    def conv(scr, col0, width):
        acc = jnp.zeros((L, width), F32) + cb_ref[:, col0:col0 + width]
        for k in range(SSM_CONV):
            acc += scr[8 - (SSM_CONV - 1) + k:8 - (SSM_CONV - 1) + k + L, :] * cw_ref[k:k + 1, col0:col0 + width]
        return _silu(acc)

    xs = conv(xs_scr, 0, SSM_D_INNER)
    bcm = conv(bc_scr, SSM_D_INNER, SSM_BC_DIM)
    xs_scr[0:8, :] = xs_scr[L:L + 8, :]
    bc_scr[0:8, :] = bc_scr[L:L + 8, :]

    lane = lax.broadcasted_iota(jnp.int32, (L, LANES), 1)
    pre = small_ref[...] + dtb_ref[...]
    dt = jnp.maximum(pre, 0.0) + jnp.log(1.0 + jnp.exp(-jnp.abs(pre)))
    dt = jnp.where(lane < SSM_HEADS, dt, 0.0)
    a = dt * (-jnp.exp(alog_ref[...]))
    tri = (lax.broadcasted_iota(jnp.int32, (L, L), 0)
           >= lax.broadcasted_iota(jnp.int32, (L, L), 1))
    tri_b = tri.astype(BF16)
    a_cs = sum(_dot(tri_b, part) for part in _split_bf16(a, 3))
    a_cs_t = a_cs.T
    a_end = a_cs[L - 1:L, :]
    eh = eh_ref[...]
    dt_x = _expand(dt, eh)
    ea_x = _expand(jnp.exp(a_cs), eh)
    de_x = _expand(jnp.exp(a_end - a_cs), eh)
    cd_x = _expand(jnp.broadcast_to(jnp.exp(a_end), (8, LANES)), eh)[0:1]

    X = xs * dt_x
    Xb = X.astype(BF16)
    Xe = (X * de_x).astype(BF16)
    lane_lo = lax.broadcasted_iota(jnp.int32, (L, LANES), 1) < SSM_HEAD_DIM
    y_parts = []
    for g in range(SSM_GROUPS):
        Bg = bcm[:, g * N:(g + 1) * N]
        Cg = bcm[:, (SSM_GROUPS + g) * N:(SSM_GROUPS + g + 1) * N]
        Cb = Cg.astype(BF16)
        cbm = _dot_nt(Cb, Bg.astype(BF16))
        hT = h_scr[g]
        y_off = _dot(Cb, hT.astype(BF16)) * ea_x[:, g * P2:(g + 1) * P2]
        y_dg = []
        for pp in range(P2 // LANES):
            h0 = g * (SSM_HEADS // SSM_GROUPS) + 2 * pp
            acc = None
            for e in range(2):
                h = h0 + e
                seg = a_cs[:, h:h + 1] - a_cs_t[h:h + 1, :]
                dec = jnp.exp(jnp.where(tri, seg, MASK_VALUE))
                m = (cbm * dec).astype(BF16)
                col = g * P2 + pp * LANES
                xh = jnp.where(lane_lo if e == 0 else jnp.logical_not(lane_lo), Xb[:, col:col + LANES],
                               jnp.zeros((), BF16))
                t = _dot(m, xh)
                acc = t if acc is None else acc + t
            y_dg.append(acc)
        y_parts.append(jnp.concatenate(y_dg, axis=1) + y_off)
        st = _dot(Bg.T.astype(BF16), Xe[:, g * P2:(g + 1) * P2])
        h_scr[g] = hT * cd_x[:, g * P2:(g + 1) * P2] + st
    y = jnp.concatenate(y_parts, axis=1) + xs * dexp_ref[...]
    y = y * _silu(z_ref[...].astype(F32))
    outs = []
    for g in range(SSM_GROUPS):
        yg = y[:, g * P2:(g + 1) * P2]
        outs.append(yg * lax.rsqrt(jnp.mean(yg * yg, axis=-1, keepdims=True) + NORM_EPS))
    o_ref[...] = (jnp.concatenate(outs, axis=1) * nw_ref[...]).astype(o_ref.dtype)


def _ssd(n_out, small, cw, cb, dtb, alog, dexp, nw, eh, *, batch, seq):
    L = SSM_CHUNK
    nc = seq // L
    T = batch * seq
    row = lambda b, c: b * nc + c
    full = lambda shape: pl.BlockSpec(shape, lambda b, c: (0,) * len(shape))
    return pl.pallas_call(
        _ssd_kernel, grid=(batch, nc),
        in_specs=[pl.BlockSpec((L, SSM_D_INNER), lambda b, c: (row(b, c), N_XS // SSM_D_INNER)),
                  pl.BlockSpec((L, SSM_BC_DIM), lambda b, c: (row(b, c), N_BC // SSM_BC_DIM)),
                  pl.BlockSpec((L, SSM_D_INNER), lambda b, c: (row(b, c), N_Z // SSM_D_INNER)),
                  pl.BlockSpec((L, LANES), lambda b, c: (row(b, c), 0)),
                  full(cw.shape), full(cb.shape), full(dtb.shape), full(alog.shape),
                  full(dexp.shape), full(nw.shape), full(eh.shape)],
        out_specs=pl.BlockSpec((L, SSM_D_INNER), lambda b, c: (row(b, c), 0)),
        out_shape=jax.ShapeDtypeStruct((T, SSM_D_INNER), BF16),
        scratch_shapes=[pltpu.VMEM((L + 8, SSM_D_INNER), F32), pltpu.VMEM((L + 8, SSM_BC_DIM), F32),
                        pltpu.VMEM((SSM_GROUPS, SSM_D_STATE, SSM_D_INNER // SSM_GROUPS), F32)],
        compiler_params=_params(("parallel", "arbitrary")), name="ssd")(
            n_out, n_out, n_out, small, cw, cb, dtb, alog, dexp, nw, eh)


def _swa_kernel(sink_ref, q_ref, kp_ref, kc_ref, vtp_ref, vtc_ref, o_ref):
    tq = q_ref.shape[0]
    qt = pl.program_id(1)
    rep = SWA_HEADS // SWA_KV_HEADS
    npair = rep // 2
    nk = 2 * tq
    c_i = lax.broadcasted_iota(jnp.int32, (nk, tq), 0)
    diff = tq + lax.broadcasted_iota(jnp.int32, (nk, tq), 1) - c_i
    first_key = jnp.where(qt > 0, 0, tq)
    bias = jnp.where(diff >= 0, jnp.where(diff < SWA_WINDOW, jnp.where(c_i >= first_key, 0.0, MASK_VALUE),
                                          MASK_VALUE), MASK_VALUE)
    bias = jnp.concatenate([bias] * npair, axis=1)
    kf = jnp.concatenate([kp_ref[...], kc_ref[...]], axis=0).astype(F32)
    ks = pltpu.roll(kf, SWA_HEAD_DIM, 1)
    lane_lo = lax.broadcasted_iota(jnp.int32, (nk, LANES), 1) < SWA_HEAD_DIM
    vt = jnp.concatenate([vtp_ref[0], vtc_ref[0]], axis=1).astype(F32)
    vts = pltpu.roll(vt, SWA_HEAD_DIM, 0)
    row_lo = lax.broadcasted_iota(jnp.int32, (LANES, nk), 0) < SWA_HEAD_DIM
    for g in range(SWA_KV_HEADS):
        k_own, k_swp = (kf, ks) if g == 0 else (ks, kf)
        v_own, v_swp = (vt, vts) if g == 0 else (vts, vt)
        k_e = (jnp.where(lane_lo, k_own, 0.0).astype(BF16), jnp.where(lane_lo, 0.0, k_swp).astype(BF16))
        v_e = (jnp.where(row_lo, v_own, 0.0).astype(BF16), jnp.where(row_lo, 0.0, v_swp).astype(BF16))
        q4 = jnp.concatenate([q_ref[:, (g * npair + pp) * LANES:(g * npair + pp + 1) * LANES]
                              for pp in range(npair)], axis=0)
        o2 = None
        for e in range(2):
            sink = jnp.concatenate([jnp.full((1, tq), sink_ref[g * rep + 2 * pp + e] * LOG2_E, F32)
                                    for pp in range(npair)], axis=1)
            s = _dot_nt(k_e[e], q4) + bias
            m = jnp.maximum(jnp.max(s, axis=0, keepdims=True), sink)
            ex = jnp.exp2(s - m)
            inv = 1.0 / (jnp.sum(ex, axis=0, keepdims=True) + jnp.exp2(sink - m))
            t = _dot(v_e[e], ex.astype(BF16)) * inv
            o2 = t if o2 is None else o2 + t
        for pp in range(npair):
            col = (g * npair + pp) * LANES
            o_ref[:, col:col + LANES] = o2[:, pp * tq:(pp + 1) * tq].T.astype(o_ref.dtype)


def _swa(sinks, c_out, vt, *, batch, seq):
    tq = SWA_WINDOW
    nq = seq // tq
    T = batch * seq
    kcol = SWA_Q_DIM // LANES
    vrow = VT_ROWS // LANES - 1
    cur = lambda b, t: b * nq + t
    prev = lambda b, t: b * nq + jnp.maximum(t - 1, 0)
    return pl.pallas_call(
        _swa_kernel, grid=(batch, nq),
        in_specs=[pl.BlockSpec(memory_space=pltpu.SMEM),
                  pl.BlockSpec((tq, SWA_Q_DIM), lambda b, t: (cur(b, t), 0)),
                  pl.BlockSpec((tq, LANES), lambda b, t: (prev(b, t), kcol)),
                  pl.BlockSpec((tq, LANES), lambda b, t: (cur(b, t), kcol)),
                  pl.BlockSpec((1, LANES, tq), lambda b, t: (b, vrow, jnp.maximum(t - 1, 0))),
                  pl.BlockSpec((1, LANES, tq), lambda b, t: (b, vrow, t))],
        out_specs=pl.BlockSpec((tq, SWA_Q_DIM), lambda b, t: (cur(b, t), 0)),
        out_shape=jax.ShapeDtypeStruct((T, SWA_Q_DIM), BF16),
        compiler_params=_params(("parallel", "arbitrary")), name="swa")(
            sinks, c_out, c_out, c_out, vt, vt)


def _merge_kernel(oa_ref, ob_ref, oc_ref, mg_ref, pa_ref, pb_ref, pc_ref, y_ref):
    D = D_MODEL
    y = _sigmoid(mg_ref[:, 0:D].astype(F32)) * _dot(oa_ref[...], pa_ref[...])
    y += _sigmoid(mg_ref[:, D:2 * D].astype(F32)) * _dot(ob_ref[...], pb_ref[...])
    y += _sigmoid(mg_ref[:, 2 * D:3 * D].astype(F32)) * _dot(oc_ref[...], pc_ref[...])
    y_ref[...] = y.astype(y_ref.dtype)


def _merge(oa, ob, oc, n_out, pa, pb, pc, *, tm):
    T = oa.shape[0]
    rowblk = lambda w: pl.BlockSpec((tm, w), lambda i: (i, 0))
    const = lambda a: pl.BlockSpec(a.shape, lambda i: (0,) * a.ndim, pipeline_mode=pl.Buffered(1))
    return pl.pallas_call(
        _merge_kernel, grid=(T // tm,),
        in_specs=[rowblk(NSA_Q_DIM), rowblk(SSM_D_INNER), rowblk(SWA_Q_DIM), rowblk(3 * D_MODEL),
                  const(pa), const(pb), const(pc)],
        out_specs=rowblk(D_MODEL),
        out_shape=jax.ShapeDtypeStruct((T, D_MODEL), BF16),
        compiler_params=_params(("parallel",)), name="merge")(oa, ob, oc, n_out, pa, pb, pc)


def _outproj_kernel(x_ref, y_ref, wo_ref, nw_ref, wrh_ref, wrl_ref, rb_ref, xo_ref, hn_ref, route_ref):
    x = x_ref[...] + _dot(y_ref[...], wo_ref[...])
    xo_ref[...] = x
    ms = jnp.mean(x * x, axis=-1, keepdims=True)
    hn = x * lax.rsqrt(ms + NORM_EPS) * nw_ref[...]
    hn_hi, hn_lo = _split_bf16(hn, 2)
    bits = pltpu.bitcast(hn_hi.astype(F32), U32)
    for c in range(ROW_CHUNKS):
        lo = lax.shift_right_logical(bits[:, c * LANES:(c + 1) * LANES], U32(16))
        hi = bits[:, (ROW_CHUNKS + c) * LANES:(ROW_CHUNKS + c + 1) * LANES] & U32(0xFFFF0000)
        hn_ref[pl.ds(c, hn.shape[0], stride=ROW_CHUNKS), :] = hi | lo
    logit = (_dot(hn_hi, wrh_ref[...]) + _dot(hn_lo, wrh_ref[...]) + _dot(hn_hi, wrl_ref[...])
             + rb_ref[...])
    tm = logit.shape[0]
    lane = lax.broadcasted_iota(jnp.int32, (tm, LANES), 1)
    big = jnp.int32(LANES)
    gl = jnp.where(lane < MOE_GROUPS, logit, -jnp.inf)
    gmax = jnp.max(gl, axis=-1, keepdims=True)
    gidx = jnp.min(jnp.where(gl == gmax, lane, big), axis=-1, keepdims=True)
    gw = 1.0 / jnp.sum(jnp.exp(gl - gmax), axis=-1, keepdims=True)
    lo = MOE_GROUPS + MOE_EXPERTS_PER_GROUP * gidx
    el = jnp.where((lane >= lo) & (lane < lo + MOE_EXPERTS_PER_GROUP), logit, -jnp.inf)
    m1 = jnp.max(el, axis=-1, keepdims=True)
    i1 = jnp.min(jnp.where(el == m1, lane, big), axis=-1, keepdims=True)
    el2 = jnp.where(lane == i1, -jnp.inf, el)
    m2 = jnp.max(el2, axis=-1, keepdims=True)
    i2 = jnp.min(jnp.where(el2 == m2, lane, big), axis=-1, keepdims=True)
    e2 = jnp.exp(m2 - m1)
    w1 = gw / (1.0 + e2)
    w2 = gw * e2 / (1.0 + e2)
    route = jnp.where(lane == 0, (i1 - MOE_GROUPS).astype(F32),
                      jnp.where(lane == 1, (i2 - MOE_GROUPS).astype(F32),
                                jnp.where(lane == 2, w1, jnp.where(lane == 3, w2, 0.0))))
    route_ref[...] = route


def _outproj(x2d, y, wo, nw, wrh, wrl, rb, *, tm):
    T, D = x2d.shape
    rowblk = lambda w: pl.BlockSpec((tm, w), lambda i: (i, 0))
    const = lambda a: pl.BlockSpec(a.shape, lambda i: (0,) * a.ndim, pipeline_mode=pl.Buffered(1))
    return pl.pallas_call(
        _outproj_kernel, grid=(T // tm,),
        in_specs=[rowblk(D), rowblk(D), const(wo), const(nw), const(wrh), const(wrl), const(rb)],
        out_specs=(rowblk(D), pl.BlockSpec((tm * ROW_CHUNKS, LANES), lambda i: (i, 0)), rowblk(LANES)),
        out_shape=(jax.ShapeDtypeStruct((T, D), F32), jax.ShapeDtypeStruct((T * ROW_CHUNKS, LANES), U32),
                   jax.ShapeDtypeStruct((T, LANES), F32)),
        compiler_params=_params(("parallel",)), name="outproj")(x2d, y, wo, nw, wrh, wrl, rb)


def _expert_kernel(te_ref, tok_ref, tok_next_ref, hn_hbm, wg_ref, wu_ref, wd_ref, y_ref,
                   xbuf, sem, wgu_scr, wd_scr):
    i = pl.program_id(0)
    tm = y_ref.shape[0]
    slot = i % 2

    def token_copy(tok, r, s):
        return pltpu.make_async_copy(
            hn_hbm.at[pl.ds(pl.multiple_of(tok * ROW_CHUNKS, ROW_CHUNKS), ROW_CHUNKS), :],
            xbuf.at[s, pl.ds(pl.multiple_of(r * ROW_CHUNKS, ROW_CHUNKS), ROW_CHUNKS), :], sem.at[s])

    def request(idx_ref, s):
        def body(j, carry):
            for p in range(2):
                r = 2 * j + p
                token_copy(idx_ref[0, 0, r], r, s).start(priority=p)
            return carry
        lax.fori_loop(0, tm // 2, body, 0, unroll=4)

    n_used = te_ref[pl.num_programs(0)]

    @pl.when(i == 0)
    def _():
        request(tok_ref, 0)

    @pl.when(i + 1 < n_used)
    def _():
        request(tok_next_ref, 1 - slot)

    @pl.when(i >= n_used)
    def _():
        y_ref[...] = jnp.zeros(y_ref.shape, y_ref.dtype)

    @pl.when(i < n_used)
    def _():
        @pl.when((i == 0) | (te_ref[i] != te_ref[jnp.maximum(i - 1, 0)]))
        def _():
            wgu_scr[:, :MOE_D_FF] = wg_ref[0].astype(BF16)
            wgu_scr[:, MOE_D_FF:] = wu_ref[0].astype(BF16)
            wd_scr[...] = wd_ref[0].astype(BF16)

        pltpu.make_async_copy(hn_hbm.at[pl.ds(0, tm * ROW_CHUNKS), :], xbuf.at[slot], sem.at[slot]).wait()
        words = [xbuf[slot, pl.ds(c, tm, stride=ROW_CHUNKS), :] for c in range(ROW_CHUNKS)]
        lows = [pltpu.bitcast(lax.shift_left(w, U32(16)), F32).astype(BF16) for w in words]
        highs = [pltpu.bitcast(w & U32(0xFFFF0000), F32).astype(BF16) for w in words]
        x = jnp.concatenate(lows + highs, axis=1)
        gu = _dot(x, wgu_scr[...])
        act = (_silu(gu[:, :MOE_D_FF]) * gu[:, MOE_D_FF:]).astype(BF16)
        y_ref[...] = _dot(act, wd_scr[...]).astype(y_ref.dtype)


def _experts(tile_expert, row_token, hn_rows, wg, wu, wd, *, layer):
    tm = MOE_ROW_TILE
    n_tiles = row_token.shape[0] // tm
    assert tile_expert.shape == (n_tiles + 1,)
    D = D_MODEL
    first = layer * MOE_EXPERTS
    tok3 = row_token.reshape(n_tiles, 1, tm)
    smem_tile = lambda idx: pl.BlockSpec((1, 1, tm), idx, memory_space=pltpu.SMEM)
    grid_spec = pltpu.PrefetchScalarGridSpec(
        num_scalar_prefetch=1, grid=(n_tiles,),
        in_specs=[smem_tile(lambda i, te: (i, 0, 0)),
                  smem_tile(lambda i, te: (jnp.minimum(i + 1, n_tiles - 1), 0, 0)),
                  pl.BlockSpec(memory_space=pl.ANY),
                  pl.BlockSpec((1, D, MOE_D_FF), lambda i, te: (first + te[i], 0, 0)),
                  pl.BlockSpec((1, D, MOE_D_FF), lambda i, te: (first + te[i], 0, 0)),
                  pl.BlockSpec((1, MOE_D_FF, D), lambda i, te: (first + te[i], 0, 0))],
        out_specs=pl.BlockSpec((tm, D), lambda i, te: (i, 0)),
        scratch_shapes=[pltpu.VMEM((2, tm * ROW_CHUNKS, LANES), U32), pltpu.SemaphoreType.DMA((2,)),
                        pltpu.VMEM((D, 2 * MOE_D_FF), BF16), pltpu.VMEM((MOE_D_FF, D), BF16)])
    params = pltpu.CompilerParams(dimension_semantics=("arbitrary",), vmem_limit_bytes=VMEM_LIMIT,
                                  disable_bounds_checks=True)
    return pl.pallas_call(
        _expert_kernel, grid_spec=grid_spec,
        out_shape=jax.ShapeDtypeStruct((n_tiles * tm, D), BF16),
        compiler_params=params, name="experts")(tile_expert, tok3, tok3, hn_rows, wg, wu, wd)


def _combine_kernel(x_ref, y0_ref, y1_ref, route_ref, nw_ref, *o_refs, final):
    r = route_ref[...]
    x = x_ref[...] + r[:, 2:3] * y0_ref[...].astype(F32) + r[:, 3:4] * y1_ref[...].astype(F32)
    ms = jnp.mean(x * x, axis=-1, keepdims=True)
    hn = x * lax.rsqrt(ms + NORM_EPS) * nw_ref[...]
    if final:
        o_refs[0][...] = hn
    else:
        o_refs[0][...] = x
        o_refs[1][...] = hn.astype(BF16)


def _combine(x2d, y0, y1, route, nw, *, final, tm):
    T, D = x2d.shape
    rowblk = lambda w: pl.BlockSpec((tm, w), lambda i: (i, 0))
    out_shape = [jax.ShapeDtypeStruct((T, D), F32)] + ([] if final else [jax.ShapeDtypeStruct((T, D), BF16)])
    return pl.pallas_call(
        functools.partial(_combine_kernel, final=final), grid=(T // tm,),
        in_specs=[rowblk(D), rowblk(D), rowblk(D), rowblk(LANES),
                  pl.BlockSpec((1, D), lambda i: (0, 0))],
        out_specs=tuple(rowblk(D) for _ in out_shape), out_shape=tuple(out_shape),
        compiler_params=_params(("parallel",)), name="combine")(x2d, y0, y1, route, nw)


def _rope_tables(seq):
    def tab(dim):
        inv = 1.0 / (ROPE_THETA ** (jnp.arange(0, dim, 2, dtype=F32) / dim))
        ang = jnp.arange(seq, dtype=F32)[:, None] * inv[None, :]
        return jnp.cos(ang), jnp.sin(ang)
    ca, sa = tab(NSA_HEAD_DIM)
    cc, sc = tab(SWA_HEAD_DIM)
    z = jnp.zeros_like(sc)
    tabs_a = (jnp.concatenate([ca, ca], 1), jnp.concatenate([-sa, sa], 1))
    tabs_c = (jnp.concatenate([cc] * 4, 1), jnp.concatenate([-sc, z, -sc, z], 1),
              jnp.concatenate([z, sc, z, sc], 1))
    return tabs_a, tabs_c


def _overlap_t(seq):
    nch = seq // NSA_CMP_STRIDE
    n_sel = seq // NSA_SLC_BLOCK
    cs = np.arange(nch) * NSA_CMP_STRIDE
    ce = cs + NSA_CMP_LEN - 1
    ss = np.arange(n_sel) * NSA_SLC_BLOCK
    ov = (cs[None, :] <= ss[:, None] + NSA_SLC_BLOCK - 1) & (ce[None, :] >= ss[:, None])
    ov[:, nch - 1] = False
    return jnp.asarray(ov.astype(np.float32), BF16)


def _head_expand():
    e = np.zeros((LANES, SSM_D_INNER), np.float32)
    for h in range(SSM_HEADS):
        e[h, h * SSM_HEAD_DIM:(h + 1) * SSM_HEAD_DIM] = 1.0
    return jnp.asarray(e, BF16)


def _split_w_in(w_in):
    o = np.cumsum([0, NSA_Q_DIM, 1536, 24, SSM_D_INNER, SSM_D_INNER + SSM_BC_DIM, SSM_HEADS,
                   SWA_Q_DIM, 256, 3 * D_MODEL])
    w_t = w_in.T
    seg = lambda a, b: w_t[a:b]
    nsa_q = seg(o[0], o[1])
    kv = o[1]
    cmp_kv, slc_k, slc_v = seg(kv, kv + 512), seg(kv + 512, kv + 768), seg(kv + 768, kv + 1024)
    win_k, win_v = seg(kv + 1024, kv + 1280), seg(kv + 1280, kv + 1536)
    nsa_g = seg(o[2], o[3])
    ssm_z = seg(o[3], o[4])
    ssm_xs, ssm_bc = seg(o[4], o[4] + SSM_D_INNER), seg(o[4] + SSM_D_INNER, o[5])
    ssm_dt = seg(o[5], o[6])
    swa_q = seg(o[6], o[7])
    swa_k, swa_v = seg(o[7], o[7] + 128), seg(o[7] + 128, o[8])
    merge_g = seg(o[8], o[9])
    w_a = jnp.concatenate([nsa_q, slc_k, win_k], 0).astype(BF16)
    w_c = jnp.concatenate([swa_q, swa_k], 0).astype(BF16)
    w_n = jnp.concatenate([merge_g, ssm_z, ssm_xs, ssm_bc, cmp_kv, slc_v, win_v, swa_v], 0).astype(BF16)
    pad = jnp.zeros((LANES - SSM_HEADS - 24, w_in.shape[0]), w_in.dtype)
    w_s = jnp.concatenate([ssm_dt, nsa_g, pad], 0).astype(BF16)
    return w_a, w_c, w_n, w_s


def _pad_lanes(v):
    return jnp.pad(v, (0, LANES - v.shape[0]))[None, :]


VT_ROWS = 2 * NSA_KV_HEADS * LANES + LANES
VT_SEQ_TILE = 512


def _vt_kernel(slc_ref, win_ref, swa_ref, o_ref):
    r = 0
    for ref in (slc_ref, win_ref, swa_ref):
        for c in range(ref.shape[1] // LANES):
            blk = ref[:, c * LANES:(c + 1) * LANES].astype(F32)
            o_ref[0, r:r + LANES, :] = blk.T.astype(o_ref.dtype)
            r += LANES


def _values_transposed(n_out, *, batch, seq):
    ts = VT_SEQ_TILE
    ns = seq // ts
    row = lambda b, s: b * ns + s
    return pl.pallas_call(
        _vt_kernel, grid=(batch, ns),
        in_specs=[pl.BlockSpec((ts, 256), lambda b, s: (row(b, s), N_SLCV // 256)),
                  pl.BlockSpec((ts, 256), lambda b, s: (row(b, s), N_WINV // 256)),
                  pl.BlockSpec((ts, LANES), lambda b, s: (row(b, s), N_SWAV // LANES))],
        out_specs=pl.BlockSpec((1, VT_ROWS, ts), lambda b, s: (b, 0, s)),
        out_shape=jax.ShapeDtypeStruct((batch, VT_ROWS, seq), BF16),
        compiler_params=_params(("parallel", "arbitrary")), name="values_t")(n_out, n_out, n_out)


def _rank_kernel(route_ref, rank_ref, cnt_ref, base_scr):
    tm = route_ref.shape[0]

    @pl.when(pl.program_id(0) == 0)
    def _():
        base_scr[...] = jnp.zeros(base_scr.shape, F32)

    r = route_ref[...]
    lane = lax.broadcasted_iota(jnp.int32, (tm, LANES), 1)
    lanef = lane.astype(F32)
    oh0 = jnp.where(r[:, 0:1] == lanef, 1.0, 0.0)
    oh1 = jnp.where(r[:, 1:2] == lanef, 1.0, 0.0)
    tri = (lax.broadcasted_iota(jnp.int32, (tm, tm), 0)
           >= lax.broadcasted_iota(jnp.int32, (tm, tm), 1)).astype(BF16)
    tot = base_scr[0:1, :] + _dot(tri, (oh0 + oh1).astype(BF16))
    rank0 = jnp.sum(oh0 * (tot - 1.0), axis=-1, keepdims=True)
    rank1 = jnp.sum(oh1 * (tot - 1.0), axis=-1, keepdims=True)
    rank_ref[...] = jnp.where(lane == 0, rank0, jnp.where(lane == 1, rank1, 0.0))
    base_scr[0:1, :] = tot[tm - 1:tm, :]
    cnt_ref[...] = jnp.broadcast_to(tot[tm - 1:tm, :], cnt_ref.shape)


def _rank(route, *, tm):
    T = route.shape[0]
    return pl.pallas_call(
        _rank_kernel, grid=(T // tm,),
        in_specs=[pl.BlockSpec((tm, LANES), lambda i: (i, 0))],
        out_specs=(pl.BlockSpec((tm, LANES), lambda i: (i, 0)), pl.BlockSpec((8, LANES), lambda i: (0, 0))),
        out_shape=(jax.ShapeDtypeStruct((T, LANES), F32), jax.ShapeDtypeStruct((8, LANES), F32)),
        scratch_shapes=[pltpu.VMEM((8, LANES), F32)],
        compiler_params=_params(("arbitrary",)), name="moe_rank")(route)


def _dispatch(route, n_tok):
    tm = MOE_ROW_TILE
    n_asg = n_tok * MOE_TOPK
    n_rows = n_asg + MOE_EXPERTS * tm
    rank, cnt = _rank(route, tm=512)
    counts = cnt[0, :MOE_EXPERTS].astype(jnp.int32)
    padded = ((counts + tm - 1) // tm) * tm
    pend = jnp.cumsum(padded)
    pstart = pend - padded
    start = jnp.cumsum(counts) - counts
    eid = route[:, 0:MOE_TOPK].astype(jnp.int32)
    experts = jnp.arange(MOE_EXPERTS, dtype=jnp.int32)
    pstart_tok = jnp.sum(jnp.where(eid[..., None] == experts, pstart, 0), axis=-1)
    pos = pstart_tok + rank[:, 0:MOE_TOPK].astype(jnp.int32)
    order = jnp.argsort(eid.reshape(-1), stable=True)
    tile_start = jnp.arange(n_rows // tm, dtype=jnp.int32) * tm
    tile_expert = jnp.minimum(jnp.sum((tile_start[:, None] >= pend[None, :]).astype(jnp.int32), axis=1),
                              MOE_EXPERTS - 1)
    per_row = lambda table: jnp.repeat(table[tile_expert], tm)
    k = jnp.arange(n_rows, dtype=jnp.int32) - per_row(pstart)
    valid = k < per_row(counts)
    src = jnp.clip(per_row(start) + jnp.where(valid, k, 0), 0, n_asg - 1)
    row_token = jnp.where(valid, order[src] // MOE_TOPK, 0)
    n_used = (pend[MOE_EXPERTS - 1] // tm).astype(jnp.int32)
    return row_token, pos, jnp.concatenate([tile_expert.astype(jnp.int32), n_used[None]])


def kernel(x, norm_mix, norm_ffn, w_in, nsa_cmp_pos, nsa_cmp_w1, nsa_cmp_b1, nsa_cmp_w2, ssm_conv_w,
           ssm_conv_b, ssm_dt_bias, ssm_a_log, ssm_d, ssm_norm, swa_sinks, proj_nsa, proj_ssm, proj_swa,
           w_out, moe_group_router, moe_group_bias, moe_expert_router, moe_expert_bias, moe_w_gate,
           moe_w_up, moe_w_down, final_norm):
    B, S, D = x.shape
    T = B * S
    depth = w_in.shape[0]
    tm = 512
    tabs_a, tabs_c = _rope_tables(S)
    ovt = _overlap_t(S)
    eh = _head_expand()
    scale_a = jnp.concatenate([jnp.full((NSA_Q_DIM,), NSA_HEAD_DIM ** -0.5 * LOG2_E, F32),
                               jnp.ones((512,), F32)])[None, :]
    scale_c = jnp.concatenate([jnp.full((SWA_Q_DIM,), SWA_HEAD_DIM ** -0.5 * LOG2_E, F32),
                               jnp.ones((LANES,), F32)])[None, :]
    xc = x.reshape(T, D)
    hn_mix = _prenorm(xc, norm_mix[0][None, :], tm=tm)
    for l in range(depth):
        w_a, w_c, w_n, w_s = _split_w_in(w_in[l])
        qr, qp = _inproj(hn_mix, w_a, seq=S, tm=1024, tn=768, rope='a', scale=scale_a, tabs=tabs_a,
                         out_dtypes=(BF16, BF16))
        (c_out,) = _inproj(hn_mix, w_c, seq=S, tm=1024, tn=w_c.shape[0], rope='c', scale=scale_c,
                           tabs=tabs_c)
        (n_out,) = _inproj(hn_mix, w_n, seq=S, tm=1024, tn=1408)
        (small,) = _inproj(hn_mix, w_s, seq=S, tm=1024, tn=LANES, out_dtypes=(F32,))

        w1r = nsa_cmp_w1[l].reshape(2, NSA_CMP_LEN, NSA_HEAD_DIM, NSA_CMP_HIDDEN).astype(BF16)
        kc, vct = _cmp_mlp(n_out, nsa_cmp_pos[l], w1r, nsa_cmp_b1[l][:, None, :],
                           nsa_cmp_w2[l, 0].astype(BF16), nsa_cmp_w2[l, 1].T.astype(BF16), batch=B, seq=S)
        vt = _values_transposed(n_out, batch=B, seq=S)
        o_a = _nsa_attn(qp, qr, small, kc, vct, ovt, vt, batch=B, seq=S)

        dexp = jnp.repeat(ssm_d[l], SSM_HEAD_DIM)[None, :]
        o_b = _ssd(n_out, small, ssm_conv_w[l], ssm_conv_b[l][None, :], _pad_lanes(ssm_dt_bias[l]),
                   _pad_lanes(ssm_a_log[l]), dexp, ssm_norm[l][None, :], eh, batch=B, seq=S)
        o_c = _swa(swa_sinks[l], c_out, vt, batch=B, seq=S)

        y = _merge(o_a, o_b, o_c, n_out, proj_nsa[l].astype(BF16), proj_ssm[l].astype(BF16),
                   proj_swa[l].astype(BF16), tm=tm)
        wr = jnp.pad(jnp.concatenate([moe_group_router[l], moe_expert_router[l]], 1),
                     ((0, 0), (0, LANES - MOE_GROUPS - MOE_EXPERTS)))
        wrh = wr.astype(BF16)
        wrl = (wr - wrh.astype(F32)).astype(BF16)
        rb = _pad_lanes(jnp.concatenate([moe_group_bias[l], moe_expert_bias[l]]))
        x_mid, hn, route = _outproj(xc, y, w_out[l].astype(BF16), norm_ffn[l][None, :], wrh, wrl, rb,
                                    tm=tm)

        row_token, pos, tile_expert = _dispatch(route, T)
        ys = _experts(tile_expert, row_token, hn, moe_w_gate.reshape(depth * MOE_EXPERTS, D, MOE_D_FF),
                      moe_w_up.reshape(depth * MOE_EXPERTS, D, MOE_D_FF),
                      moe_w_down.reshape(depth * MOE_EXPERTS, MOE_D_FF, D), layer=l)
        final = l == depth - 1
        nw_next = final_norm if final else norm_mix[l + 1]
        outs = _combine(x_mid, ys[pos[:, 0]], ys[pos[:, 1]], route, nw_next[None, :], final=final, tm=tm)
        xc = outs[0]
        if not final:
            hn_mix = outs[1]
    return xc.reshape(B, S, D)
```
